```python
import math
import jax
import jax.numpy as jnp
from jax import lax
import numpy as np

D_MODEL = 1024
BATCH = 4
SEQ = 8192
DEPTH = 4

GRID_W = 64
CTX_LEN = 256
NORM_EPS = 1e-6
F32 = jnp.float32

HY_W = 512
HY_ORDER = 2
HY_SHORT = 3
HY_BANDS = 16
HY_EMB = 2 * HY_BANDS + 1
HY_FFN = 64
HY_TARGET = 1e-2
HY_FAST_RATE = math.log(HY_TARGET) / 0.3
HY_SLOW_RATE = math.log(HY_TARGET) / 1.5

RG_W = 512
RG_BLOCKS = 8
RG_BS = RG_W // RG_BLOCKS
RG_CONV = 4
RG_C = 8.0

GLA_HEADS = 4
GLA_DK = 64
GLA_DV = 128
GLA_QK = GLA_HEADS * GLA_DK
GLA_V = GLA_HEADS * GLA_DV
GLA_RANK = 16
GLA_TAU = 16.0
GLA_CHUNK = 64

N_BRANCH = 3
IN_GROUPS = ((HY_ORDER + 1) * HY_W, RG_W, RG_W, 2 * GLA_QK + GLA_V + 2 * GLA_RANK, GLA_V, N_BRANCH * D_MODEL)
IN_DIM = sum(IN_GROUPS)

PEER_HEADS = 8
PEER_NKEYS = 128
PEER_EXPERTS = PEER_NKEYS * PEER_NKEYS
PEER_DKEY = 256
PEER_TOPK = 16
PEER_BLOCK = 128

kernel_name = 'hybrid_hyena_rglru_gla_peer_dit'


def rmsnorm(x, g):
    xf = x.astype(F32)
    y = xf * lax.rsqrt(jnp.mean(xf * xf, axis=-1, keepdims=True) + NORM_EPS)
    return (y * g.astype(F32)).astype(x.dtype)


def modulate(h, shift, scale):
    return h * (1.0 + scale) + shift


def flip_seq(t, d):
    return t[:, ::-1] if d else t


def raster_to_column(t):
    b_, n = t.shape[:2]
    rows = n // GRID_W
    return t.reshape(b_, rows, GRID_W, *t.shape[2:]).swapaxes(1, 2).reshape(t.shape)


def column_to_raster(t):
    b_, n = t.shape[:2]
    rows = n // GRID_W
    return t.reshape(b_, GRID_W, rows, *t.shape[2:]).swapaxes(1, 2).reshape(t.shape)


def depthwise_conv(x, w, b, left):
    width = w.shape[0]
    n = x.shape[1]
    xp = jnp.pad(x, ((0, 0), (left, width - 1 - left), (0, 0)))
    y = b
    for j in range(width):
        y = y + w[j] * xp[:, j:j + n]
    return y


def linear_scan(a, b, h0):
    def combine(lft, rgt):
        return lft[0] * rgt[0], rgt[0] * lft[1] + rgt[1]
    a_cum, b_cum = lax.associative_scan(combine, (a, b), axis=1)
    h = a_cum * h0[:, None] + b_cum
    return h, h[:, -1]


def hyena_filter_spectra(n, w1, b1, w2, b2, w3, freq):
    idx = jnp.arange(n, dtype=F32)
    tn = idx / (n - 1)
    bands = jnp.linspace(1e-4, HY_BANDS - 1, HY_BANDS, dtype=F32)
    ang = (2.0 * math.pi / n) * idx[:, None] * bands[None, :]
    feats = jnp.concatenate([tn[:, None], jnp.cos(ang), -jnp.sin(ang)], axis=-1)
    fr = freq.astype(F32)
    h = jnp.sin(fr * (feats @ w1.astype(F32) + b1.astype(F32)))
    h = jnp.sin(fr * (h @ w2.astype(F32) + b2.astype(F32)))
    h = (h @ w3.astype(F32)).reshape(n, HY_ORDER, 2, HY_W)
    deltas = jnp.abs(jnp.linspace(HY_FAST_RATE, HY_SLOW_RATE, HY_W, dtype=F32))
    window = jnp.exp(-tn[:, None] * deltas[None, :])
    h = h * window[:, None, None, :]
    h = h / (jnp.sum(jnp.abs(h), axis=(0, 2), keepdims=True) + 1e-6)
    h_fwd, h_bwd = h[:, :, 0], h[:, :, 1]
    taps = jnp.concatenate([h_fwd[:1] + h_bwd[:1], h_fwd[1:], jnp.zeros_like(h_fwd[:1]), h_bwd[:0:-1]], axis=0)
    return jnp.fft.rfft(taps, axis=0)


def fft_long_conv(u, spec, skip):
    n = u.shape[1]
    uf = jnp.fft.rfft(u, n=2 * n, axis=1)
    y = jnp.fft.irfft(uf * spec[None], n=2 * n, axis=1)[:, :n]
    return y + u * skip


def hyena_mix(u, conv_w, conv_b, w1, b1, w2, b2, w3, freq, skip):
    dt = u.dtype
    n = u.shape[1]
    u = depthwise_conv(u, conv_w, conv_b, left=(HY_SHORT - 1) // 2).astype(F32)
    v, x1, x2 = jnp.split(u, 3, axis=-1)
    spec = hyena_filter_spectra(n, w1, b1, w2, b2, w3, freq)
    skip = skip.astype(F32)
    z = x1 * fft_long_conv(v, spec[:, 0], skip[0])
    z = x2 * fft_long_conv(z, spec[:, 1], skip[1])
    return z.astype(dt)


def rglru_scan(u, h0, conv_w, conv_b, wa, ba, wx, bx, lam):
    b_, n, _ = u.shape
    xc = depthwise_conv(u, conv_w, conv_b, left=RG_CONV - 1)
    xb = xc.reshape(b_, n, RG_BLOCKS, RG_BS)
    gate_r = jax.nn.sigmoid((jnp.einsum('blgi,gij->blgj', xb, wa).reshape(b_, n, RG_W) + ba).astype(F32))
    gate_i = jax.nn.sigmoid((jnp.einsum('blgi,gij->blgj', xb, wx).reshape(b_, n, RG_W) + bx).astype(F32))
    log_a = -RG_C * gate_r * jax.nn.softplus(-lam.astype(F32))
    a = jnp.exp(log_a)
    b = jnp.sqrt(-jnp.expm1(2.0 * log_a)) * (gate_i * xc.astype(F32))
    return linear_scan(a, b, h0)


def rglru_mix(u_ctx, g_ctx, u_lat, g_lat, conv_w, conv_b, wa, ba, wx, bx, lam, need_ctx):
    h0 = jnp.zeros((u_lat.shape[0], RG_W), F32)
    h_ctx = 0.0
    h_lat = 0.0
    for d in range(2):
        args = (conv_w[d], conv_b[d], wa[d], ba[d], wx[d], bx[d], lam[d])
        hc, hc_last = rglru_scan(flip_seq(u_ctx, d), h0, *args)
        hl, _ = rglru_scan(flip_seq(u_lat, d), hc_last, *args)
        h_ctx = h_ctx + flip_seq(hc, d)
        h_lat = h_lat + flip_seq(hl, d)
    y_lat = h_lat.astype(u_lat.dtype) * jax.nn.gelu(g_lat)
    y_ctx = h_ctx.astype(u_ctx.dtype) * jax.nn.gelu(g_ctx) if need_ctx else None
    return y_ctx, y_lat


def gla_chunked(q, k, v, log_a, s0):
    b_, n, nh, _ = q.shape
    nc = n // GLA_CHUNK

    def blocks(t):
        return t.reshape(b_, nc, GLA_CHUNK, nh, t.shape[-1]).transpose(0, 3, 1, 2, 4)

    q, k, v, log_a = blocks(q), blocks(k), blocks(v), blocks(log_a)
    cum = jnp.cumsum(log_a, axis=3)
    cum_last = cum[:, :, :, -1:]
    qg = q * jnp.exp(cum)
    scores = jnp.einsum('bhncd,bhnsd->bhncs', qg, k * jnp.exp(-cum))
    lower = jnp.tril(jnp.ones((GLA_CHUNK, GLA_CHUNK), dtype=bool))
    scores = jnp.where(lower, scores, 0.0)
    o = jnp.einsum('bhncs,bhnse->bhnce', scores, v)
    ds = jnp.einsum('bhncd,bhnce->bhnde', k * jnp.exp(cum_last - cum), v)
    decay = jnp.exp(cum_last[:, :, :, 0])

    def combine(lft, rgt):
        return lft[0] * rgt[0], rgt[0][..., None] * lft[1] + rgt[1]

    d_cum, s_cum = lax.associative_scan(combine, (decay, ds), axis=2)
    s_end = d_cum[..., None] * s0[:, :, None] + s_cum
    s_start = jnp.concatenate([s0[:, :, None], s_end[:, :, :-1]], axis=2)
    o = o + jnp.einsum('bhncd,bhnde->bhnce', qg, s_start)
    o = o.transpose(0, 2, 3, 1, 4).reshape(b_, n, nh, GLA_DV)
    return o, s_end[:, :, -1]


def gla_prepare(t, w_lr, b_lr):
    b_, n, _ = t.shape
    t = t.astype(F32)
    q, k, v, lr = jnp.split(t, [GLA_QK, 2 * GLA_QK, 2 * GLA_QK + GLA_V], axis=-1)
    q = q.reshape(b_, n, GLA_HEADS, GLA_DK) * (GLA_DK ** -0.5)
    k = k.reshape(b_, n, GLA_HEADS, GLA_DK)
    v = v.reshape(b_, n, GLA_HEADS, GLA_DV)
    lr = lr.reshape(b_, n, 2, GLA_RANK)
    logits = jnp.einsum('bldr,drk->bldk', lr, w_lr.astype(F32)) + b_lr.astype(F32)
    log_a = (jax.nn.log_sigmoid(logits) / GLA_TAU).reshape(b_, n, 2, GLA_HEADS, GLA_DK)
    return q, k, v, log_a


def gla_mix(t_ctx, t_lat, w_lr, b_lr):
    qc, kc, vc, lac = gla_prepare(t_ctx, w_lr, b_lr)
    ql, kl, vl, lal = gla_prepare(t_lat, w_lr, b_lr)
    s0 = jnp.zeros((qc.shape[0], GLA_HEADS, GLA_DK, GLA_DV), F32)
    o_ctx = -jnp.sum(qc * kc, axis=-1, keepdims=True) * vc
    o_lat = -jnp.sum(ql * kl, axis=-1, keepdims=True) * vl
    for d in range(2):
        oc, sc = gla_chunked(flip_seq(qc, d), flip_seq(kc, d), flip_seq(vc, d), flip_seq(lac[:, :, d], d), s0)
        ol, _ = gla_chunked(flip_seq(ql, d), flip_seq(kl, d), flip_seq(vl, d), flip_seq(lal[:, :, d], d), sc)
        o_ctx = o_ctx + flip_seq(oc, d)
        o_lat = o_lat + flip_seq(ol, d)
    return o_ctx, o_lat


def gla_finish(o, g, norm_g):
    on = o * lax.rsqrt(jnp.mean(o * o, axis=-1, keepdims=True) + NORM_EPS) * norm_g.astype(F32)
    return on.reshape(g.shape).astype(g.dtype) * jax.nn.silu(g)


def merge_branches(gate_logits, y_hy, y_rg, y_gla, lp):
    g = jax.nn.sigmoid(gate_logits + lp['b_merge'])
    g_hy, g_rg, g_gla = jnp.split(g, N_BRANCH, axis=-1)
    m = g_hy * (y_hy @ lp['w_hy_o']) + g_rg * (y_rg @ lp['w_rg_o']) + g_gla * (y_gla @ lp['w_gla_o'])
    return m @ lp['w_out']


def token_mixer(h_ctx, h_lat, lp, need_ctx):
    pts = [int(p) for p in np.cumsum(IN_GROUPS)[:-1]]
    hy_c, rgx_c, rgg_c, gla_c, glag_c, mg_c = jnp.split(h_ctx @ lp['w_in'], pts, axis=-1)
    hy_l, rgx_l, rgg_l, gla_l, glag_l, mg_l = jnp.split(h_lat @ lp['w_in'], pts, axis=-1)
    hy_args = (lp['hy_conv_w'], lp['hy_conv_b'], lp['hy_w1'], lp['hy_b1'], lp['hy_w2'], lp['hy_b2'],
               lp['hy_w3'], lp['hy_freq'], lp['hy_skip'])
    y_hy_l = hyena_mix(hy_l, *hy_args)
    y_rg_c, y_rg_l = rglru_mix(rgx_c, rgg_c, rgx_l, rgg_l, lp['rg_conv_w'], lp['rg_conv_b'], lp['rg_wa'],
                               lp['rg_ba'], lp['rg_wx'], lp['rg_bx'], lp['rg_lambda'], need_ctx)
    o_gla_c, o_gla_l = gla_mix(gla_c, raster_to_column(gla_l), lp['gla_w_lr'], lp['gla_b_lr'])
    y_gla_l = gla_finish(column_to_raster(o_gla_l), glag_l, lp['gla_norm_g'])
    out_lat = merge_branches(mg_l, y_hy_l, y_rg_l, y_gla_l, lp)
    if not need_ctx:
        return None, out_lat
    y_hy_c = hyena_mix(hy_c, *hy_args)
    y_gla_c = gla_finish(o_gla_c, glag_c, lp['gla_norm_g'])
    out_ctx = merge_branches(mg_c, y_hy_c, y_rg_c, y_gla_c, lp)
    return out_ctx, out_lat


def peer_ffn(h, wq, keys, u_tab, v_tab):
    b_, n, dm = h.shape
    token_blocks = h.reshape(b_ * n // PEER_BLOCK, PEER_BLOCK, dm)
    keys = keys.astype(F32)

    def block(xb):
        q = (xb @ wq).astype(F32).reshape(PEER_BLOCK, PEER_HEADS, 2, PEER_DKEY // 2)
        s = jnp.einsum('thpc,hpnc->thpn', q, keys)
        s_top, i_top = lax.top_k(s, PEER_TOPK)
        cand = (s_top[:, :, 0, :, None] + s_top[:, :, 1, None, :]).reshape(PEER_BLOCK, PEER_HEADS, -1)
        cand_idx = (i_top[:, :, 0, :, None] * PEER_NKEYS + i_top[:, :, 1, None, :]).reshape(PEER_BLOCK, PEER_HEADS, -1)
        best, pos = lax.top_k(cand, PEER_TOPK)
        expert = jnp.take_along_axis(cand_idx, pos, axis=-1).reshape(PEER_BLOCK, -1)
        weight = jax.nn.softmax(best, axis=-1).reshape(PEER_BLOCK, -1)
        u = jnp.take(u_tab, expert, axis=0)
        v = jnp.take(v_tab, expert, axis=0)
        act = jax.nn.gelu(jnp.einsum('td,ted->te', xb, u).astype(F32))
        return jnp.einsum('te,ted->td', (weight * act).astype(v.dtype), v)

    return lax.map(block, token_blocks).reshape(b_, n, dm)


def setup_inputs(seed: int = 0) -> dict:
    key = jax.random.key(seed)
    keys = jax.random.split(key, 48)
    counter = [0]

    def nrm(shape, scale):
        k = keys[counter[0]]
        counter[0] += 1
        return jax.random.normal(k, shape, F32) * scale

    def gain(shape):
        return 1.0 + nrm(shape, 0.05)

    d = D_MODEL
    lam_u = jax.random.uniform(keys[47], (DEPTH, 2, RG_W), F32, 0.9, 0.999)
    lam_a = lam_u ** (1.0 / RG_C)
    rg_lambda = jnp.log(lam_a) - jnp.log1p(-lam_a)
    return {
        'x': nrm((BATCH, SEQ, d), 1.0),
        'c': nrm((BATCH, d), 1.0),
        'ctx': nrm((BATCH, CTX_LEN, d), 1.0),
        'c_ctx': nrm((d,), 1.0),
        'w_mod': nrm((DEPTH, d, 6 * d), 0.5 * d ** -0.5),
        'b_mod': nrm((DEPTH, 6 * d), 0.02),
        'g_norm_mix': gain((DEPTH, d)),
        'g_norm_ffn': gain((DEPTH, d)),
        'w_in': nrm((DEPTH, d, IN_DIM), d ** -0.5),
        'hy_conv_w': nrm((DEPTH, HY_SHORT, (HY_ORDER + 1) * HY_W), 0.5),
        'hy_conv_b': nrm((DEPTH, (HY_ORDER + 1) * HY_W), 0.02),
        'hy_w1': nrm((DEPTH, HY_EMB, HY_FFN), HY_EMB ** -0.5),
        'hy_b1': nrm((DEPTH, HY_FFN), 0.1),
        'hy_w2': nrm((DEPTH, HY_FFN, HY_FFN), HY_FFN ** -0.5),
        'hy_b2': nrm((DEPTH, HY_FFN), 0.1),
        'hy_w3': nrm((DEPTH, HY_FFN, HY_ORDER * 2 * HY_W), HY_FFN ** -0.5),
        'hy_freq': gain((DEPTH, HY_FFN)),
        'hy_skip': nrm((DEPTH, HY_ORDER, HY_W), 0.1),
        'rg_conv_w': nrm((DEPTH, 2, RG_CONV, RG_W), 0.5),
        'rg_conv_b': nrm((DEPTH, 2, RG_W), 0.02),
        'rg_wa': nrm((DEPTH, 2, RG_BLOCKS, RG_BS, RG_BS), RG_BS ** -0.5),
        'rg_ba': nrm((DEPTH, 2, RG_W), 0.1),
        'rg_wx': nrm((DEPTH, 2, RG_BLOCKS, RG_BS, RG_BS), RG_BS ** -0.5),
        'rg_bx': nrm((DEPTH, 2, RG_W), 0.1),
        'rg_lambda': rg_lambda,
        'gla_w_lr': nrm((DEPTH, 2, GLA_RANK, GLA_QK), GLA_RANK ** -0.5),
        'gla_b_lr': nrm((DEPTH, 2, GLA_QK), 0.1),
        'gla_norm_g': gain((DEPTH, GLA_DV)),
        'w_hy_o': nrm((DEPTH, HY_W, d), HY_W ** -0.5),
        'w_rg_o': nrm((DEPTH, RG_W, d), RG_W ** -0.5),
        'w_gla_o': nrm((DEPTH, GLA_V, d), GLA_V ** -0.5),
        'b_merge': nrm((DEPTH, N_BRANCH * d), 0.02),
        'w_out': nrm((DEPTH, d, d), d ** -0.5),
        'peer_wq': nrm((DEPTH, d, PEER_HEADS * PEER_DKEY), d ** -0.5),
        'peer_keys': nrm((DEPTH, PEER_HEADS, 2, PEER_NKEYS, PEER_DKEY // 2), (PEER_DKEY // 2) ** -0.5),
        'peer_u': nrm((DEPTH, PEER_EXPERTS, d), d ** -0.5),
        'peer_v': nrm((DEPTH, PEER_EXPERTS, d), 0.5),
        'g_final': gain((d,)),
    }


def reference(x, c, ctx, c_ctx, w_mod, b_mod, g_norm_mix, g_norm_ffn, w_in, hy_conv_w, hy_conv_b,
              hy_w1, hy_b1, hy_w2, hy_b2, hy_w3, hy_freq, hy_skip, rg_conv_w, rg_conv_b, rg_wa, rg_ba,
              rg_wx, rg_bx, rg_lambda, gla_w_lr, gla_b_lr, gla_norm_g, w_hy_o, w_rg_o, w_gla_o, b_merge,
              w_out, peer_wq, peer_keys, peer_u, peer_v, g_final):
    cond_lat = jax.nn.silu(c)[:, None, :]
    cond_ctx = jax.nn.silu(c_ctx)[None, None, :]
    x_lat, x_ctx = x, ctx
    for l in range(DEPTH):
        need_ctx = l < DEPTH - 1
        sh1, sc1, gt1, sh2, sc2, gt2 = jnp.split(cond_lat @ w_mod[l] + b_mod[l], 6, axis=-1)
        csh1, csc1, cgt1, csh2, csc2, cgt2 = jnp.split(cond_ctx @ w_mod[l] + b_mod[l], 6, axis=-1)
        lp = {
            'w_in': w_in[l], 'hy_conv_w': hy_conv_w[l], 'hy_conv_b': hy_conv_b[l],
            'hy_w1': hy_w1[l], 'hy_b1': hy_b1[l], 'hy_w2': hy_w2[l], 'hy_b2': hy_b2[l],
            'hy_w3': hy_w3[l], 'hy_freq': hy_freq[l], 'hy_skip': hy_skip[l],
            'rg_conv_w': rg_conv_w[l], 'rg_conv_b': rg_conv_b[l], 'rg_wa': rg_wa[l], 'rg_ba': rg_ba[l],
            'rg_wx': rg_wx[l], 'rg_bx': rg_bx[l], 'rg_lambda': rg_lambda[l],
            'gla_w_lr': gla_w_lr[l], 'gla_b_lr': gla_b_lr[l], 'gla_norm_g': gla_norm_g[l],
            'w_hy_o': w_hy_o[l], 'w_rg_o': w_rg_o[l], 'w_gla_o': w_gla_o[l],
            'b_merge': b_merge[l], 'w_out': w_out[l],
        }
        h_lat = modulate(rmsnorm(x_lat, g_norm_mix[l]), sh1, sc1)
        h_ctx = modulate(rmsnorm(x_ctx, g_norm_mix[l]), csh1, csc1)
        y_ctx, y_lat = token_mixer(h_ctx, h_lat, lp, need_ctx)
        x_lat = x_lat + gt1 * y_lat
        h_lat = modulate(rmsnorm(x_lat, g_norm_ffn[l]), sh2, sc2)
        x_lat = x_lat + gt2 * peer_ffn(h_lat, peer_wq[l], peer_keys[l], peer_u[l], peer_v[l])
        if need_ctx:
            x_ctx = x_ctx + cgt1 * y_ctx
            h_ctx = modulate(rmsnorm(x_ctx, g_norm_ffn[l]), csh2, csc2)
            x_ctx = x_ctx + cgt2 * peer_ffn(h_ctx, peer_wq[l], peer_keys[l], peer_u[l], peer_v[l])
    return rmsnorm(x_lat, g_final)
```

```python
import functools
import math

import numpy as np
import jax
import jax.numpy as jnp
from jax import lax
from jax.experimental import pallas as pl
from jax.experimental.pallas import tpu as pltpu

F32 = jnp.float32
BF16 = jnp.bfloat16
HIGHEST = lax.Precision.HIGHEST

D_MODEL = 1024
DEPTH = 4
GRID_W = 64
NORM_EPS = 1e-6

HY_W = 512
HY_BANDS = 16
HY_FFN = 64
HY_FAST_RATE = math.log(1e-2) / 0.3
HY_SLOW_RATE = math.log(1e-2) / 1.5

RG_W = 512
RG_BLOCKS = 8
RG_C = 8.0

GLA_HEADS = 4
GLA_DK = 64
GLA_DV = 128
GLA_QK = GLA_HEADS * GLA_DK
GLA_V = GLA_HEADS * GLA_DV
GLA_RANK = 16
GLA_TAU = 16.0
GLA_CHUNK = 64
GLA_IN = 2 * GLA_QK + GLA_V + 2 * GLA_RANK
GLA_IN_PAD = 2 * GLA_QK + GLA_V + 128

PEER_HEADS = 8
PEER_NKEYS = 128
PEER_TOPK = 16
PEER_HALF = 128
PEER_SEL = PEER_HEADS * PEER_TOPK

V7X_VMEM_BYTES = 64 * 1024 * 1024
SUBLANES = 8
LANES = 128


def _cparams(n_grid, vmem_bytes=None):
    kw = dict(dimension_semantics=("arbitrary",) * n_grid)
    if vmem_bytes is not None:
        assert vmem_bytes < V7X_VMEM_BYTES
        kw["vmem_limit_bytes"] = int(vmem_bytes)
    return pltpu.CompilerParams(**kw)


def _gelu_tanh(x):
    return 0.5 * x * (1.0 + jnp.tanh(math.sqrt(2.0 / math.pi) * (x + 0.044715 * (x * x * x))))


def _sigmoid(x):
    return 1.0 / (1.0 + jnp.exp(-x))


def _log_sigmoid(x):
    return jnp.minimum(x, 0.0) - jnp.log(1.0 + jnp.exp(-jnp.abs(x)))


def _mm_kernel(x_ref, w_ref, *rest, silu_in, has_bias):
    o_ref = rest[-1]
    x = x_ref[...]
    if silu_in:
        x = x * _sigmoid(x)
    acc = jnp.dot(x.astype(BF16), w_ref[...], preferred_element_type=F32)
    if has_bias:
        acc = acc + rest[0][...]
    o_ref[...] = acc.astype(o_ref.dtype)


def matmul(x, w, bias=None, silu_in=False, tm=512):
    m, k = x.shape
    n = w.shape[1]
    tm = min(tm, m)
    assert m % tm == 0
    in_specs = [pl.BlockSpec((tm, k), lambda i: (i, 0)), pl.BlockSpec((k, n), lambda i: (0, 0))]
    args = [x, w]
    if bias is not None:
        in_specs.append(pl.BlockSpec((1, n), lambda i: (0, 0)))
        args.append(bias.reshape(1, n))
    est = 2 * (tm * k * x.dtype.itemsize + k * n * 2 + tm * n * 4) + (4 << 20)
    return pl.pallas_call(
        functools.partial(_mm_kernel, silu_in=silu_in, has_bias=bias is not None),
        grid=(m // tm,),
        in_specs=in_specs,
        out_specs=pl.BlockSpec((tm, n), lambda i: (i, 0)),
        out_shape=jax.ShapeDtypeStruct((m, n), F32),
        compiler_params=_cparams(1, est),
        name="matmul",
    )(*args)


def _normmod_kernel(x_ref, g_ref, sh_ref, sc_ref, o_ref, *, mod):
    x = x_ref[...]
    y = x * lax.rsqrt(jnp.mean(x * x, axis=-1, keepdims=True) + NORM_EPS) * g_ref[...]
    if mod:
        y = y * (1.0 + sc_ref[...]) + sh_ref[...]
    o_ref[...] = y.astype(o_ref.dtype)


def normmod(x, g, shift, scale, out_dtype, mod=True, tm=512):
    b, l, d = x.shape
    tm = min(tm, l)
    assert l % tm == 0
    vec = pl.BlockSpec((None, 1, d), lambda bi, i: (bi, 0, 0))
    return pl.pallas_call(
        functools.partial(_normmod_kernel, mod=mod),
        grid=(b, l // tm),
        in_specs=[pl.BlockSpec((None, tm, d), lambda bi, i: (bi, i, 0)),
                  pl.BlockSpec((1, d), lambda bi, i: (0, 0)), vec, vec],
        out_specs=pl.BlockSpec((None, tm, d), lambda bi, i: (bi, i, 0)),
        out_shape=jax.ShapeDtypeStruct((b, l, d), out_dtype),
        compiler_params=_cparams(2),
        name="normmod",
    )(x, g.reshape(1, d), shift, scale)


def _resid_kernel(x_ref, g_ref, y_ref, o_ref):
    o_ref[...] = x_ref[...] + g_ref[...] * y_ref[...]


def gated_residual(x, gate, y, tm=512):
    b, l, d = x.shape
    tm = min(tm, l)
    blk = pl.BlockSpec((None, tm, d), lambda bi, i: (bi, i, 0))
    return pl.pallas_call(
        _resid_kernel,
        grid=(b, l // tm),
        in_specs=[blk, pl.BlockSpec((None, 1, d), lambda bi, i: (bi, 0, 0)), blk],
        out_specs=blk,
        out_shape=jax.ShapeDtypeStruct((b, l, d), F32),
        compiler_params=_cparams(2),
        name="gated_residual",
    )(x, gate, y)


def _assemble_rows(rows, n):
    t = rows[0].shape[1]
    rid = lax.broadcasted_iota(jnp.int32, (n, t), 0)
    out = jnp.zeros((n, t), rows[0].dtype)
    for r in range(n):
        out = jnp.where(rid == r, rows[r], out)
    return out


def _extract_topk(s, rowid, k):
    vals, ids = [], []
    for _ in range(k):
        m = jnp.max(s, axis=0, keepdims=True)
        first = jnp.min(jnp.where(s == m, rowid, 1e9), axis=0, keepdims=True)
        s = jnp.where(rowid == first, -jnp.inf, s)
        vals.append(m)
        ids.append(first)
    return vals, ids


def _peer_select_kernel(h_ref, wq_ref, keys_ref, exp_ref, wgt_ref):
    tt = h_ref.shape[0]
    k = PEER_TOPK
    q = jnp.dot(h_ref[...], wq_ref[...], precision=HIGHEST, preferred_element_type=F32)
    key_id = lax.broadcasted_iota(jnp.int32, (PEER_NKEYS, tt), 0).astype(F32)
    row8 = lax.broadcasted_iota(jnp.int32, (SUBLANES, tt), 0)
    row16 = lax.broadcasted_iota(jnp.int32, (2 * SUBLANES, tt), 0)
    exp_blocks, wgt_blocks = [], []
    for h in range(PEER_HEADS):
        tops = []
        for p in range(2):
            col = (h * 2 + p) * PEER_HALF
            qs = q[:, col:col + PEER_HALF]
            s = lax.dot_general(keys_ref[h, p], qs, (((1,), (1,)), ((), ())),
                                precision=HIGHEST, preferred_element_type=F32)
            vals, ids = _extract_topk(s, key_id, k)
            tops.append((_assemble_rows(vals, k), _assemble_rows(ids, k)))
        (a, ia), (b, ib) = tops
        blocks, flat = [a[0:1] + b], [row16.astype(F32)]
        for i in range(1, 8):
            nj = k // (i + 1)
            blocks.append(jnp.where(row8 < nj, a[i:i + 1] + b[0:8], -jnp.inf))
            flat.append((row8 + i * k).astype(F32))
        blocks.append(a[8:16] + b[0:1])
        flat.append(((row8 + 8) * k).astype(F32))
        cand = jnp.concatenate(blocks, axis=0)
        cand_id = jnp.concatenate(flat, axis=0)
        vals, ids = _extract_topk(cand, cand_id, k)
        best = _assemble_rows(vals, k)
        fl = _assemble_rows(ids, k)
        fi = jnp.floor(fl * (1.0 / k))
        fj = fl - fi * k
        ei = jnp.zeros_like(fl)
        ej = jnp.zeros_like(fl)
        for r in range(k):
            ei = jnp.where(fi == r, ia[r:r + 1], ei)
            ej = jnp.where(fj == r, ib[r:r + 1], ej)
        e = jnp.exp(best - jnp.max(best, axis=0, keepdims=True))
        wgt_blocks.append(e / jnp.sum(e, axis=0, keepdims=True))
        exp_blocks.append(ei * PEER_NKEYS + ej)
    exp_ref[...] = jnp.concatenate(exp_blocks, axis=0).T.astype(jnp.int32)
    wgt_ref[...] = jnp.concatenate(wgt_blocks, axis=0).T


def peer_select(h, wq, keys, tt=128):
    t, d = h.shape
    assert t % tt == 0
    nq = wq.shape[1]
    return pl.pallas_call(
        _peer_select_kernel,
        grid=(t // tt,),
        in_specs=[pl.BlockSpec((tt, d), lambda i: (i, 0)),
                  pl.BlockSpec((d, nq), lambda i: (0, 0)),
                  pl.BlockSpec(keys.shape, lambda i: (0, 0, 0, 0))],
        out_specs=[pl.BlockSpec((tt, PEER_SEL), lambda i: (i, 0)),
                   pl.BlockSpec((tt, PEER_SEL), lambda i: (i, 0))],
        out_shape=[jax.ShapeDtypeStruct((t, PEER_SEL), jnp.int32),
                   jax.ShapeDtypeStruct((t, PEER_SEL), F32)],
        compiler_params=_cparams(1, 2 * (d * nq + keys.size + 4 * tt * d) * 4 + (8 << 20)),
        name="peer_select",
    )(h, wq, keys)


ROW_WORDS = D_MODEL // 2 // LANES


def _unpack_pair(w):
    lo = lax.bitcast_convert_type(lax.shift_left(w, jnp.int32(16)), F32)
    hi = lax.bitcast_convert_type(jnp.bitwise_and(w, jnp.int32(-65536)), F32)
    return lo, hi


def _peer_act_kernel(idx_ref, xlo_ref, xhi_ref, wgt_ref, tab_ref, coef_ref, p_ref):
    tt = xlo_ref.shape[0]
    nsel = PEER_SEL
    grp = (lax.broadcasted_iota(jnp.int32, (nsel, nsel * ROW_WORDS), 1) // ROW_WORDS
           == lax.broadcasted_iota(jnp.int32, (nsel, nsel * ROW_WORDS), 0)).astype(F32)
    ones = jnp.ones((SUBLANES, LANES), F32)

    def body(t, carry):
        xl = xlo_ref[t]
        xh = xhi_ref[t]
        for e in range(nsel):
            r = pl.multiple_of(idx_ref[t, e] * ROW_WORDS, ROW_WORDS)
            lo, hi = _unpack_pair(tab_ref[pl.ds(r, ROW_WORDS), :])
            p_ref[e * ROW_WORDS:(e + 1) * ROW_WORDS, :] = lo * xl + hi * xh
        per_lane = jnp.dot(grp, p_ref[...], precision=HIGHEST, preferred_element_type=F32)
        act = lax.dot_general(ones, per_lane, (((1,), (1,)), ((), ())),
                              precision=HIGHEST, preferred_element_type=F32)[0:1]
        coef_ref[t] = wgt_ref[t] * _gelu_tanh(act)
        return carry

    lax.fori_loop(0, tt, body, 0)


def _peer_out_kernel(idx_ref, coef_ref, tab_ref, out_ref):
    tt = out_ref.shape[0]
    nacc = 4

    def body(t, carry):
        acc_lo = [jnp.zeros((ROW_WORDS, LANES), F32) for _ in range(nacc)]
        acc_hi = [jnp.zeros((ROW_WORDS, LANES), F32) for _ in range(nacc)]
        for e in range(PEER_SEL):
            r = pl.multiple_of(idx_ref[t, e] * ROW_WORDS, ROW_WORDS)
            lo, hi = _unpack_pair(tab_ref[pl.ds(r, ROW_WORDS), :])
            c = coef_ref[t, e]
            acc_lo[e % nacc] = acc_lo[e % nacc] + c * lo
            acc_hi[e % nacc] = acc_hi[e % nacc] + c * hi
        out_ref[t, 0] = (acc_lo[0] + acc_lo[1]) + (acc_lo[2] + acc_lo[3])
        out_ref[t, 1] = (acc_hi[0] + acc_hi[1]) + (acc_hi[2] + acc_hi[3])
        return carry

    lax.fori_loop(0, tt, body, 0)


def _pack_table(tab):
    e, d = tab.shape
    pairs = tab.astype(BF16).reshape(e, d // 2, 2)
    return lax.bitcast_convert_type(pairs, jnp.int32).reshape(e * ROW_WORDS, LANES)


def _table_spec(shape):
    return pl.BlockSpec(shape, lambda i: (0, 0), pipeline_mode=pl.Buffered(1))


def peer_experts(h, expert, weight, u_pack, v_pack, tt=32):
    t, d = h.shape
    assert t % tt == 0
    hp = h.reshape(t, ROW_WORDS, LANES, 2)
    xlo, xhi = hp[..., 0], hp[..., 1]
    tab_bytes = u_pack.size * 4
    vmem = tab_bytes + (12 << 20)
    smem_idx = pl.BlockSpec((tt, PEER_SEL), lambda i: (i, 0), memory_space=pltpu.SMEM)
    xspec = pl.BlockSpec((tt, ROW_WORDS, LANES), lambda i: (i, 0, 0))
    rowspec = pl.BlockSpec((tt, 1, PEER_SEL), lambda i: (i, 0, 0))
    coef = pl.pallas_call(
        _peer_act_kernel,
        grid=(t // tt,),
        in_specs=[smem_idx, xspec, xspec, rowspec, _table_spec(u_pack.shape)],
        out_specs=rowspec,
        out_shape=jax.ShapeDtypeStruct((t, 1, PEER_SEL), F32),
        scratch_shapes=[pltpu.VMEM((PEER_SEL * ROW_WORDS, LANES), F32)],
        compiler_params=_cparams(1, vmem),
        name="peer_act",
    )(expert, xlo, xhi, weight.reshape(t, 1, PEER_SEL), u_pack)
    out = pl.pallas_call(
        _peer_out_kernel,
        grid=(t // tt,),
        in_specs=[smem_idx,
                  pl.BlockSpec((tt, PEER_SEL), lambda i: (i, 0), memory_space=pltpu.SMEM),
                  _table_spec(v_pack.shape)],
        out_specs=pl.BlockSpec((tt, 2, ROW_WORDS, LANES), lambda i: (i, 0, 0, 0)),
        out_shape=jax.ShapeDtypeStruct((t, 2, ROW_WORDS, LANES), F32),
        compiler_params=_cparams(1, vmem),
        name="peer_out",
    )(expert, coef.reshape(t, PEER_SEL), v_pack)
    return jnp.moveaxis(out, 1, 3).reshape(t, d)


def peer_ffn(h, wq, keys, u_pack, v_pack):
    b, n, d = h.shape
    hf = h.reshape(b * n, d)
    expert, weight = peer_select(hf, wq, keys)
    return peer_experts(hf, expert, weight, u_pack, v_pack).reshape(b, n, d)


RG_CONV = 4


def _shift_rows(cur, halo, k, reverse):
    tb = cur.shape[0]
    row8 = lax.broadcasted_iota(jnp.int32, (SUBLANES, cur.shape[1]), 0)
    if not reverse:
        rolled = pltpu.roll(cur, k, axis=0)
        first = jnp.where(row8 < k, pltpu.roll(halo, k, axis=0), rolled[0:SUBLANES])
        return jnp.concatenate([first, rolled[SUBLANES:]], axis=0)
    rolled = pltpu.roll(cur, tb - k, axis=0)
    last = jnp.where(row8 >= SUBLANES - k, pltpu.roll(halo, SUBLANES - k, axis=0),
                     rolled[tb - SUBLANES:])
    return jnp.concatenate([rolled[:tb - SUBLANES], last], axis=0)


def _rglru_kernel(u_ref, h0_ref, cw_ref, cb_ref, wa_ref, ba_ref, wx_ref, bx_ref, lam_ref, *rest,
                  reverse, has_acc):
    if has_acc:
        acc_ref, out_ref, hlast_ref, a_s, b_s, hp_s, halo_s = rest
    else:
        out_ref, hlast_ref, a_s, b_s, hp_s, halo_s = rest
    tb, c = u_ref.shape

    @pl.when(pl.program_id(1) == 0)
    def _():
        hp_s[...] = h0_ref[...]
        halo_s[...] = jnp.zeros_like(halo_s)

    cur = u_ref[...]
    halo = halo_s[...]
    xc = cb_ref[...] + cw_ref[RG_CONV - 1:RG_CONV, :] * cur
    for k in range(1, RG_CONV):
        xc = xc + cw_ref[RG_CONV - 1 - k:RG_CONV - k, :] * _shift_rows(cur, halo, k, reverse)
    halo_s[...] = cur[0:SUBLANES] if reverse else cur[tb - SUBLANES:]

    xb = xc.astype(BF16)
    gate_r = _sigmoid(jnp.dot(xb, wa_ref[...], preferred_element_type=F32) + ba_ref[...])
    gate_i = _sigmoid(jnp.dot(xb, wx_ref[...], preferred_element_type=F32) + bx_ref[...])
    lam = lam_ref[...]
    softplus_neg = jnp.maximum(-lam, 0.0) + jnp.log(1.0 + jnp.exp(-jnp.abs(lam)))
    a = jnp.exp(-RG_C * gate_r * softplus_neg)
    a_s[...] = a
    b_s[...] = jnp.sqrt(1.0 - a * a) * (gate_i * xc)

    row8 = lax.broadcasted_iota(jnp.int32, (SUBLANES, c), 0)
    nt = tb // SUBLANES

    def step(j, hp):
        jj = nt - 1 - j if reverse else j
        r0 = pl.multiple_of(jj * SUBLANES, SUBLANES)
        av = a_s[pl.ds(r0, SUBLANES), :]
        bv = b_s[pl.ds(r0, SUBLANES), :]
        for k in (1, 2, 4):
            if reverse:
                ok = row8 < SUBLANES - k
                sh = SUBLANES - k
            else:
                ok = row8 >= k
                sh = k
            a_prev = jnp.where(ok, pltpu.roll(av, sh, axis=0), 1.0)
            b_prev = jnp.where(ok, pltpu.roll(bv, sh, axis=0), 0.0)
            bv = av * b_prev + bv
            av = av * a_prev
        h = av * hp + bv
        if has_acc:
            out_ref[pl.ds(r0, SUBLANES), :] = h + acc_ref[pl.ds(r0, SUBLANES), :]
        else:
            out_ref[pl.ds(r0, SUBLANES), :] = h
        return h[0:1] if reverse else h[SUBLANES - 1:SUBLANES]

    hp = lax.fori_loop(0, nt, step, hp_s[...])
    hp_s[...] = hp
    hlast_ref[...] = hp


def rglru_scan(u, h0, p, reverse, acc=None, tb=512):
    b, n, c = u.shape
    tb = min(tb, n)
    assert n % tb == 0
    nblk = n // tb
    if reverse:
        seq = pl.BlockSpec((None, tb, c), lambda bi, i: (bi, nblk - 1 - i, 0))
    else:
        seq = pl.BlockSpec((None, tb, c), lambda bi, i: (bi, i, 0))
    state = pl.BlockSpec((None, 1, c), lambda bi, i: (bi, 0, 0))

    def par(a):
        return pl.BlockSpec(a.shape, lambda bi, i: (0,) * a.ndim)

    params = [p["conv_w"], p["conv_b"], p["wa"], p["ba"], p["wx"], p["bx"], p["lam"]]
    in_specs = [seq, state] + [par(a) for a in params]
    args = [u, h0] + params
    if acc is not None:
        in_specs.append(seq)
        args.append(acc)
    return pl.pallas_call(
        functools.partial(_rglru_kernel, reverse=reverse, has_acc=acc is not None),
        grid=(b, nblk),
        in_specs=in_specs,
        out_specs=[seq, state],
        out_shape=[jax.ShapeDtypeStruct((b, n, c), F32), jax.ShapeDtypeStruct((b, 1, c), F32)],
        scratch_shapes=[pltpu.VMEM((tb, c), F32), pltpu.VMEM((tb, c), F32),
                        pltpu.VMEM((1, c), F32), pltpu.VMEM((SUBLANES, c), F32)],
        compiler_params=_cparams(2),
        name="rglru_bwd" if reverse else "rglru_fwd",
    )(*args)


def _block_diag(w):
    g, bs, _ = w.shape
    eye = jnp.eye(g, dtype=w.dtype)
    return (eye[:, None, :, None] * w[:, :, None, :]).reshape(g * bs, g * bs)


def rglru_params(conv_w, conv_b, wa, ba, wx, bx, lam, d):
    c = conv_b.shape[-1]
    return {"conv_w": conv_w[d], "conv_b": conv_b[d].reshape(1, c),
            "wa": _block_diag(wa[d]).astype(BF16), "ba": ba[d].reshape(1, c),
            "wx": _block_diag(wx[d]).astype(BF16), "bx": bx[d].reshape(1, c),
            "lam": lam[d].reshape(1, c)}


def rglru_mix(u_ctx, u_lat, params):
    b, _, c = u_ctx.shape
    zero = jnp.zeros((b, 1, c), F32)
    h_ctx = h_lat = None
    for d in range(2):
        h_ctx, last = rglru_scan(u_ctx, zero, params[d], reverse=bool(d), acc=h_ctx)
        h_lat, _ = rglru_scan(u_lat, last, params[d], reverse=bool(d), acc=h_lat)
    return h_ctx, h_lat


def _gla_kernel(x_ref, s0_ref, wlr_ref, blr_ref, *rest, reverse, has_acc):
    if has_acc:
        acc_ref, o_ref, s_out_ref, st_s = rest
    else:
        o_ref, s_out_ref, st_s = rest
    tb = x_ref.shape[0]
    ch = GLA_CHUNK

    @pl.when(pl.program_id(1) == 0)
    def _():
        st_s[...] = s0_ref[...]

    r_i = lax.broadcasted_iota(jnp.int32, (ch, ch), 0)
    c_i = lax.broadcasted_iota(jnp.int32, (ch, ch), 1)
    if reverse:
        cum_mat = (c_i >= r_i).astype(F32)
        keep = c_i > r_i
    else:
        cum_mat = (c_i <= r_i).astype(F32)
        keep = c_i <= r_i
    lane = lax.broadcasted_iota(jnp.int32, (1, LANES), 1)
    head_lanes = (lane < GLA_DK, lane >= GLA_DK)
    nt_dims = (((1,), (1,)), ((), ()))
    chunks = range(tb // ch)
    for cidx in (reversed(chunks) if reverse else chunks):
        r0 = cidx * ch
        q = x_ref[r0:r0 + ch, 0:GLA_QK] * (GLA_DK ** -0.5)
        k = x_ref[r0:r0 + ch, GLA_QK:2 * GLA_QK]
        v = x_ref[r0:r0 + ch, 2 * GLA_QK:2 * GLA_QK + GLA_V]
        lr = x_ref[r0:r0 + ch, 2 * GLA_QK + GLA_V:GLA_IN_PAD]
        logits = jnp.dot(lr, wlr_ref[...], precision=HIGHEST, preferred_element_type=F32) + blr_ref[...]
        log_a = _log_sigmoid(logits) * (1.0 / GLA_TAU)
        cum = jnp.dot(cum_mat, log_a, precision=HIGHEST, preferred_element_type=F32)
        tot = cum[0:1] if reverse else cum[ch - 1:ch]
        qg = q * jnp.exp(cum)
        kg = k * jnp.exp(-cum)
        kd = k * jnp.exp(tot - cum)
        decay = jnp.exp(tot)
        outs = []
        for h in range(GLA_HEADS):
            sl = slice((h // 2) * LANES, (h // 2 + 1) * LANES)
            mine = head_lanes[h % 2]
            qm = jnp.where(mine, qg[:, sl], 0.0).astype(BF16)
            scores = lax.dot_general(qm, kg[:, sl].astype(BF16), nt_dims, preferred_element_type=F32)
            scores = jnp.where(keep, scores, 0.0)
            vh = v[:, h * GLA_DV:(h + 1) * GLA_DV]
            st = st_s[h]
            o = jnp.dot(scores.astype(BF16), vh.astype(BF16), preferred_element_type=F32)
            o = o + lax.dot_general(qm, st.astype(BF16), nt_dims, preferred_element_type=F32)
            kdm = jnp.where(mine, kd[:, sl], 0.0).astype(BF16)
            st_s[h] = st * decay[:, sl] + jnp.dot(vh.T.astype(BF16), kdm, preferred_element_type=F32)
            outs.append(o)
        o_all = jnp.concatenate(outs, axis=1)
        if has_acc:
            o_all = o_all + acc_ref[r0:r0 + ch, :]
        o_ref[r0:r0 + ch, :] = o_all
    s_out_ref[...] = st_s[...]


def gla_scan(x, s0, wlr, blr, reverse, acc=None, tb=512):
    b, n, cin = x.shape
    tb = min(tb, n)
    assert n % tb == 0 and tb % GLA_CHUNK == 0
    nblk = n // tb

    def seq(width):
        if reverse:
            return pl.BlockSpec((None, tb, width), lambda bi, i: (bi, nblk - 1 - i, 0))
        return pl.BlockSpec((None, tb, width), lambda bi, i: (bi, i, 0))

    state = pl.BlockSpec((None, GLA_HEADS, GLA_DV, LANES), lambda bi, i: (bi, 0, 0, 0))
    in_specs = [seq(cin), state,
                pl.BlockSpec(wlr.shape, lambda bi, i: (0, 0)), pl.BlockSpec(blr.shape, lambda bi, i: (0, 0))]
    args = [x, s0, wlr, blr]
    if acc is not None:
        in_specs.append(seq(GLA_V))
        args.append(acc)
    return pl.pallas_call(
        functools.partial(_gla_kernel, reverse=reverse, has_acc=acc is not None),
        grid=(b, nblk),
        in_specs=in_specs,
        out_specs=[seq(GLA_V), state],
        out_shape=[jax.ShapeDtypeStruct((b, n, GLA_V), F32),
                   jax.ShapeDtypeStruct((b, GLA_HEADS, GLA_DV, LANES), F32)],
        scratch_shapes=[pltpu.VMEM((GLA_HEADS, GLA_DV, LANES), F32)],
        compiler_params=_cparams(2),
        name="gla_bwd" if reverse else "gla_fwd",
    )(*args)


def gla_params(w_lr, b_lr, d):
    w = jnp.zeros((LANES, GLA_QK), F32).at[d * GLA_RANK:(d + 1) * GLA_RANK].set(w_lr[d])
    return w, b_lr[d].reshape(1, GLA_QK)


def gla_mix(x_ctx, x_lat_cols, w_lr, b_lr):
    b = x_ctx.shape[0]
    zero = jnp.zeros((b, GLA_HEADS, GLA_DV, LANES), F32)
    o_ctx = o_lat = None
    for d in range(2):
        w, bias = gla_params(w_lr, b_lr, d)
        o_ctx, s = gla_scan(x_ctx, zero, w, bias, reverse=bool(d), acc=o_ctx)
        o_lat, _ = gla_scan(x_lat_cols, s, w, bias, reverse=bool(d), acc=o_lat)
    return o_ctx, o_lat


def raster_to_column(t):
    b, n = t.shape[:2]
    return t.reshape(b, n // GRID_W, GRID_W, *t.shape[2:]).swapaxes(1, 2).reshape(t.shape)


def column_to_raster(t):
    b, n = t.shape[:2]
    return t.reshape(b, GRID_W, n // GRID_W, *t.shape[2:]).swapaxes(1, 2).reshape(t.shape)


def _merge_kernel(x_ref, gt_ref, mg_ref, yhy_ref, hrg_ref, grg_ref, ogla_ref, ggla_ref,
                  bm_ref, gn_ref, why_ref, wrg_ref, wgla_ref, wout_ref, o_ref):
    d = x_ref.shape[1]
    y_rg = hrg_ref[...] * _gelu_tanh(grg_ref[...])
    o = ogla_ref[...]
    heads = []
    for h in range(GLA_HEADS):
        oh = o[:, h * GLA_DV:(h + 1) * GLA_DV]
        heads.append(oh * lax.rsqrt(jnp.mean(oh * oh, axis=-1, keepdims=True) + NORM_EPS) * gn_ref[...])
    gg = ggla_ref[...]
    y_gla = jnp.concatenate(heads, axis=1) * (gg * _sigmoid(gg))
    gate = _sigmoid(mg_ref[...] + bm_ref[...])

    def proj(y, w_ref):
        return jnp.dot(y.astype(BF16), w_ref[...], preferred_element_type=F32)

    m = (gate[:, 0:d] * proj(yhy_ref[...], why_ref) + gate[:, d:2 * d] * proj(y_rg, wrg_ref)
         + gate[:, 2 * d:3 * d] * proj(y_gla, wgla_ref))
    o_ref[...] = x_ref[...] + gt_ref[...] * proj(m, wout_ref)


def merge_residual(x, gt, mg, y_hy, h_rg, g_rg, o_gla, g_gla, lw, tm=256):
    b, n, d = x.shape
    tm = min(tm, n)
    assert n % tm == 0

    def seq(a):
        return pl.BlockSpec((None, tm, a.shape[2]), lambda bi, i: (bi, i, 0))

    def par(a):
        return pl.BlockSpec(a.shape, lambda bi, i: (0,) * a.ndim)

    streams = [mg, y_hy, h_rg, g_rg, o_gla, g_gla]
    params = [lw["b_merge"], lw["gla_norm_g"], lw["w_hy_o"], lw["w_rg_o"], lw["w_gla_o"], lw["w_out"]]
    return pl.pallas_call(
        _merge_kernel,
        grid=(b, n // tm),
        in_specs=[seq(x), pl.BlockSpec((None, 1, d), lambda bi, i: (bi, 0, 0))]
                 + [seq(a) for a in streams] + [par(a) for a in params],
        out_specs=seq(x),
        out_shape=jax.ShapeDtypeStruct((b, n, d), F32),
        compiler_params=_cparams(2, 40 << 20),
        name="merge_residual",
    )(x, gt, *streams, *params)


def _shortconv_kernel(x_ref, w_ref, b_ref, o_ref, *, rows):
    n, c = x_ref.shape
    zero = jnp.zeros((SUBLANES, c), F32)
    for r0 in range(0, n, rows):
        cur = x_ref[r0:r0 + rows, :]
        before = x_ref[r0 - SUBLANES:r0, :] if r0 > 0 else zero
        after = x_ref[r0 + rows:r0 + rows + SUBLANES, :] if r0 + rows < n else zero
        o_ref[r0:r0 + rows, :] = (b_ref[...] + w_ref[0:1, :] * _shift_rows(cur, before, 1, False)
                                  + w_ref[1:2, :] * cur + w_ref[2:3, :] * _shift_rows(cur, after, 1, True))


def shortconv(x, w, bias):
    b, n, c = x.shape
    blk = pl.BlockSpec((None, n, LANES), lambda bi, j: (bi, 0, j))
    return pl.pallas_call(
        functools.partial(_shortconv_kernel, rows=min(n, 1024)),
        grid=(b, c // LANES),
        in_specs=[blk, pl.BlockSpec((3, LANES), lambda bi, j: (0, j)),
                  pl.BlockSpec((1, LANES), lambda bi, j: (0, j))],
        out_specs=blk,
        out_shape=jax.ShapeDtypeStruct((b, n, c), F32),
        compiler_params=_cparams(2, 40 << 20),
        name="shortconv",
    )(x, w, bias.reshape(1, c))


def _hy_filter_kernel(w1_ref, b1_ref, w2_ref, b2_ref, w3_ref, freq_ref, band_ref, delta_ref,
                      h_ref, asum_ref, *, n):
    tb = h_ref.shape[0]
    i = pl.program_id(0)
    idx = (i * tb + lax.broadcasted_iota(jnp.int32, (tb, LANES), 0)).astype(F32)
    lane = lax.broadcasted_iota(jnp.int32, (tb, LANES), 1)
    tn = idx / (n - 1)
    ang = (2.0 * math.pi / n) * idx * band_ref[...]
    feats = jnp.where(lane == 0, tn,
                      jnp.where(lane <= HY_BANDS, jnp.cos(ang),
                                jnp.where(lane <= 2 * HY_BANDS, -jnp.sin(ang), 0.0)))
    fr = freq_ref[...]

    def dense(x, w_ref):
        return jnp.dot(x, w_ref[...], precision=HIGHEST, preferred_element_type=F32)

    h = jnp.sin(fr * (dense(feats, w1_ref) + b1_ref[...]))
    h = jnp.sin(fr * (dense(h, w2_ref) + b2_ref[...]))
    h = dense(h, w3_ref) * jnp.exp(-tn[:, 0:1] * delta_ref[...])
    h_ref[...] = h

    @pl.when(i == 0)
    def _():
        asum_ref[...] = jnp.zeros_like(asum_ref)

    asum_ref[...] += jnp.sum(jnp.abs(h), axis=0, keepdims=True)


def hyena_filter(n, w1, b1, w2, b2, w3, freq, tb=256):
    nout = w3.shape[1]
    tb = min(tb, n)
    bands = np.zeros((1, LANES), np.float32)
    lin = np.linspace(1e-4, HY_BANDS - 1, HY_BANDS, dtype=np.float32)
    bands[0, 1:1 + HY_BANDS] = lin
    bands[0, 1 + HY_BANDS:1 + 2 * HY_BANDS] = lin
    deltas = np.abs(np.linspace(HY_FAST_RATE, HY_SLOW_RATE, HY_W, dtype=np.float32))
    deltas = np.tile(deltas, nout // HY_W).reshape(1, nout)
    w1p = jnp.zeros((LANES, HY_FFN), F32).at[:w1.shape[0]].set(w1)
    params = [w1p, b1.reshape(1, -1), w2, b2.reshape(1, -1), w3, freq.reshape(1, -1),
              jnp.asarray(bands), jnp.asarray(deltas)]
    return pl.pallas_call(
        functools.partial(_hy_filter_kernel, n=n),
        grid=(n // tb,),
        in_specs=[pl.BlockSpec(a.shape, lambda i: (0, 0)) for a in params],
        out_specs=[pl.BlockSpec((tb, nout), lambda i: (i, 0)), pl.BlockSpec((1, nout), lambda i: (0, 0))],
        out_shape=[jax.ShapeDtypeStruct((n, nout), F32), jax.ShapeDtypeStruct((1, nout), F32)],
        compiler_params=_cparams(1),
        name="hyena_filter",
    )(*params)


def _filter_taps(hraw, n):
    h = hraw.reshape(n, 2, 2, HY_W)
    h_fwd = jnp.moveaxis(h[:, :, 0], 1, 0)
    h_bwd = jnp.moveaxis(h[:, :, 1], 1, 0)
    zeros = jnp.zeros_like(h_fwd)
    taps_f = jnp.concatenate([h_fwd, zeros], axis=1)
    taps_b = jnp.concatenate([h_bwd[:, :1], zeros, h_bwd[:, :0:-1]], axis=1)
    return taps_f, taps_b


def _cis(num, den):
    ang = (2.0 * math.pi / den) * (num % den).astype(F32)
    return jnp.cos(ang), jnp.sin(ang)


def _dft_tables(n):
    big = 2 * n
    q = int(round(math.sqrt(big)))
    assert q * q == big and q % (2 * SUBLANES) == 0
    ar = jnp.arange(q, dtype=jnp.int32)
    num = ar[None, :, None] * (q * ar[None, None, :] + ar[:, None, None])
    c, s = _cis(num, big)
    w1 = jnp.concatenate([c, -s], axis=1)
    ct, st = jnp.swapaxes(c, 1, 2)[:, :q // 2], jnp.swapaxes(s, 1, 2)[:, :q // 2]
    v = jnp.concatenate([ct, -st], axis=2) * (1.0 / big)
    c2, s2 = _cis(ar[:, None] * ar[None, :], q)
    f2 = jnp.concatenate([jnp.concatenate([c2, s2], axis=1), jnp.concatenate([-s2, c2], axis=1)], axis=0)
    g2 = jnp.concatenate([jnp.concatenate([c2, -s2], axis=1), jnp.concatenate([s2, c2], axis=1)], axis=0)
    return {"q": q, "w1": w1, "v": v, "f2": f2, "g2": g2}


def _level_kernel(w_ref, x_ref, *rest, n_add, has_gate, precise):
    rest = list(rest)
    x2_ref = rest.pop(0) if n_add else None
    if has_gate:
        src_ref, gate_ref, skip_ref = rest[:3]
        rest = rest[3:]
    o_ref = rest[0]
    for s in range(w_ref.shape[0]):
        x = x_ref[s]
        if n_add:
            x = x + x2_ref[s]
        if precise:
            y = jnp.dot(w_ref[s], x, precision=HIGHEST, preferred_element_type=F32)
        else:
            y = jnp.dot(w_ref[s], x.astype(BF16), preferred_element_type=F32)
        if has_gate:
            y = gate_ref[s] * (y + src_ref[s] * skip_ref[...])
        o_ref[s] = y.astype(o_ref.dtype)


def dft_level(w, x, x_group=0, add=None, gate=None, out_dtype=F32, precise=False, sb=8):
    b, q, k, _ = x.shape
    m = w.shape[1]
    c = HY_W
    sb = min(sb, q)

    def seq(rows, group):
        return pl.BlockSpec((None, sb, rows, c), lambda i, bi: (bi, i, 0, group))

    in_specs = [pl.BlockSpec((sb, m, k), lambda i, bi: (i, 0, 0)), seq(k, x_group)]
    args = [w, x]
    if add is not None:
        in_specs.append(seq(k, 0))
        args.append(add)
    if gate is not None:
        src, src_group, gates, gate_group, skip = gate
        in_specs += [seq(m, src_group), seq(m, gate_group), pl.BlockSpec((1, c), lambda i, bi: (0, 0))]
        args += [src, gates, skip]
    return pl.pallas_call(
        functools.partial(_level_kernel, n_add=add is not None, has_gate=gate is not None,
                          precise=precise),
        grid=(q // sb, b),
        in_specs=in_specs,
        out_specs=seq(m, 0),
        out_shape=jax.ShapeDtypeStruct((b, q, m, c), out_dtype),
        compiler_params=_cparams(2, 40 << 20),
        name="dft_level",
    )(*args)


def _dft_mid_kernel(a_ref, h_ref, f_ref, g_ref, o_ref):
    q = a_ref.shape[1] // 2
    for j in range(a_ref.shape[0]):
        x = jnp.dot(f_ref[...], a_ref[j], preferred_element_type=F32)
        xr, xi = x[:q], x[q:]
        hr, hi = h_ref[j, :q, :], h_ref[j, q:, :]
        y = jnp.concatenate([xr * hr - xi * hi, xr * hi + xi * hr], axis=0)
        o_ref[j] = jnp.dot(g_ref[...], y.astype(BF16), preferred_element_type=F32).astype(o_ref.dtype)


def dft_mid(a, spec, f2, g2, kb=8):
    b, q, q2, c = a.shape
    blk = pl.BlockSpec((None, kb, q2, c), lambda i, bi: (bi, i, 0, 0))
    mat = pl.BlockSpec((q2, q2), lambda i, bi: (0, 0))
    return pl.pallas_call(
        _dft_mid_kernel,
        grid=(q // kb, b),
        in_specs=[blk, pl.BlockSpec((kb, q2, c), lambda i, bi: (i, 0, 0)), mat, mat],
        out_specs=blk,
        out_shape=jax.ShapeDtypeStruct(a.shape, BF16),
        compiler_params=_cparams(2, 40 << 20),
        name="dft_mid",
    )(a, spec, f2, g2)


def _dft_spec_kernel(a_ref, f_ref, asum_ref, o_ref):
    inv = 1.0 / (asum_ref[0:1, :] + asum_ref[1:2, :] + 1e-6)
    for j in range(a_ref.shape[0]):
        o_ref[j] = jnp.dot(f_ref[...], a_ref[j], precision=HIGHEST, preferred_element_type=F32) * inv


def dft_spec(a, f2, asum, kb=8):
    o, q, q2, c = a.shape
    blk = pl.BlockSpec((None, kb, q2, c), lambda i, oi: (oi, i, 0, 0))
    return pl.pallas_call(
        _dft_spec_kernel,
        grid=(q // kb, o),
        in_specs=[blk, pl.BlockSpec((q2, q2), lambda i, oi: (0, 0)),
                  pl.BlockSpec((None, 2, c), lambda i, oi: (oi, 0, 0))],
        out_specs=blk,
        out_shape=jax.ShapeDtypeStruct(a.shape, F32),
        compiler_params=_cparams(2, 40 << 20),
        name="dft_spec",
    )(a, f2, asum)


def _swap_levels(a):
    b, q, q2, c = a.shape
    return a.reshape(b, q, 2, q, c).transpose(0, 3, 2, 1, 4).reshape(b, q, q2, c)


def hyena_long(u, hraw, asum, skip, tabs):
    b, n, _ = u.shape
    q = tabs["q"]
    c = HY_W
    taps_f, taps_b = _filter_taps(hraw, n)

    def to_levels(t, rows):
        return t.reshape(t.shape[0], rows, q, t.shape[2]).swapaxes(1, 2)

    a = dft_level(tabs["w1"], to_levels(taps_f, q), add=to_levels(taps_b, q), precise=True)
    spec = dft_spec(_swap_levels(a), tabs["f2"], asum.reshape(2, 2, c))
    w1d = tabs["w1"][:, :, :q // 2].astype(BF16)
    vd = tabs["v"].astype(BF16)
    f2, g2 = tabs["f2"].astype(BF16), tabs["g2"].astype(BF16)
    u_t = to_levels(u, q // 2)
    src, group = u_t, 0
    for order in range(2):
        a = dft_level(w1d, src, x_group=group, out_dtype=BF16)
        cm = dft_mid(_swap_levels(a), spec[order], f2, g2)
        src = dft_level(vd, _swap_levels(cm),
                        gate=(src, group, u_t, order + 1, skip[order].reshape(1, c)))
        group = 0
    return src.swapaxes(1, 2).reshape(b, n, c)


def _dft_small_kernel(u_ref, gate_ref, skip_ref, h_ref, f_ref, g_ref, o_ref):
    u = u_ref[...]
    nb = h_ref.shape[0] // 2
    x = jnp.dot(f_ref[...], u, precision=HIGHEST, preferred_element_type=F32)
    xr, xi = x[:nb], x[nb:]
    hr, hi = h_ref[:nb, :], h_ref[nb:, :]
    y = jnp.concatenate([xr * hr - xi * hi, xr * hi + xi * hr], axis=0)
    conv = jnp.dot(g_ref[...], y, precision=HIGHEST, preferred_element_type=F32)
    o_ref[...] = gate_ref[...] * (conv + u * skip_ref[...])


def _dft_small_spec_kernel(tf_ref, tb_ref, f_ref, asum_ref, o_ref):
    inv = 1.0 / (asum_ref[0:1, :] + asum_ref[1:2, :] + 1e-6)
    o_ref[...] = jnp.dot(f_ref[...], tf_ref[...] + tb_ref[...], precision=HIGHEST,
                         preferred_element_type=F32) * inv


def hyena_short_seq(u, hraw, asum, skip):
    b, n, _ = u.shape
    big = 2 * n
    c = HY_W
    ar = jnp.arange(big, dtype=jnp.int32)
    cs, sn = _cis(ar[:, None] * ar[None, :], big)
    f_full = jnp.concatenate([cs, -sn], axis=0)
    g_half = jnp.concatenate([cs[:n], -sn[:n]], axis=1) * (1.0 / big)
    taps_f, taps_b = _filter_taps(hraw, n)
    tap = pl.BlockSpec((None, big, c), lambda o: (o, 0, 0))
    spec = pl.pallas_call(
        _dft_small_spec_kernel,
        grid=(2,),
        in_specs=[tap, tap, pl.BlockSpec((2 * big, big), lambda o: (0, 0)),
                  pl.BlockSpec((None, 2, c), lambda o: (o, 0, 0))],
        out_specs=pl.BlockSpec((None, 2 * big, c), lambda o: (o, 0, 0)),
        out_shape=jax.ShapeDtypeStruct((2, 2 * big, c), F32),
        compiler_params=_cparams(1),
        name="dft_small_spec",
    )(taps_f, taps_b, f_full, asum.reshape(2, 2, c))
    f_data = f_full[:, :n]
    src, group = u, 0
    for order in range(2):
        src = pl.pallas_call(
            _dft_small_kernel,
            grid=(b,),
            in_specs=[pl.BlockSpec((None, n, c), functools.partial(lambda bi, g: (bi, 0, g), g=group)),
                      pl.BlockSpec((None, n, c), functools.partial(lambda bi, g: (bi, 0, g), g=order + 1)),
                      pl.BlockSpec((1, c), lambda bi: (0, 0)),
                      pl.BlockSpec((2 * big, c), lambda bi: (0, 0)),
                      pl.BlockSpec((2 * big, n), lambda bi: (0, 0)),
                      pl.BlockSpec((n, 2 * big), lambda bi: (0, 0))],
            out_specs=pl.BlockSpec((None, n, c), lambda bi: (bi, 0, 0)),
            out_shape=jax.ShapeDtypeStruct((b, n, c), F32),
            compiler_params=_cparams(1),
            name="dft_small",
        )(src, u, skip[order].reshape(1, c), spec[order], f_data, g_half)
        group = 0
    return src


HY_IN = 3 * HY_W
IN_GROUPS = (HY_IN, RG_W, RG_W, GLA_IN, GLA_V, 3 * D_MODEL)


def _split_w_in(w):
    parts, start = [], 0
    for width in IN_GROUPS:
        parts.append(w[:, start:start + width])
        start += width
    parts[3] = jnp.pad(parts[3], ((0, 0), (0, GLA_IN_PAD - GLA_IN)))
    return [p.astype(BF16) for p in parts]


def _project(h, w):
    b, n, d = h.shape
    return matmul(h.reshape(b * n, d), w).reshape(b, n, w.shape[1])


def kernel(x, c, ctx, c_ctx, w_mod, b_mod, g_norm_mix, g_norm_ffn, w_in, hy_conv_w, hy_conv_b,
           hy_w1, hy_b1, hy_w2, hy_b2, hy_w3, hy_freq, hy_skip, rg_conv_w, rg_conv_b, rg_wa, rg_ba,
           rg_wx, rg_bx, rg_lambda, gla_w_lr, gla_b_lr, gla_norm_g, w_hy_o, w_rg_o, w_gla_o, b_merge,
           w_out, peer_wq, peer_keys, peer_u, peer_v, g_final):
    b, n, d = x.shape
    n_ctx = ctx.shape[1]
    depth = w_mod.shape[0]
    cond = jnp.concatenate([c, c_ctx[None, :]], axis=0)
    cond = jnp.pad(cond, ((0, -(b + 1) % SUBLANES), (0, 0)))
    tabs = _dft_tables(n)
    x_lat, x_ctx = x, ctx
    for l in range(depth):
        need_ctx = l < depth - 1
        mod = matmul(cond, w_mod[l].astype(BF16), bias=b_mod[l], silu_in=True)
        sh1, sc1, gt1, sh2, sc2, gt2 = [mod[:b, i * d:(i + 1) * d].reshape(b, 1, d) for i in range(6)]
        csh1, csc1, cgt1, csh2, csc2, cgt2 = [
            jnp.broadcast_to(mod[b:b + 1, i * d:(i + 1) * d].reshape(1, 1, d), (b, 1, d)) for i in range(6)]
        w_groups = _split_w_in(w_in[l])
        lw = {"b_merge": b_merge[l].reshape(1, 3 * d), "gla_norm_g": gla_norm_g[l].reshape(1, GLA_DV),
              "w_hy_o": w_hy_o[l].astype(BF16), "w_rg_o": w_rg_o[l].astype(BF16),
              "w_gla_o": w_gla_o[l].astype(BF16), "w_out": w_out[l].astype(BF16)}
        rg_par = [rglru_params(rg_conv_w[l], rg_conv_b[l], rg_wa[l], rg_ba[l], rg_wx[l], rg_bx[l],
                               rg_lambda[l], dd) for dd in range(2)]
        filt = (hy_w1[l], hy_b1[l], hy_w2[l], hy_b2[l], hy_w3[l], hy_freq[l])
        u_pack, v_pack = _pack_table(peer_u[l]), _pack_table(peer_v[l])

        h_lat = normmod(x_lat, g_norm_mix[l], sh1, sc1, BF16)
        h_ctx = normmod(x_ctx, g_norm_mix[l], csh1, csc1, BF16)
        hy_l, rgx_l, rgg_l, gla_l, glag_l, mg_l = [_project(h_lat, w) for w in w_groups]
        ctx_groups = range(6) if need_ctx else (1, 3)
        ctx_proj = {i: _project(h_ctx, w_groups[i]) for i in ctx_groups}

        hraw, asum = hyena_filter(n, *filt)
        y_hy_l = hyena_long(shortconv(hy_l, hy_conv_w[l], hy_conv_b[l]), hraw, asum, hy_skip[l], tabs)
        h_rg_c, h_rg_l = rglru_mix(ctx_proj[1], rgx_l, rg_par)
        o_gla_c, o_gla_l = gla_mix(ctx_proj[3], raster_to_column(gla_l), gla_w_lr[l], gla_b_lr[l])
        x_lat = merge_residual(x_lat, gt1, mg_l, y_hy_l, h_rg_l, rgg_l, column_to_raster(o_gla_l),
                               glag_l, lw)
        h2 = normmod(x_lat, g_norm_ffn[l], sh2, sc2, F32)
        x_lat = gated_residual(x_lat, gt2, peer_ffn(h2, peer_wq[l], peer_keys[l], u_pack, v_pack))
        if need_ctx:
            hraw_c, asum_c = hyena_filter(n_ctx, *filt)
            y_hy_c = hyena_short_seq(shortconv(ctx_proj[0], hy_conv_w[l], hy_conv_b[l]), hraw_c, asum_c,
                                     hy_skip[l])
            x_ctx = merge_residual(x_ctx, cgt1, ctx_proj[5], y_hy_c, h_rg_c, ctx_proj[2], o_gla_c,
                                   ctx_proj[4], lw)
            h2c = normmod(x_ctx, g_norm_ffn[l], csh2, csc2, F32)
            x_ctx = gated_residual(x_ctx, cgt2, peer_ffn(h2c, peer_wq[l], peer_keys[l], u_pack, v_pack))
    zero = jnp.zeros((b, 1, d), F32)
    return normmod(x_lat, g_final, zero, zero, F32, mod=False)
```

```python
import functools
import math

import numpy as np
import jax
import jax.numpy as jnp
from jax import lax
from jax.experimental import pallas as pl
from jax.experimental.pallas import tpu as pltpu

F32 = jnp.float32
BF16 = jnp.bfloat16
HIGHEST = lax.Precision.HIGHEST

D_MODEL = 1024
DEPTH = 4
GRID_W = 64
NORM_EPS = 1e-6

HY_W = 512
HY_BANDS = 16
HY_FFN = 64
HY_FAST_RATE = math.log(1e-2) / 0.3
HY_SLOW_RATE = math.log(1e-2) / 1.5

RG_W = 512
RG_BLOCKS = 8
RG_C = 8.0

GLA_HEADS = 4
GLA_DK = 64
GLA_DV = 128
GLA_QK = GLA_HEADS * GLA_DK
GLA_V = GLA_HEADS * GLA_DV
GLA_RANK = 16
GLA_TAU = 16.0
GLA_CHUNK = 64
GLA_IN = 2 * GLA_QK + GLA_V + 2 * GLA_RANK
GLA_IN_PAD = 2 * GLA_QK + GLA_V + 128

PEER_HEADS = 8
PEER_NKEYS = 128
PEER_TOPK = 16
PEER_HALF = 128
PEER_SEL = PEER_HEADS * PEER_TOPK

V7X_VMEM_BYTES = 64 * 1024 * 1024
SUBLANES = 8
LANES = 128


def _cparams(n_grid, vmem_bytes=None):
    kw = dict(dimension_semantics=("arbitrary",) * n_grid)
    if vmem_bytes is not None:
        assert vmem_bytes < V7X_VMEM_BYTES
        kw["vmem_limit_bytes"] = int(vmem_bytes)
    return pltpu.CompilerParams(**kw)


def _gelu_tanh(x):
    return 0.5 * x * (1.0 + jnp.tanh(math.sqrt(2.0 / math.pi) * (x + 0.044715 * (x * x * x))))


def _sigmoid(x):
    return 1.0 / (1.0 + jnp.exp(-x))


def _log_sigmoid(x):
    return jnp.minimum(x, 0.0) - jnp.log(1.0 + jnp.exp(-jnp.abs(x)))


def _mm_kernel(x_ref, w_ref, *rest, silu_in, has_bias):
    o_ref = rest[-1]
    x = x_ref[...]
    if silu_in:
        x = x * _sigmoid(x)
    acc = jnp.dot(x.astype(BF16), w_ref[...], preferred_element_type=F32)
    if has_bias:
        acc = acc + rest[0][...]
    o_ref[...] = acc.astype(o_ref.dtype)


def matmul(x, w, bias=None, silu_in=False, tm=512):
    m, k = x.shape
    n = w.shape[1]
    tm = min(tm, m)
    assert m % tm == 0
    in_specs = [pl.BlockSpec((tm, k), lambda i: (i, 0)), pl.BlockSpec((k, n), lambda i: (0, 0))]
    args = [x, w]
    if bias is not None:
        in_specs.append(pl.BlockSpec((1, n), lambda i: (0, 0)))
        args.append(bias.reshape(1, n))
    est = 2 * (tm * k * x.dtype.itemsize + k * n * 2 + tm * n * 4) + (4 << 20)
    return pl.pallas_call(
        functools.partial(_mm_kernel, silu_in=silu_in, has_bias=bias is not None),
        grid=(m // tm,),
        in_specs=in_specs,
        out_specs=pl.BlockSpec((tm, n), lambda i: (i, 0)),
        out_shape=jax.ShapeDtypeStruct((m, n), F32),
        compiler_params=_cparams(1, est),
        name="matmul",
    )(*args)


def _normmod_kernel(x_ref, g_ref, sh_ref, sc_ref, o_ref, *, mod):
    x = x_ref[...]
    y = x * lax.rsqrt(jnp.mean(x * x, axis=-1, keepdims=True) + NORM_EPS) * g_ref[...]
    if mod:
        y = y * (1.0 + sc_ref[...]) + sh_ref[...]
    o_ref[...] = y.astype(o_ref.dtype)


def normmod(x, g, shift, scale, out_dtype, mod=True, tm=512):
    b, l, d = x.shape
    tm = min(tm, l)
    assert l % tm == 0
    vec = pl.BlockSpec((None, 1, d), lambda bi, i: (bi, 0, 0))
    return pl.pallas_call(
        functools.partial(_normmod_kernel, mod=mod),
        grid=(b, l // tm),
        in_specs=[pl.BlockSpec((None, tm, d), lambda bi, i: (bi, i, 0)),
                  pl.BlockSpec((1, d), lambda bi, i: (0, 0)), vec, vec],
        out_specs=pl.BlockSpec((None, tm, d), lambda bi, i: (bi, i, 0)),
        out_shape=jax.ShapeDtypeStruct((b, l, d), out_dtype),
        compiler_params=_cparams(2),
        name="normmod",
    )(x, g.reshape(1, d), shift, scale)


def _resid_kernel(x_ref, g_ref, y_ref, o_ref):
    o_ref[...] = x_ref[...] + g_ref[...] * y_ref[...]


def gated_residual(x, gate, y, tm=512):
    b, l, d = x.shape
    tm = min(tm, l)
    blk = pl.BlockSpec((None, tm, d), lambda bi, i: (bi, i, 0))
    return pl.pallas_call(
        _resid_kernel,
        grid=(b, l // tm),
        in_specs=[blk, pl.BlockSpec((None, 1, d), lambda bi, i: (bi, 0, 0)), blk],
        out_specs=blk,
        out_shape=jax.ShapeDtypeStruct((b, l, d), F32),
        compiler_params=_cparams(2),
        name="gated_residual",
    )(x, gate, y)


def _assemble_rows(rows, n):
    t = rows[0].shape[1]
    rid = lax.broadcasted_iota(jnp.int32, (n, t), 0)
    out = jnp.zeros((n, t), rows[0].dtype)
    for r in range(n):
        out = jnp.where(rid == r, rows[r], out)
    return out


def _extract_topk(s, rowid, k):
    vals, ids = [], []
    for _ in range(k):
        m = jnp.max(s, axis=0, keepdims=True)
        first = jnp.min(jnp.where(s == m, rowid, 1e9), axis=0, keepdims=True)
        s = jnp.where(rowid == first, -jnp.inf, s)
        vals.append(m)
        ids.append(first)
    return vals, ids


def _split_bf16(x):
    hi = x.astype(BF16)
    return hi, (x - hi.astype(F32)).astype(BF16)


def _dot3(a_hi, a_lo, b_hi, b_lo, dims):
    def one(a, b):
        return lax.dot_general(a, b, dims, preferred_element_type=F32)
    return one(a_hi, b_hi) + (one(a_hi, b_lo) + one(a_lo, b_hi))


def _peer_select_kernel(h_ref, wqh_ref, wql_ref, kh_ref, kl_ref, exp_ref, wgt_ref):
    tt = h_ref.shape[0]
    k = PEER_TOPK
    h_hi, h_lo = _split_bf16(h_ref[...])
    q = _dot3(h_hi, h_lo, wqh_ref[...], wql_ref[...], (((1,), (0,)), ((), ())))
    key_id = lax.broadcasted_iota(jnp.int32, (PEER_NKEYS, tt), 0).astype(F32)
    row8 = lax.broadcasted_iota(jnp.int32, (SUBLANES, tt), 0)
    row16 = lax.broadcasted_iota(jnp.int32, (2 * SUBLANES, tt), 0)
    exp_blocks, wgt_blocks = [], []
    for h in range(PEER_HEADS):
        tops = []
        for p in range(2):
            col = (h * 2 + p) * PEER_HALF
            q_hi, q_lo = _split_bf16(q[:, col:col + PEER_HALF])
            s = _dot3(kh_ref[h, p], kl_ref[h, p], q_hi, q_lo, (((1,), (1,)), ((), ())))
            vals, ids = _extract_topk(s, key_id, k)
            tops.append((_assemble_rows(vals, k), _assemble_rows(ids, k)))
        (a, ia), (b, ib) = tops
        blocks, flat = [a[0:1] + b], [row16.astype(F32)]
        for i in range(1, 8):
            nj = k // (i + 1)
            blocks.append(jnp.where(row8 < nj, a[i:i + 1] + b[0:8], -jnp.inf))
            flat.append((row8 + i * k).astype(F32))
        blocks.append(a[8:16] + b[0:1])
        flat.append(((row8 + 8) * k).astype(F32))
        cand = jnp.concatenate(blocks, axis=0)
        cand_id = jnp.concatenate(flat, axis=0)
        vals, ids = _extract_topk(cand, cand_id, k)
        best = _assemble_rows(vals, k)
        fl = _assemble_rows(ids, k)
        fi = jnp.floor(fl * (1.0 / k))
        fj = fl - fi * k
        ei = jnp.zeros_like(fl)
        ej = jnp.zeros_like(fl)
        for r in range(k):
            ei = jnp.where(fi == r, ia[r:r + 1], ei)
            ej = jnp.where(fj == r, ib[r:r + 1], ej)
        e = jnp.exp(best - jnp.max(best, axis=0, keepdims=True))
        wgt_blocks.append(e / jnp.sum(e, axis=0, keepdims=True))
        exp_blocks.append((ei * PEER_NKEYS + ej) * ROW_WORDS)
    exp_ref[...] = jnp.concatenate(exp_blocks, axis=0).T.astype(jnp.int32)
    wgt_ref[...] = jnp.concatenate(wgt_blocks, axis=0).T


def peer_select(h, wq, keys, tt=128):
    t, d = h.shape
    assert t % tt == 0
    nq = wq.shape[1]
    wq_hi, wq_lo = _split_bf16(wq)
    k_hi, k_lo = _split_bf16(keys)
    wspec = pl.BlockSpec((d, nq), lambda i: (0, 0))
    kspec = pl.BlockSpec(keys.shape, lambda i: (0, 0, 0, 0))
    return pl.pallas_call(
        _peer_select_kernel,
        grid=(t // tt,),
        in_specs=[pl.BlockSpec((tt, d), lambda i: (i, 0)), wspec, wspec, kspec, kspec],
        out_specs=[pl.BlockSpec((tt, PEER_SEL), lambda i: (i, 0)),
                   pl.BlockSpec((tt, PEER_SEL), lambda i: (i, 0))],
        out_shape=[jax.ShapeDtypeStruct((t, PEER_SEL), jnp.int32),
                   jax.ShapeDtypeStruct((t, PEER_SEL), F32)],
        compiler_params=_cparams(1, 2 * (2 * d * nq + 2 * keys.size) * 2 + (16 << 20)),
        name="peer_select",
    )(h, wq_hi, wq_lo, k_hi, k_lo)


ROW_WORDS = D_MODEL // 2 // LANES


def _unpack_pair(w):
    lo = lax.bitcast_convert_type(lax.shift_left(w, jnp.int32(16)), F32)
    hi = lax.bitcast_convert_type(jnp.bitwise_and(w, jnp.int32(-65536)), F32)
    return lo, hi


PEER_CHUNK = 64
PEER_NCHUNK = PEER_SEL // PEER_CHUNK
PEER_GROUP = 8


def _peer_act_kernel(idx_ref, x_ref, wgt_ref, tab_ref, coef_ref, p_ref):
    tt = x_ref.shape[0]

    def group(g, carry):
        t0 = g * PEER_GROUP

        def token(tk, c):
            xl = x_ref[t0 + tk, 0:ROW_WORDS, :]
            xh = x_ref[t0 + tk, ROW_WORDS:2 * ROW_WORDS, :]

            def chunk(ci, c2):
                base = ((t0 + tk) * PEER_NCHUNK + ci) * PEER_CHUNK
                for j in range(PEER_CHUNK):
                    r = pl.multiple_of(idx_ref[base + j], ROW_WORDS)
                    lo, hi = _unpack_pair(tab_ref[pl.ds(r, ROW_WORDS), :])
                    p_ref[tk * PEER_NCHUNK + ci, j * ROW_WORDS:(j + 1) * ROW_WORDS, :] = lo * xl + hi * xh
                return c2

            return lax.fori_loop(0, PEER_NCHUNK, chunk, c)

        lax.fori_loop(0, PEER_GROUP, token, 0)
        for tk in range(PEER_GROUP):
            tok = p_ref.at[tk * PEER_NCHUNK:(tk + 1) * PEER_NCHUNK]
            parts = [tok[:, pl.ds(r, PEER_CHUNK, stride=ROW_WORDS), :].reshape(PEER_SEL, LANES)
                     for r in range(ROW_WORDS)]
            per_lane = (parts[0] + parts[1]) + (parts[2] + parts[3])
            act = jnp.sum(per_lane.T, axis=0, keepdims=True)
            coef_ref[t0 + tk] = wgt_ref[t0 + tk] * _gelu_tanh(act)
        return carry

    lax.fori_loop(0, tt // PEER_GROUP, group, 0)


def _peer_out_kernel(idx_ref, coef_ref, tab_ref, out_ref):
    tt = out_ref.shape[0]
    zero = jnp.zeros((ROW_WORDS, LANES), F32)

    def token(t, carry):
        def chunk(ci, acc):
            acc = list(acc)
            base = (t * PEER_NCHUNK + ci) * PEER_CHUNK
            for j in range(PEER_CHUNK):
                r = pl.multiple_of(idx_ref[base + j], ROW_WORDS)
                lo, hi = _unpack_pair(tab_ref[pl.ds(r, ROW_WORDS), :])
                c = coef_ref[base + j]
                acc[2 * (j % 2)] = acc[2 * (j % 2)] + c * lo
                acc[2 * (j % 2) + 1] = acc[2 * (j % 2) + 1] + c * hi
            return tuple(acc)

        acc = lax.fori_loop(0, PEER_NCHUNK, chunk, (zero, zero, zero, zero))
        out_ref[t, 0] = acc[0] + acc[2]
        out_ref[t, 1] = acc[1] + acc[3]
        return carry

    lax.fori_loop(0, tt, token, 0)


def _pack_table(tab):
    e, d = tab.shape
    pairs = jnp.moveaxis(tab.astype(BF16).reshape(e, 2, d // 2), 1, 2)
    return lax.bitcast_convert_type(pairs, jnp.int32).reshape(e * ROW_WORDS, LANES)


def _table_spec(shape):
    return pl.BlockSpec(shape, lambda i: (0, 0), pipeline_mode=pl.Buffered(1))


def peer_experts(h, expert, weight, u_pack, v_pack, tt=32):
    t, d = h.shape
    assert t % tt == 0
    hp = h.reshape(t, 2 * ROW_WORDS, LANES)
    tab_bytes = u_pack.size * 4
    vmem = tab_bytes + (12 << 20)
    smem_flat = pl.BlockSpec((tt * PEER_SEL,), lambda i: (i,), memory_space=pltpu.SMEM)
    xspec = pl.BlockSpec((tt, 2 * ROW_WORDS, LANES), lambda i: (i, 0, 0))
    rowspec = pl.BlockSpec((tt, 1, PEER_SEL), lambda i: (i, 0, 0))
    rows = expert.reshape(t * PEER_SEL)
    coef = pl.pallas_call(
        _peer_act_kernel,
        grid=(t // tt,),
        in_specs=[smem_flat, xspec, rowspec, _table_spec(u_pack.shape)],
        out_specs=rowspec,
        out_shape=jax.ShapeDtypeStruct((t, 1, PEER_SEL), F32),
        scratch_shapes=[pltpu.VMEM((PEER_GROUP * PEER_NCHUNK, PEER_CHUNK * ROW_WORDS, LANES), F32)],
        compiler_params=_cparams(1, vmem),
        name="peer_act",
    )(rows, hp, weight.reshape(t, 1, PEER_SEL), u_pack)
    out = pl.pallas_call(
        _peer_out_kernel,
        grid=(t // tt,),
        in_specs=[smem_flat, smem_flat, _table_spec(v_pack.shape)],
        out_specs=pl.BlockSpec((tt, 2, ROW_WORDS, LANES), lambda i: (i, 0, 0, 0)),
        out_shape=jax.ShapeDtypeStruct((t, 2, ROW_WORDS, LANES), F32),
        compiler_params=_cparams(1, vmem),
        name="peer_out",
    )(rows, coef.reshape(t * PEER_SEL), v_pack)
    return out.reshape(t, d)


def peer_ffn(h, wq, keys, u_pack, v_pack):
    b, n, d = h.shape
    hf = h.reshape(b * n, d)
    expert, weight = peer_select(hf, wq, keys)
    return peer_experts(hf, expert, weight, u_pack, v_pack).reshape(b, n, d)


RG_CONV = 4


def _shift_rows(cur, halo, k, reverse):
    tb = cur.shape[0]
    row8 = lax.broadcasted_iota(jnp.int32, (SUBLANES, cur.shape[1]), 0)
    if not reverse:
        rolled = pltpu.roll(cur, k, axis=0)
        first = jnp.where(row8 < k, pltpu.roll(halo, k, axis=0), rolled[0:SUBLANES])
        return jnp.concatenate([first, rolled[SUBLANES:]], axis=0)
    rolled = pltpu.roll(cur, tb - k, axis=0)
    last = jnp.where(row8 >= SUBLANES - k, pltpu.roll(halo, SUBLANES - k, axis=0),
                     rolled[tb - SUBLANES:])
    return jnp.concatenate([rolled[:tb - SUBLANES], last], axis=0)


def _rglru_kernel(u_ref, h0_ref, cw_ref, cb_ref, wa_ref, ba_ref, wx_ref, bx_ref, lam_ref, *rest,
                  reverse, has_acc):
    if has_acc:
        acc_ref, out_ref, hlast_ref, a_s, b_s, hp_s, halo_s = rest
    else:
        out_ref, hlast_ref, a_s, b_s, hp_s, halo_s = rest
    tb, c = u_ref.shape

    @pl.when(pl.program_id(1) == 0)
    def _():
        hp_s[...] = h0_ref[...]
        halo_s[...] = jnp.zeros_like(halo_s)

    cur = u_ref[...]
    halo = halo_s[...]
    xc = cb_ref[...] + cw_ref[RG_CONV - 1:RG_CONV, :] * cur
    for k in range(1, RG_CONV):
        xc = xc + cw_ref[RG_CONV - 1 - k:RG_CONV - k, :] * _shift_rows(cur, halo, k, reverse)
    halo_s[...] = cur[0:SUBLANES] if reverse else cur[tb - SUBLANES:]

    xb = xc.astype(BF16)
    gate_r = _sigmoid(jnp.dot(xb, wa_ref[...], preferred_element_type=F32) + ba_ref[...])
    gate_i = _sigmoid(jnp.dot(xb, wx_ref[...], preferred_element_type=F32) + bx_ref[...])
    lam = lam_ref[...]
    softplus_neg = jnp.maximum(-lam, 0.0) + jnp.log(1.0 + jnp.exp(-jnp.abs(lam)))
    a = jnp.exp(-RG_C * gate_r * softplus_neg)
    a_s[...] = a
    b_s[...] = jnp.sqrt(1.0 - a * a) * (gate_i * xc)

    row8 = lax.broadcasted_iota(jnp.int32, (SUBLANES, c), 0)
    nt = tb // SUBLANES

    def step(j, hp):
        jj = nt - 1 - j if reverse else j
        r0 = pl.multiple_of(jj * SUBLANES, SUBLANES)
        av = a_s[pl.ds(r0, SUBLANES), :]
        bv = b_s[pl.ds(r0, SUBLANES), :]
        for k in (1, 2, 4):
            if reverse:
                ok = row8 < SUBLANES - k
                sh = SUBLANES - k
            else:
                ok = row8 >= k
                sh = k
            a_prev = jnp.where(ok, pltpu.roll(av, sh, axis=0), 1.0)
            b_prev = jnp.where(ok, pltpu.roll(bv, sh, axis=0), 0.0)
            bv = av * b_prev + bv
            av = av * a_prev
        h = av * hp + bv
        if has_acc:
            out_ref[pl.ds(r0, SUBLANES), :] = h + acc_ref[pl.ds(r0, SUBLANES), :]
        else:
            out_ref[pl.ds(r0, SUBLANES), :] = h
        return h[0:1] if reverse else h[SUBLANES - 1:SUBLANES]

    hp = lax.fori_loop(0, nt, step, hp_s[...])
    hp_s[...] = hp
    hlast_ref[...] = hp


def rglru_scan(u, h0, p, reverse, acc=None, tb=512):
    b, n, c = u.shape
    tb = min(tb, n)
    assert n % tb == 0
    nblk = n // tb
    if reverse:
        seq = pl.BlockSpec((None, tb, c), lambda bi, i: (bi, nblk - 1 - i, 0))
    else:
        seq = pl.BlockSpec((None, tb, c), lambda bi, i: (bi, i, 0))
    state = pl.BlockSpec((None, 1, c), lambda bi, i: (bi, 0, 0))

    def par(a):
        return pl.BlockSpec(a.shape, lambda bi, i: (0,) * a.ndim)

    params = [p["conv_w"], p["conv_b"], p["wa"], p["ba"], p["wx"], p["bx"], p["lam"]]
    in_specs = [seq, state] + [par(a) for a in params]
    args = [u, h0] + params
    if acc is not None:
        in_specs.append(seq)
        args.append(acc)
    return pl.pallas_call(
        functools.partial(_rglru_kernel, reverse=reverse, has_acc=acc is not None),
        grid=(b, nblk),
        in_specs=in_specs,
        out_specs=[seq, state],
        out_shape=[jax.ShapeDtypeStruct((b, n, c), F32), jax.ShapeDtypeStruct((b, 1, c), F32)],
        scratch_shapes=[pltpu.VMEM((tb, c), F32), pltpu.VMEM((tb, c), F32),
                        pltpu.VMEM((1, c), F32), pltpu.VMEM((SUBLANES, c), F32)],
        compiler_params=_cparams(2),
        name="rglru_bwd" if reverse else "rglru_fwd",
    )(*args)


def _block_diag(w):
    g, bs, _ = w.shape
    eye = jnp.eye(g, dtype=w.dtype)
    return (eye[:, None, :, None] * w[:, :, None, :]).reshape(g * bs, g * bs)


def rglru_params(conv_w, conv_b, wa, ba, wx, bx, lam, d):
    c = conv_b.shape[-1]
    return {"conv_w": conv_w[d], "conv_b": conv_b[d].reshape(1, c),
            "wa": _block_diag(wa[d]).astype(BF16), "ba": ba[d].reshape(1, c),
            "wx": _block_diag(wx[d]).astype(BF16), "bx": bx[d].reshape(1, c),
            "lam": lam[d].reshape(1, c)}


def rglru_mix(u_ctx, u_lat, params):
    b, _, c = u_ctx.shape
    zero = jnp.zeros((b, 1, c), F32)
    h_ctx = h_lat = None
    for d in range(2):
        h_ctx, last = rglru_scan(u_ctx, zero, params[d], reverse=bool(d), acc=h_ctx)
        h_lat, _ = rglru_scan(u_lat, last, params[d], reverse=bool(d), acc=h_lat)
    return h_ctx, h_lat


def _gla_kernel(x_ref, s0_ref, wlr_ref, blr_ref, *rest, reverse, has_acc):
    if has_acc:
        acc_ref, o_ref, s_out_ref, st_s = rest
    else:
        o_ref, s_out_ref, st_s = rest
    tb = x_ref.shape[0]
    ch = GLA_CHUNK

    @pl.when(pl.program_id(1) == 0)
    def _():
        st_s[...] = s0_ref[...]

    r_i = lax.broadcasted_iota(jnp.int32, (ch, ch), 0)
    c_i = lax.broadcasted_iota(jnp.int32, (ch, ch), 1)
    if reverse:
        cum_mat = (c_i >= r_i).astype(F32)
        keep = c_i > r_i
    else:
        cum_mat = (c_i <= r_i).astype(F32)
        keep = c_i <= r_i
    lane = lax.broadcasted_iota(jnp.int32, (1, LANES), 1)
    head_lanes = (lane < GLA_DK, lane >= GLA_DK)
    nt_dims = (((1,), (1,)), ((), ()))
    chunks = range(tb // ch)
    for cidx in (reversed(chunks) if reverse else chunks):
        r0 = cidx * ch
        q = x_ref[r0:r0 + ch, 0:GLA_QK] * (GLA_DK ** -0.5)
        k = x_ref[r0:r0 + ch, GLA_QK:2 * GLA_QK]
        v = x_ref[r0:r0 + ch, 2 * GLA_QK:2 * GLA_QK + GLA_V]
        lr = x_ref[r0:r0 + ch, 2 * GLA_QK + GLA_V:GLA_IN_PAD]
        logits = jnp.dot(lr, wlr_ref[...], precision=HIGHEST, preferred_element_type=F32) + blr_ref[...]
        log_a = _log_sigmoid(logits) * (1.0 / GLA_TAU)
        cum = jnp.dot(cum_mat, log_a, precision=HIGHEST, preferred_element_type=F32)
        tot = cum[0:1] if reverse else cum[ch - 1:ch]
        qg = q * jnp.exp(cum)
        kg = k * jnp.exp(-cum)
        kd = k * jnp.exp(tot - cum)
        decay = jnp.exp(tot)
        outs = []
        for h in range(GLA_HEADS):
            sl = slice((h // 2) * LANES, (h // 2 + 1) * LANES)
            mine = head_lanes[h % 2]
            qm = jnp.where(mine, qg[:, sl], 0.0).astype(BF16)
            scores = lax.dot_general(qm, kg[:, sl].astype(BF16), nt_dims, preferred_element_type=F32)
            scores = jnp.where(keep, scores, 0.0)
            vh = v[:, h * GLA_DV:(h + 1) * GLA_DV]
            st = st_s[h]
            o = jnp.dot(scores.astype(BF16), vh.astype(BF16), preferred_element_type=F32)
            o = o + lax.dot_general(qm, st.astype(BF16), nt_dims, preferred_element_type=F32)
            kdm = jnp.where(mine, kd[:, sl], 0.0).astype(BF16)
            st_s[h] = st * decay[:, sl] + jnp.dot(vh.T.astype(BF16), kdm, preferred_element_type=F32)
            outs.append(o)
        o_all = jnp.concatenate(outs, axis=1)
        if has_acc:
            o_all = o_all + acc_ref[r0:r0 + ch, :]
        o_ref[r0:r0 + ch, :] = o_all
    s_out_ref[...] = st_s[...]


def gla_scan(x, s0, wlr, blr, reverse, acc=None, tb=512):
    b, n, cin = x.shape
    tb = min(tb, n)
    assert n % tb == 0 and tb % GLA_CHUNK == 0
    nblk = n // tb

    def seq(width):
        if reverse:
            return pl.BlockSpec((None, tb, width), lambda bi, i: (bi, nblk - 1 - i, 0))
        return pl.BlockSpec((None, tb, width), lambda bi, i: (bi, i, 0))

    state = pl.BlockSpec((None, GLA_HEADS, GLA_DV, LANES), lambda bi, i: (bi, 0, 0, 0))
    in_specs = [seq(cin), state,
                pl.BlockSpec(wlr.shape, lambda bi, i: (0, 0)), pl.BlockSpec(blr.shape, lambda bi, i: (0, 0))]
    args = [x, s0, wlr, blr]
    if acc is not None:
        in_specs.append(seq(GLA_V))
        args.append(acc)
    return pl.pallas_call(
        functools.partial(_gla_kernel, reverse=reverse, has_acc=acc is not None),
        grid=(b, nblk),
        in_specs=in_specs,
        out_specs=[seq(GLA_V), state],
        out_shape=[jax.ShapeDtypeStruct((b, n, GLA_V), F32),
                   jax.ShapeDtypeStruct((b, GLA_HEADS, GLA_DV, LANES), F32)],
        scratch_shapes=[pltpu.VMEM((GLA_HEADS, GLA_DV, LANES), F32)],
        compiler_params=_cparams(2),
        name="gla_bwd" if reverse else "gla_fwd",
    )(*args)


def gla_params(w_lr, b_lr, d):
    w = jnp.zeros((LANES, GLA_QK), F32).at[d * GLA_RANK:(d + 1) * GLA_RANK].set(w_lr[d])
    return w, b_lr[d].reshape(1, GLA_QK)


def gla_mix(x_ctx, x_lat_cols, w_lr, b_lr):
    b = x_ctx.shape[0]
    zero = jnp.zeros((b, GLA_HEADS, GLA_DV, LANES), F32)
    o_ctx = o_lat = None
    for d in range(2):
        w, bias = gla_params(w_lr, b_lr, d)
        o_ctx, s = gla_scan(x_ctx, zero, w, bias, reverse=bool(d), acc=o_ctx)
        o_lat, _ = gla_scan(x_lat_cols, s, w, bias, reverse=bool(d), acc=o_lat)
    return o_ctx, o_lat


def raster_to_column(t):
    b, n = t.shape[:2]
    return t.reshape(b, n // GRID_W, GRID_W, *t.shape[2:]).swapaxes(1, 2).reshape(t.shape)


def column_to_raster(t):
    b, n = t.shape[:2]
    return t.reshape(b, GRID_W, n // GRID_W, *t.shape[2:]).swapaxes(1, 2).reshape(t.shape)


def _merge_kernel(x_ref, gt_ref, mg_ref, yhy_ref, hrg_ref, grg_ref, ogla_ref, ggla_ref,
                  bm_ref, gn_ref, why_ref, wrg_ref, wgla_ref, wout_ref, o_ref):
    d = x_ref.shape[1]
    y_rg = hrg_ref[...] * _gelu_tanh(grg_ref[...])
    o = ogla_ref[...]
    heads = []
    for h in range(GLA_HEADS):
        oh = o[:, h * GLA_DV:(h + 1) * GLA_DV]
        heads.append(oh * lax.rsqrt(jnp.mean(oh * oh, axis=-1, keepdims=True) + NORM_EPS) * gn_ref[...])
    gg = ggla_ref[...]
    y_gla = jnp.concatenate(heads, axis=1) * (gg * _sigmoid(gg))
    gate = _sigmoid(mg_ref[...] + bm_ref[...])

    def proj(y, w_ref):
        return jnp.dot(y.astype(BF16), w_ref[...], preferred_element_type=F32)

    m = (gate[:, 0:d] * proj(yhy_ref[...], why_ref) + gate[:, d:2 * d] * proj(y_rg, wrg_ref)
         + gate[:, 2 * d:3 * d] * proj(y_gla, wgla_ref))
    o_ref[...] = x_ref[...] + gt_ref[...] * proj(m, wout_ref)


def merge_residual(x, gt, mg, y_hy, h_rg, g_rg, o_gla, g_gla, lw, tm=256):
    b, n, d = x.shape
    tm = min(tm, n)
    assert n % tm == 0

    def seq(a):
        return pl.BlockSpec((None, tm, a.shape[2]), lambda bi, i: (bi, i, 0))

    def par(a):
        return pl.BlockSpec(a.shape, lambda bi, i: (0,) * a.ndim)

    streams = [mg, y_hy, h_rg, g_rg, o_gla, g_gla]
    params = [lw["b_merge"], lw["gla_norm_g"], lw["w_hy_o"], lw["w_rg_o"], lw["w_gla_o"], lw["w_out"]]
    return pl.pallas_call(
        _merge_kernel,
        grid=(b, n // tm),
        in_specs=[seq(x), pl.BlockSpec((None, 1, d), lambda bi, i: (bi, 0, 0))]
                 + [seq(a) for a in streams] + [par(a) for a in params],
        out_specs=seq(x),
        out_shape=jax.ShapeDtypeStruct((b, n, d), F32),
        compiler_params=_cparams(2, 40 << 20),
        name="merge_residual",
    )(x, gt, *streams, *params)


def _shortconv_kernel(x_ref, w_ref, b_ref, o_ref, *, rows):
    n, c = x_ref.shape
    zero = jnp.zeros((SUBLANES, c), F32)
    for r0 in range(0, n, rows):
        cur = x_ref[r0:r0 + rows, :]
        before = x_ref[r0 - SUBLANES:r0, :] if r0 > 0 else zero
        after = x_ref[r0 + rows:r0 + rows + SUBLANES, :] if r0 + rows < n else zero
        o_ref[r0:r0 + rows, :] = (b_ref[...] + w_ref[0:1, :] * _shift_rows(cur, before, 1, False)
                                  + w_ref[1:2, :] * cur + w_ref[2:3, :] * _shift_rows(cur, after, 1, True))


def shortconv(x, w, bias):
    b, n, c = x.shape
    blk = pl.BlockSpec((None, n, LANES), lambda bi, j: (bi, 0, j))
    return pl.pallas_call(
        functools.partial(_shortconv_kernel, rows=min(n, 1024)),
        grid=(b, c // LANES),
        in_specs=[blk, pl.BlockSpec((3, LANES), lambda bi, j: (0, j)),
                  pl.BlockSpec((1, LANES), lambda bi, j: (0, j))],
        out_specs=blk,
        out_shape=jax.ShapeDtypeStruct((b, n, c), F32),
        compiler_params=_cparams(2, 40 << 20),
        name="shortconv",
    )(x, w, bias.reshape(1, c))


def _hy_filter_kernel(w1_ref, b1_ref, w2_ref, b2_ref, w3_ref, freq_ref, band_ref, delta_ref,
                      h_ref, asum_ref, *, n):
    tb = h_ref.shape[0]
    i = pl.program_id(0)
    idx = (i * tb + lax.broadcasted_iota(jnp.int32, (tb, LANES), 0)).astype(F32)
    lane = lax.broadcasted_iota(jnp.int32, (tb, LANES), 1)
    tn = idx / (n - 1)
    ang = (2.0 * math.pi / n) * idx * band_ref[...]
    feats = jnp.where(lane == 0, tn,
                      jnp.where(lane <= HY_BANDS, jnp.cos(ang),
                                jnp.where(lane <= 2 * HY_BANDS, -jnp.sin(ang), 0.0)))
    fr = freq_ref[...]

    def dense(x, w_ref):
        return jnp.dot(x, w_ref[...], precision=HIGHEST, preferred_element_type=F32)

    h = jnp.sin(fr * (dense(feats, w1_ref) + b1_ref[...]))
    h = jnp.sin(fr * (dense(h, w2_ref) + b2_ref[...]))
    h = dense(h, w3_ref) * jnp.exp(-tn[:, 0:1] * delta_ref[...])
    h_ref[...] = h

    @pl.when(i == 0)
    def _():
        asum_ref[...] = jnp.zeros_like(asum_ref)

    asum_ref[...] += jnp.sum(jnp.abs(h), axis=0, keepdims=True)


def hyena_filter(n, w1, b1, w2, b2, w3, freq, tb=256):
    nout = w3.shape[1]
    tb = min(tb, n)
    bands = np.zeros((1, LANES), np.float32)
    lin = np.linspace(1e-4, HY_BANDS - 1, HY_BANDS, dtype=np.float32)
    bands[0, 1:1 + HY_BANDS] = lin
    bands[0, 1 + HY_BANDS:1 + 2 * HY_BANDS] = lin
    deltas = np.abs(np.linspace(HY_FAST_RATE, HY_SLOW_RATE, HY_W, dtype=np.float32))
    deltas = np.tile(deltas, nout // HY_W).reshape(1, nout)
    w1p = jnp.zeros((LANES, HY_FFN), F32).at[:w1.shape[0]].set(w1)
    params = [w1p, b1.reshape(1, -1), w2, b2.reshape(1, -1), w3, freq.reshape(1, -1),
              jnp.asarray(bands), jnp.asarray(deltas)]
    return pl.pallas_call(
        functools.partial(_hy_filter_kernel, n=n),
        grid=(n // tb,),
        in_specs=[pl.BlockSpec(a.shape, lambda i: (0, 0)) for a in params],
        out_specs=[pl.BlockSpec((tb, nout), lambda i: (i, 0)), pl.BlockSpec((1, nout), lambda i: (0, 0))],
        out_shape=[jax.ShapeDtypeStruct((n, nout), F32), jax.ShapeDtypeStruct((1, nout), F32)],
        compiler_params=_cparams(1),
        name="hyena_filter",
    )(*params)


def _filter_taps(hraw, n):
    h = hraw.reshape(n, 2, 2, HY_W)
    h_fwd = jnp.moveaxis(h[:, :, 0], 1, 0)
    h_bwd = jnp.moveaxis(h[:, :, 1], 1, 0)
    zeros = jnp.zeros_like(h_fwd)
    taps_f = jnp.concatenate([h_fwd, zeros], axis=1)
    taps_b = jnp.concatenate([h_bwd[:, :1], zeros, h_bwd[:, :0:-1]], axis=1)
    return taps_f, taps_b


def _cis(num, den):
    ang = (2.0 * math.pi / den) * (num % den).astype(F32)
    return jnp.cos(ang), jnp.sin(ang)


def _dft_tables(n):
    big = 2 * n
    q = int(round(math.sqrt(big)))
    assert q * q == big and q % (2 * SUBLANES) == 0
    ar = jnp.arange(q, dtype=jnp.int32)
    num = ar[None, :, None] * (q * ar[None, None, :] + ar[:, None, None])
    c, s = _cis(num, big)
    w1 = jnp.concatenate([c, -s], axis=1)
    ct, st = jnp.swapaxes(c, 1, 2)[:, :q // 2], jnp.swapaxes(s, 1, 2)[:, :q // 2]
    v = jnp.concatenate([ct, -st], axis=2) * (1.0 / big)
    c2, s2 = _cis(ar[:, None] * ar[None, :], q)
    f2 = jnp.concatenate([jnp.concatenate([c2, s2], axis=1), jnp.concatenate([-s2, c2], axis=1)], axis=0)
    g2 = jnp.concatenate([jnp.concatenate([c2, -s2], axis=1), jnp.concatenate([s2, c2], axis=1)], axis=0)
    return {"q": q, "w1": w1, "v": v, "f2": f2, "g2": g2}


def _level_kernel(w_ref, x_ref, *rest, n_add, has_gate, precise):
    rest = list(rest)
    x2_ref = rest.pop(0) if n_add else None
    if has_gate:
        src_ref, gate_ref, skip_ref = rest[:3]
        rest = rest[3:]
    o_ref = rest[0]
    for s in range(w_ref.shape[0]):
        x = x_ref[s]
        if n_add:
            x = x + x2_ref[s]
        if precise:
            y = jnp.dot(w_ref[s], x, precision=HIGHEST, preferred_element_type=F32)
        else:
            y = jnp.dot(w_ref[s], x.astype(BF16), preferred_element_type=F32)
        if has_gate:
            y = gate_ref[s] * (y + src_ref[s] * skip_ref[...])
        o_ref[s] = y.astype(o_ref.dtype)


def dft_level(w, x, x_group=0, add=None, gate=None, out_dtype=F32, precise=False, sb=8):
    b, q, k, _ = x.shape
    m = w.shape[1]
    c = HY_W
    sb = min(sb, q)

    def seq(rows, group):
        return pl.BlockSpec((None, sb, rows, c), lambda i, bi: (bi, i, 0, group))

    in_specs = [pl.BlockSpec((sb, m, k), lambda i, bi: (i, 0, 0)), seq(k, x_group)]
    args = [w, x]
    if add is not None:
        in_specs.append(seq(k, 0))
        args.append(add)
    if gate is not None:
        src, src_group, gates, gate_group, skip = gate
        in_specs += [seq(m, src_group), seq(m, gate_group), pl.BlockSpec((1, c), lambda i, bi: (0, 0))]
        args += [src, gates, skip]
    return pl.pallas_call(
        functools.partial(_level_kernel, n_add=add is not None, has_gate=gate is not None,
                          precise=precise),
        grid=(q // sb, b),
        in_specs=in_specs,
        out_specs=seq(m, 0),
        out_shape=jax.ShapeDtypeStruct((b, q, m, c), out_dtype),
        compiler_params=_cparams(2, 40 << 20),
        name="dft_level",
    )(*args)


def _dft_mid_kernel(a_ref, h_ref, f_ref, g_ref, o_ref):
    q = a_ref.shape[1] // 2
    for j in range(a_ref.shape[0]):
        x = jnp.dot(f_ref[...], a_ref[j], preferred_element_type=F32)
        xr, xi = x[:q], x[q:]
        hr, hi = h_ref[j, :q, :], h_ref[j, q:, :]
        y = jnp.concatenate([xr * hr - xi * hi, xr * hi + xi * hr], axis=0)
        o_ref[j] = jnp.dot(g_ref[...], y.astype(BF16), preferred_element_type=F32).astype(o_ref.dtype)


def dft_mid(a, spec, f2, g2, kb=8):
    b, q, q2, c = a.shape
    blk = pl.BlockSpec((None, kb, q2, c), lambda i, bi: (bi, i, 0, 0))
    mat = pl.BlockSpec((q2, q2), lambda i, bi: (0, 0))
    return pl.pallas_call(
        _dft_mid_kernel,
        grid=(q // kb, b),
        in_specs=[blk, pl.BlockSpec((kb, q2, c), lambda i, bi: (i, 0, 0)), mat, mat],
        out_specs=blk,
        out_shape=jax.ShapeDtypeStruct(a.shape, BF16),
        compiler_params=_cparams(2, 40 << 20),
        name="dft_mid",
    )(a, spec, f2, g2)


def _dft_spec_kernel(a_ref, f_ref, asum_ref, o_ref):
    inv = 1.0 / (asum_ref[0:1, :] + asum_ref[1:2, :] + 1e-6)
    for j in range(a_ref.shape[0]):
        o_ref[j] = jnp.dot(f_ref[...], a_ref[j], precision=HIGHEST, preferred_element_type=F32) * inv


def dft_spec(a, f2, asum, kb=8):
    o, q, q2, c = a.shape
    blk = pl.BlockSpec((None, kb, q2, c), lambda i, oi: (oi, i, 0, 0))
    return pl.pallas_call(
        _dft_spec_kernel,
        grid=(q // kb, o),
        in_specs=[blk, pl.BlockSpec((q2, q2), lambda i, oi: (0, 0)),
                  pl.BlockSpec((None, 2, c), lambda i, oi: (oi, 0, 0))],
        out_specs=blk,
        out_shape=jax.ShapeDtypeStruct(a.shape, F32),
        compiler_params=_cparams(2, 40 << 20),
        name="dft_spec",
    )(a, f2, asum)


def _swap_levels(a):
    b, q, q2, c = a.shape
    return a.reshape(b, q, 2, q, c).transpose(0, 3, 2, 1, 4).reshape(b, q, q2, c)


def hyena_long(u, hraw, asum, skip, tabs):
    b, n, _ = u.shape
    q = tabs["q"]
    c = HY_W
    taps_f, taps_b = _filter_taps(hraw, n)

    def to_levels(t, rows):
        return t.reshape(t.shape[0], rows, q, t.shape[2]).swapaxes(1, 2)

    a = dft_level(tabs["w1"], to_levels(taps_f, q), add=to_levels(taps_b, q), precise=True)
    spec = dft_spec(_swap_levels(a), tabs["f2"], asum.reshape(2, 2, c))
    w1d = tabs["w1"][:, :, :q // 2].astype(BF16)
    vd = tabs["v"].astype(BF16)
    f2, g2 = tabs["f2"].astype(BF16), tabs["g2"].astype(BF16)
    u_t = to_levels(u, q // 2)
    src, group = u_t, 0
    for order in range(2):
        a = dft_level(w1d, src, x_group=group, out_dtype=BF16)
        cm = dft_mid(_swap_levels(a), spec[order], f2, g2)
        src = dft_level(vd, _swap_levels(cm),
                        gate=(src, group, u_t, order + 1, skip[order].reshape(1, c)))
        group = 0
    return src.swapaxes(1, 2).reshape(b, n, c)


def _dft_small_kernel(u_ref, gate_ref, skip_ref, h_ref, f_ref, g_ref, o_ref):
    u = u_ref[...]
    nb = h_ref.shape[0] // 2
    x = jnp.dot(f_ref[...], u, precision=HIGHEST, preferred_element_type=F32)
    xr, xi = x[:nb], x[nb:]
    hr, hi = h_ref[:nb, :], h_ref[nb:, :]
    y = jnp.concatenate([xr * hr - xi * hi, xr * hi + xi * hr], axis=0)
    conv = jnp.dot(g_ref[...], y, precision=HIGHEST, preferred_element_type=F32)
    o_ref[...] = gate_ref[...] * (conv + u * skip_ref[...])


def _dft_small_spec_kernel(tf_ref, tb_ref, f_ref, asum_ref, o_ref):
    inv = 1.0 / (asum_ref[0:1, :] + asum_ref[1:2, :] + 1e-6)
    o_ref[...] = jnp.dot(f_ref[...], tf_ref[...] + tb_ref[...], precision=HIGHEST,
                         preferred_element_type=F32) * inv


def hyena_short_seq(u, hraw, asum, skip):
    b, n, _ = u.shape
    big = 2 * n
    c = HY_W
    ar = jnp.arange(big, dtype=jnp.int32)
    cs, sn = _cis(ar[:, None] * ar[None, :], big)
    f_full = jnp.concatenate([cs, -sn], axis=0)
    g_half = jnp.concatenate([cs[:n], -sn[:n]], axis=1) * (1.0 / big)
    taps_f, taps_b = _filter_taps(hraw, n)
    tap = pl.BlockSpec((None, big, c), lambda o: (o, 0, 0))
    spec = pl.pallas_call(
        _dft_small_spec_kernel,
        grid=(2,),
        in_specs=[tap, tap, pl.BlockSpec((2 * big, big), lambda o: (0, 0)),
                  pl.BlockSpec((None, 2, c), lambda o: (o, 0, 0))],
        out_specs=pl.BlockSpec((None, 2 * big, c), lambda o: (o, 0, 0)),
        out_shape=jax.ShapeDtypeStruct((2, 2 * big, c), F32),
        compiler_params=_cparams(1),
        name="dft_small_spec",
    )(taps_f, taps_b, f_full, asum.reshape(2, 2, c))
    f_data = f_full[:, :n]
    src, group = u, 0
    for order in range(2):
        src = pl.pallas_call(
            _dft_small_kernel,
            grid=(b,),
            in_specs=[pl.BlockSpec((None, n, c), functools.partial(lambda bi, g: (bi, 0, g), g=group)),
                      pl.BlockSpec((None, n, c), functools.partial(lambda bi, g: (bi, 0, g), g=order + 1)),
                      pl.BlockSpec((1, c), lambda bi: (0, 0)),
                      pl.BlockSpec((2 * big, c), lambda bi: (0, 0)),
                      pl.BlockSpec((2 * big, n), lambda bi: (0, 0)),
                      pl.BlockSpec((n, 2 * big), lambda bi: (0, 0))],
            out_specs=pl.BlockSpec((None, n, c), lambda bi: (bi, 0, 0)),
            out_shape=jax.ShapeDtypeStruct((b, n, c), F32),
            compiler_params=_cparams(1),
            name="dft_small",
        )(src, u, skip[order].reshape(1, c), spec[order], f_data, g_half)
        group = 0
    return src


HY_IN = 3 * HY_W
IN_GROUPS = (HY_IN, RG_W, RG_W, GLA_IN, GLA_V, 3 * D_MODEL)


def _split_w_in(w):
    parts, start = [], 0
    for width in IN_GROUPS:
        parts.append(w[:, start:start + width])
        start += width
    parts[3] = jnp.pad(parts[3], ((0, 0), (0, GLA_IN_PAD - GLA_IN)))
    return [p.astype(BF16) for p in parts]


def _project(h, w):
    b, n, d = h.shape
    return matmul(h.reshape(b * n, d), w).reshape(b, n, w.shape[1])


def kernel(x, c, ctx, c_ctx, w_mod, b_mod, g_norm_mix, g_norm_ffn, w_in, hy_conv_w, hy_conv_b,
           hy_w1, hy_b1, hy_w2, hy_b2, hy_w3, hy_freq, hy_skip, rg_conv_w, rg_conv_b, rg_wa, rg_ba,
           rg_wx, rg_bx, rg_lambda, gla_w_lr, gla_b_lr, gla_norm_g, w_hy_o, w_rg_o, w_gla_o, b_merge,
           w_out, peer_wq, peer_keys, peer_u, peer_v, g_final):
    b, n, d = x.shape
    n_ctx = ctx.shape[1]
    depth = w_mod.shape[0]
    cond = jnp.concatenate([c, c_ctx[None, :]], axis=0)
    cond = jnp.pad(cond, ((0, -(b + 1) % SUBLANES), (0, 0)))
    tabs = _dft_tables(n)
    x_lat, x_ctx = x, ctx
    for l in range(depth):
        need_ctx = l < depth - 1
        mod = matmul(cond, w_mod[l].astype(BF16), bias=b_mod[l], silu_in=True)
        sh1, sc1, gt1, sh2, sc2, gt2 = [mod[:b, i * d:(i + 1) * d].reshape(b, 1, d) for i in range(6)]
        csh1, csc1, cgt1, csh2, csc2, cgt2 = [
            jnp.broadcast_to(mod[b:b + 1, i * d:(i + 1) * d].reshape(1, 1, d), (b, 1, d)) for i in range(6)]
        w_groups = _split_w_in(w_in[l])
        lw = {"b_merge": b_merge[l].reshape(1, 3 * d), "gla_norm_g": gla_norm_g[l].reshape(1, GLA_DV),
              "w_hy_o": w_hy_o[l].astype(BF16), "w_rg_o": w_rg_o[l].astype(BF16),
              "w_gla_o": w_gla_o[l].astype(BF16), "w_out": w_out[l].astype(BF16)}
        rg_par = [rglru_params(rg_conv_w[l], rg_conv_b[l], rg_wa[l], rg_ba[l], rg_wx[l], rg_bx[l],
                               rg_lambda[l], dd) for dd in range(2)]
        filt = (hy_w1[l], hy_b1[l], hy_w2[l], hy_b2[l], hy_w3[l], hy_freq[l])
        u_pack, v_pack = _pack_table(peer_u[l]), _pack_table(peer_v[l])

        h_lat = normmod(x_lat, g_norm_mix[l], sh1, sc1, BF16)
        h_ctx = normmod(x_ctx, g_norm_mix[l], csh1, csc1, BF16)
        hy_l, rgx_l, rgg_l, gla_l, glag_l, mg_l = [_project(h_lat, w) for w in w_groups]
        ctx_groups = range(6) if need_ctx else (1, 3)
        ctx_proj = {i: _project(h_ctx, w_groups[i]) for i in ctx_groups}

        hraw, asum = hyena_filter(n, *filt)
        y_hy_l = hyena_long(shortconv(hy_l, hy_conv_w[l], hy_conv_b[l]), hraw, asum, hy_skip[l], tabs)
        h_rg_c, h_rg_l = rglru_mix(ctx_proj[1], rgx_l, rg_par)
        o_gla_c, o_gla_l = gla_mix(ctx_proj[3], raster_to_column(gla_l), gla_w_lr[l], gla_b_lr[l])
        x_lat = merge_residual(x_lat, gt1, mg_l, y_hy_l, h_rg_l, rgg_l, column_to_raster(o_gla_l),
                               glag_l, lw)
        h2 = normmod(x_lat, g_norm_ffn[l], sh2, sc2, F32)
        x_lat = gated_residual(x_lat, gt2, peer_ffn(h2, peer_wq[l], peer_keys[l], u_pack, v_pack))
        if need_ctx:
            hraw_c, asum_c = hyena_filter(n_ctx, *filt)
            y_hy_c = hyena_short_seq(shortconv(ctx_proj[0], hy_conv_w[l], hy_conv_b[l]), hraw_c, asum_c,
                                     hy_skip[l])
            x_ctx = merge_residual(x_ctx, cgt1, ctx_proj[5], y_hy_c, h_rg_c, ctx_proj[2], o_gla_c,
                                   ctx_proj[4], lw)
            h2c = normmod(x_ctx, g_norm_ffn[l], csh2, csc2, F32)
            x_ctx = gated_residual(x_ctx, cgt2, peer_ffn(h2c, peer_wq[l], peer_keys[l], u_pack, v_pack))
    zero = jnp.zeros((b, 1, d), F32)
    return normmod(x_lat, g_final, zero, zero, F32, mod=False)
```

```python
import functools
import math

import numpy as np
import jax
import jax.numpy as jnp
from jax import lax
from jax.experimental import pallas as pl
from jax.experimental.pallas import tpu as pltpu

F32 = jnp.float32
BF16 = jnp.bfloat16
HIGHEST = lax.Precision.HIGHEST

D_MODEL = 1024
DEPTH = 4
GRID_W = 64
NORM_EPS = 1e-6

HY_W = 512
HY_BANDS = 16
HY_FFN = 64
HY_FAST_RATE = math.log(1e-2) / 0.3
HY_SLOW_RATE = math.log(1e-2) / 1.5

RG_W = 512
RG_BLOCKS = 8
RG_C = 8.0

GLA_HEADS = 4
GLA_DK = 64
GLA_DV = 128
GLA_QK = GLA_HEADS * GLA_DK
GLA_V = GLA_HEADS * GLA_DV
GLA_RANK = 16
GLA_TAU = 16.0
GLA_CHUNK = 64
GLA_IN = 2 * GLA_QK + GLA_V + 2 * GLA_RANK
GLA_IN_PAD = 2 * GLA_QK + GLA_V + 128

PEER_HEADS = 8
PEER_NKEYS = 128
PEER_TOPK = 16
PEER_HALF = 128
PEER_SEL = PEER_HEADS * PEER_TOPK

V7X_VMEM_BYTES = 64 * 1024 * 1024
SUBLANES = 8
LANES = 128


def _cparams(n_grid, vmem_bytes=None):
    kw = dict(dimension_semantics=("arbitrary",) * n_grid)
    if vmem_bytes is not None:
        assert vmem_bytes < V7X_VMEM_BYTES
        kw["vmem_limit_bytes"] = int(vmem_bytes)
    return pltpu.CompilerParams(**kw)


def _gelu_tanh(x):
    return 0.5 * x * (1.0 + jnp.tanh(math.sqrt(2.0 / math.pi) * (x + 0.044715 * (x * x * x))))


def _sigmoid(x):
    return 1.0 / (1.0 + jnp.exp(-x))


def _log_sigmoid(x):
    return jnp.minimum(x, 0.0) - jnp.log(1.0 + jnp.exp(-jnp.abs(x)))


def _mm_kernel(x_ref, w_ref, *rest, silu_in, has_bias):
    o_ref = rest[-1]
    x = x_ref[...]
    if silu_in:
        x = x * _sigmoid(x)
    acc = jnp.dot(x.astype(BF16), w_ref[...], preferred_element_type=F32)
    if has_bias:
        acc = acc + rest[0][...]
    o_ref[...] = acc.astype(o_ref.dtype)


def matmul(x, w, bias=None, silu_in=False, tm=512):
    m, k = x.shape
    n = w.shape[1]
    tm = min(tm, m)
    assert m % tm == 0
    in_specs = [pl.BlockSpec((tm, k), lambda i: (i, 0)), pl.BlockSpec((k, n), lambda i: (0, 0))]
    args = [x, w]
    if bias is not None:
        in_specs.append(pl.BlockSpec((1, n), lambda i: (0, 0)))
        args.append(bias.reshape(1, n))
    est = 2 * (tm * k * x.dtype.itemsize + k * n * 2 + tm * n * 4) + (4 << 20)
    return pl.pallas_call(
        functools.partial(_mm_kernel, silu_in=silu_in, has_bias=bias is not None),
        grid=(m // tm,),
        in_specs=in_specs,
        out_specs=pl.BlockSpec((tm, n), lambda i: (i, 0)),
        out_shape=jax.ShapeDtypeStruct((m, n), F32),
        compiler_params=_cparams(1, est),
        name="matmul",
    )(*args)


def _normmod_kernel(x_ref, g_ref, sh_ref, sc_ref, o_ref, *, mod):
    x = x_ref[...]
    y = x * lax.rsqrt(jnp.mean(x * x, axis=-1, keepdims=True) + NORM_EPS) * g_ref[...]
    if mod:
        y = y * (1.0 + sc_ref[...]) + sh_ref[...]
    o_ref[...] = y.astype(o_ref.dtype)


def normmod(x, g, shift, scale, out_dtype, mod=True, tm=512):
    b, l, d = x.shape
    tm = min(tm, l)
    assert l % tm == 0
    vec = pl.BlockSpec((None, 1, d), lambda bi, i: (bi, 0, 0))
    return pl.pallas_call(
        functools.partial(_normmod_kernel, mod=mod),
        grid=(b, l // tm),
        in_specs=[pl.BlockSpec((None, tm, d), lambda bi, i: (bi, i, 0)),
                  pl.BlockSpec((1, d), lambda bi, i: (0, 0)), vec, vec],
        out_specs=pl.BlockSpec((None, tm, d), lambda bi, i: (bi, i, 0)),
        out_shape=jax.ShapeDtypeStruct((b, l, d), out_dtype),
        compiler_params=_cparams(2),
        name="normmod",
    )(x, g.reshape(1, d), shift, scale)


def _assemble_rows(rows, n):
    t = rows[0].shape[1]
    rid = lax.broadcasted_iota(jnp.int32, (n, t), 0)
    out = jnp.zeros((n, t), rows[0].dtype)
    for r in range(n):
        out = jnp.where(rid == r, rows[r], out)
    return out


def _extract_topk(s, rowid, k):
    vals, ids = [], []
    for _ in range(k):
        m = jnp.max(s, axis=0, keepdims=True)
        first = jnp.min(jnp.where(s == m, rowid, 1e9), axis=0, keepdims=True)
        s = jnp.where(rowid == first, -jnp.inf, s)
        vals.append(m)
        ids.append(first)
    return vals, ids


def _split_bf16(x):
    hi = x.astype(BF16)
    return hi, (x - hi.astype(F32)).astype(BF16)


def _dot3(a_hi, a_lo, b_hi, b_lo, dims):
    def one(a, b):
        return lax.dot_general(a, b, dims, preferred_element_type=F32)
    return one(a_hi, b_hi) + (one(a_hi, b_lo) + one(a_lo, b_hi))


def _peer_select_kernel(h_ref, wqh_ref, wql_ref, kh_ref, kl_ref, exp_ref, wgt_ref, q_ref):
    h_hi, h_lo = _split_bf16(h_ref[...])
    q_ref[...] = _dot3(h_hi, h_lo, wqh_ref[...], wql_ref[...], (((1,), (0,)), ((), ())))

    def lane_block(j, carry):
        r0 = pl.multiple_of(j * LANES, LANES)
        rows, weights = _select_tokens(q_ref[pl.ds(r0, LANES), :], kh_ref, kl_ref)
        exp_ref[pl.ds(r0, LANES), :] = rows
        wgt_ref[pl.ds(r0, LANES), :] = weights
        return carry

    lax.fori_loop(0, h_ref.shape[0] // LANES, lane_block, 0)


def _select_tokens(q, kh_ref, kl_ref):
    tt = q.shape[0]
    k = PEER_TOPK
    key_id = lax.broadcasted_iota(jnp.int32, (PEER_NKEYS, tt), 0).astype(F32)
    row8 = lax.broadcasted_iota(jnp.int32, (SUBLANES, tt), 0)
    row16 = lax.broadcasted_iota(jnp.int32, (2 * SUBLANES, tt), 0)
    exp_blocks, wgt_blocks = [], []
    for h in range(PEER_HEADS):
        tops = []
        for p in range(2):
            col = (h * 2 + p) * PEER_HALF
            q_hi, q_lo = _split_bf16(q[:, col:col + PEER_HALF])
            s = _dot3(kh_ref[h, p], kl_ref[h, p], q_hi, q_lo, (((1,), (1,)), ((), ())))
            vals, ids = _extract_topk(s, key_id, k)
            tops.append((_assemble_rows(vals, k), _assemble_rows(ids, k)))
        (a, ia), (b, ib) = tops
        blocks, flat = [a[0:1] + b], [row16.astype(F32)]
        for i in range(1, 8):
            nj = k // (i + 1)
            blocks.append(jnp.where(row8 < nj, a[i:i + 1] + b[0:8], -jnp.inf))
            flat.append((row8 + i * k).astype(F32))
        blocks.append(a[8:16] + b[0:1])
        flat.append(((row8 + 8) * k).astype(F32))
        cand = jnp.concatenate(blocks, axis=0)
        cand_id = jnp.concatenate(flat, axis=0)
        vals, ids = _extract_topk(cand, cand_id, k)
        best = _assemble_rows(vals, k)
        fl = _assemble_rows(ids, k)
        fi = jnp.floor(fl * (1.0 / k))
        fj = fl - fi * k
        ei = jnp.zeros_like(fl)
        ej = jnp.zeros_like(fl)
        for r in range(k):
            ei = jnp.where(fi == r, ia[r:r + 1], ei)
            ej = jnp.where(fj == r, ib[r:r + 1], ej)
        e = jnp.exp(best - jnp.max(best, axis=0, keepdims=True))
        wgt_blocks.append(e / jnp.sum(e, axis=0, keepdims=True))
        exp_blocks.append((ei * PEER_NKEYS + ej) * ROW_WORDS)
    return (jnp.concatenate(exp_blocks, axis=0).T.astype(jnp.int32),
            jnp.concatenate(wgt_blocks, axis=0).T)


def peer_select(h, wq, keys, tt=512):
    t, d = h.shape
    tt = min(tt, t)
    assert t % tt == 0 and tt % LANES == 0
    nq = wq.shape[1]
    wq_hi, wq_lo = _split_bf16(wq)
    k_hi, k_lo = _split_bf16(keys)
    wspec = pl.BlockSpec((d, nq), lambda i: (0, 0))
    kspec = pl.BlockSpec(keys.shape, lambda i: (0, 0, 0, 0))
    return pl.pallas_call(
        _peer_select_kernel,
        grid=(t // tt,),
        in_specs=[pl.BlockSpec((tt, d), lambda i: (i, 0)), wspec, wspec, kspec, kspec],
        out_specs=[pl.BlockSpec((tt, PEER_SEL), lambda i: (i, 0)),
                   pl.BlockSpec((tt, PEER_SEL), lambda i: (i, 0))],
        out_shape=[jax.ShapeDtypeStruct((t, PEER_SEL), jnp.int32),
                   jax.ShapeDtypeStruct((t, PEER_SEL), F32)],
        scratch_shapes=[pltpu.VMEM((tt, nq), F32)],
        compiler_params=_cparams(1, 2 * (2 * d * nq + 2 * keys.size) * 2 + 3 * tt * (d + nq) * 4 + (8 << 20)),
        name="peer_select",
    )(h, wq_hi, wq_lo, k_hi, k_lo)


ROW_WORDS = D_MODEL // 2 // LANES


def _unpack_pair(w):
    lo = lax.bitcast_convert_type(lax.shift_left(w, jnp.int32(16)), F32)
    hi = lax.bitcast_convert_type(jnp.bitwise_and(w, jnp.int32(-65536)), F32)
    return lo, hi


PEER_CHUNK = 64
PEER_NCHUNK = PEER_SEL // PEER_CHUNK
PEER_GROUP = 8


def _rows_to_tiles(x8):
    row8 = lax.broadcasted_iota(jnp.int32, (SUBLANES, LANES), 0)
    tiles = []
    for tk in range(SUBLANES):
        tile = jnp.zeros((SUBLANES, LANES), x8.dtype)
        for r in range(SUBLANES):
            tile = jnp.where(row8 == r, x8[tk:tk + 1, r * LANES:(r + 1) * LANES], tile)
        tiles.append(tile)
    return tiles


def _tiles_to_rows(tiles):
    row8 = lax.broadcasted_iota(jnp.int32, (SUBLANES, LANES), 0)
    cols = []
    for r in range(SUBLANES):
        col = jnp.zeros((SUBLANES, LANES), tiles[0].dtype)
        for tk in range(SUBLANES):
            col = jnp.where(row8 == tk, tiles[tk][r:r + 1, :], col)
        cols.append(col)
    return jnp.concatenate(cols, axis=1)


def _peer_act_kernel(idx_ref, x_ref, wgt_ref, tab_ref, coef_ref, p_ref, xs_ref):
    tt = x_ref.shape[0]

    def group(g, carry):
        t0 = pl.multiple_of(g * PEER_GROUP, PEER_GROUP)
        for tk, tile in enumerate(_rows_to_tiles(x_ref[pl.ds(t0, PEER_GROUP), :])):
            xs_ref[tk] = tile

        def token(tk, c):
            xl = xs_ref[tk, 0:ROW_WORDS, :]
            xh = xs_ref[tk, ROW_WORDS:2 * ROW_WORDS, :]

            def chunk(ci, c2):
                base = ((t0 + tk) * PEER_NCHUNK + ci) * PEER_CHUNK
                for j in range(PEER_CHUNK):
                    r = pl.multiple_of(idx_ref[base + j], ROW_WORDS)
                    lo, hi = _unpack_pair(tab_ref[pl.ds(r, ROW_WORDS), :])
                    p_ref[tk * PEER_NCHUNK + ci, j * ROW_WORDS:(j + 1) * ROW_WORDS, :] = lo * xl + hi * xh
                return c2

            return lax.fori_loop(0, PEER_NCHUNK, chunk, c)

        lax.fori_loop(0, PEER_GROUP, token, 0)
        for tk in range(PEER_GROUP):
            tok = p_ref.at[tk * PEER_NCHUNK:(tk + 1) * PEER_NCHUNK]
            parts = [tok[:, pl.ds(r, PEER_CHUNK, stride=ROW_WORDS), :].reshape(PEER_SEL, LANES)
                     for r in range(ROW_WORDS)]
            per_lane = (parts[0] + parts[1]) + (parts[2] + parts[3])
            act = jnp.sum(per_lane.T, axis=0, keepdims=True)
            coef_ref[t0 + tk] = wgt_ref[t0 + tk] * _gelu_tanh(act)
        return carry

    lax.fori_loop(0, tt // PEER_GROUP, group, 0)


def _peer_out_kernel(idx_ref, coef_ref, res_ref, gate_ref, tab_ref, out_ref, ys_ref):
    tt = out_ref.shape[0]
    zero = jnp.zeros((ROW_WORDS, LANES), F32)

    def group(g, carry):
        t0 = pl.multiple_of(g * PEER_GROUP, PEER_GROUP)

        def token(tk, c):
            def chunk(ci, acc):
                acc = list(acc)
                base = ((t0 + tk) * PEER_NCHUNK + ci) * PEER_CHUNK
                for j in range(PEER_CHUNK):
                    r = pl.multiple_of(idx_ref[base + j], ROW_WORDS)
                    lo, hi = _unpack_pair(tab_ref[pl.ds(r, ROW_WORDS), :])
                    cf = coef_ref[base + j]
                    acc[2 * (j % 2)] = acc[2 * (j % 2)] + cf * lo
                    acc[2 * (j % 2) + 1] = acc[2 * (j % 2) + 1] + cf * hi
                return tuple(acc)

            acc = lax.fori_loop(0, PEER_NCHUNK, chunk, (zero, zero, zero, zero))
            ys_ref[tk, 0:ROW_WORDS, :] = acc[0] + acc[2]
            ys_ref[tk, ROW_WORDS:2 * ROW_WORDS, :] = acc[1] + acc[3]
            return c

        lax.fori_loop(0, PEER_GROUP, token, 0)
        rows = _tiles_to_rows([ys_ref[tk] for tk in range(PEER_GROUP)])
        out_ref[pl.ds(t0, PEER_GROUP), :] = res_ref[pl.ds(t0, PEER_GROUP), :] + gate_ref[...] * rows
        return carry

    lax.fori_loop(0, tt // PEER_GROUP, group, 0)


def _pack_table(tab):
    e, d = tab.shape
    pairs = jnp.moveaxis(tab.astype(BF16).reshape(e, 2, d // 2), 1, 2)
    return lax.bitcast_convert_type(pairs, jnp.int32).reshape(e * ROW_WORDS, LANES)


def _table_spec(shape):
    return pl.BlockSpec(shape, lambda i: (0, 0), pipeline_mode=pl.Buffered(1))


def peer_experts(h, expert, weight, u_pack, v_pack, res, gate, tokens_per_gate, tt=128):
    t, d = h.shape
    assert t % tt == 0 and tokens_per_gate % tt == 0 and tt % PEER_GROUP == 0
    assert PEER_GROUP == SUBLANES and d == 2 * ROW_WORDS * LANES
    vmem = u_pack.size * 4 + (16 << 20)
    smem_flat = pl.BlockSpec((tt * PEER_SEL,), lambda i: (i,), memory_space=pltpu.SMEM)
    xspec = pl.BlockSpec((tt, d), lambda i: (i, 0))
    rowspec = pl.BlockSpec((tt, 1, PEER_SEL), lambda i: (i, 0, 0))
    tile_scratch = pltpu.VMEM((PEER_GROUP, SUBLANES, LANES), F32)
    rows = expert.reshape(t * PEER_SEL)
    coef = pl.pallas_call(
        _peer_act_kernel,
        grid=(t // tt,),
        in_specs=[smem_flat, xspec, rowspec, _table_spec(u_pack.shape)],
        out_specs=rowspec,
        out_shape=jax.ShapeDtypeStruct((t, 1, PEER_SEL), F32),
        scratch_shapes=[pltpu.VMEM((PEER_GROUP * PEER_NCHUNK, PEER_CHUNK * ROW_WORDS, LANES), F32),
                        tile_scratch],
        compiler_params=_cparams(1, vmem),
        name="peer_act",
    )(rows, h, weight.reshape(t, 1, PEER_SEL), u_pack)
    return pl.pallas_call(
        _peer_out_kernel,
        grid=(t // tt,),
        in_specs=[smem_flat, smem_flat, xspec,
                  pl.BlockSpec((None, 1, d), lambda i: (i * tt // tokens_per_gate, 0, 0)),
                  _table_spec(v_pack.shape)],
        out_specs=xspec,
        out_shape=jax.ShapeDtypeStruct((t, d), F32),
        scratch_shapes=[tile_scratch],
        compiler_params=_cparams(1, vmem),
        name="peer_out",
    )(rows, coef.reshape(t * PEER_SEL), res, gate, v_pack)


def peer_residual(x, gate, h, wq, keys, u_pack, v_pack):
    b, n, d = h.shape
    hf = h.reshape(b * n, d)
    expert, weight = peer_select(hf, wq, keys)
    out = peer_experts(hf, expert, weight, u_pack, v_pack, x.reshape(b * n, d), gate, n)
    return out.reshape(b, n, d)


RG_CONV = 4


def _shift_rows(cur, halo, k, reverse):
    tb = cur.shape[0]
    row8 = lax.broadcasted_iota(jnp.int32, (SUBLANES, cur.shape[1]), 0)
    if not reverse:
        rolled = pltpu.roll(cur, k, axis=0)
        first = jnp.where(row8 < k, pltpu.roll(halo, k, axis=0), rolled[0:SUBLANES])
        return jnp.concatenate([first, rolled[SUBLANES:]], axis=0)
    rolled = pltpu.roll(cur, tb - k, axis=0)
    last = jnp.where(row8 >= SUBLANES - k, pltpu.roll(halo, SUBLANES - k, axis=0),
                     rolled[tb - SUBLANES:])
    return jnp.concatenate([rolled[:tb - SUBLANES], last], axis=0)


def _rglru_kernel(u_ref, h0_ref, cw_ref, cb_ref, wa_ref, ba_ref, wx_ref, bx_ref, lam_ref, *rest,
                  reverse, has_acc):
    if has_acc:
        acc_ref, out_ref, hlast_ref, a_s, b_s, hp_s, halo_s = rest
    else:
        out_ref, hlast_ref, a_s, b_s, hp_s, halo_s = rest
    tb, c = u_ref.shape

    @pl.when(pl.program_id(1) == 0)
    def _():
        hp_s[...] = h0_ref[...]
        halo_s[...] = jnp.zeros_like(halo_s)

    cur = u_ref[...]
    halo = halo_s[...]
    xc = cb_ref[...] + cw_ref[RG_CONV - 1:RG_CONV, :] * cur
    for k in range(1, RG_CONV):
        xc = xc + cw_ref[RG_CONV - 1 - k:RG_CONV - k, :] * _shift_rows(cur, halo, k, reverse)
    halo_s[...] = cur[0:SUBLANES] if reverse else cur[tb - SUBLANES:]

    xb = xc.astype(BF16)
    gate_r = _sigmoid(jnp.dot(xb, wa_ref[...], preferred_element_type=F32) + ba_ref[...])
    gate_i = _sigmoid(jnp.dot(xb, wx_ref[...], preferred_element_type=F32) + bx_ref[...])
    lam = lam_ref[...]
    softplus_neg = jnp.maximum(-lam, 0.0) + jnp.log(1.0 + jnp.exp(-jnp.abs(lam)))
    a = jnp.exp(-RG_C * gate_r * softplus_neg)
    a_s[...] = a
    b_s[...] = jnp.sqrt(1.0 - a * a) * (gate_i * xc)

    row8 = lax.broadcasted_iota(jnp.int32, (SUBLANES, c), 0)
    nt = tb // SUBLANES

    def step(j, hp):
        jj = nt - 1 - j if reverse else j
        r0 = pl.multiple_of(jj * SUBLANES, SUBLANES)
        av = a_s[pl.ds(r0, SUBLANES), :]
        bv = b_s[pl.ds(r0, SUBLANES), :]
        for k in (1, 2, 4):
            if reverse:
                ok = row8 < SUBLANES - k
                sh = SUBLANES - k
            else:
                ok = row8 >= k
                sh = k
            a_prev = jnp.where(ok, pltpu.roll(av, sh, axis=0), 1.0)
            b_prev = jnp.where(ok, pltpu.roll(bv, sh, axis=0), 0.0)
            bv = av * b_prev + bv
            av = av * a_prev
        h = av * hp + bv
        if has_acc:
            out_ref[pl.ds(r0, SUBLANES), :] = h + acc_ref[pl.ds(r0, SUBLANES), :]
        else:
            out_ref[pl.ds(r0, SUBLANES), :] = h
        return h[0:1] if reverse else h[SUBLANES - 1:SUBLANES]

    hp = lax.fori_loop(0, nt, step, hp_s[...])
    hp_s[...] = hp
    hlast_ref[...] = hp


def rglru_scan(u, h0, p, reverse, acc=None, tb=512):
    b, n, c = u.shape
    tb = min(tb, n)
    assert n % tb == 0
    nblk = n // tb
    if reverse:
        seq = pl.BlockSpec((None, tb, c), lambda bi, i: (bi, nblk - 1 - i, 0))
    else:
        seq = pl.BlockSpec((None, tb, c), lambda bi, i: (bi, i, 0))
    state = pl.BlockSpec((None, 1, c), lambda bi, i: (bi, 0, 0))

    def par(a):
        return pl.BlockSpec(a.shape, lambda bi, i: (0,) * a.ndim)

    params = [p["conv_w"], p["conv_b"], p["wa"], p["ba"], p["wx"], p["bx"], p["lam"]]
    in_specs = [seq, state] + [par(a) for a in params]
    args = [u, h0] + params
    if acc is not None:
        in_specs.append(seq)
        args.append(acc)
    return pl.pallas_call(
        functools.partial(_rglru_kernel, reverse=reverse, has_acc=acc is not None),
        grid=(b, nblk),
        in_specs=in_specs,
        out_specs=[seq, state],
        out_shape=[jax.ShapeDtypeStruct((b, n, c), F32), jax.ShapeDtypeStruct((b, 1, c), F32)],
        scratch_shapes=[pltpu.VMEM((tb, c), F32), pltpu.VMEM((tb, c), F32),
                        pltpu.VMEM((1, c), F32), pltpu.VMEM((SUBLANES, c), F32)],
        compiler_params=_cparams(2),
        name="rglru_bwd" if reverse else "rglru_fwd",
    )(*args)


def _block_diag(w):
    g, bs, _ = w.shape
    eye = jnp.eye(g, dtype=w.dtype)
    return (eye[:, None, :, None] * w[:, :, None, :]).reshape(g * bs, g * bs)


def rglru_params(conv_w, conv_b, wa, ba, wx, bx, lam, d):
    c = conv_b.shape[-1]
    return {"conv_w": conv_w[d], "conv_b": conv_b[d].reshape(1, c),
            "wa": _block_diag(wa[d]).astype(BF16), "ba": ba[d].reshape(1, c),
            "wx": _block_diag(wx[d]).astype(BF16), "bx": bx[d].reshape(1, c),
            "lam": lam[d].reshape(1, c)}


def rglru_mix(u_ctx, u_lat, params):
    b, _, c = u_ctx.shape
    zero = jnp.zeros((b, 1, c), F32)
    h_ctx = h_lat = None
    for d in range(2):
        h_ctx, last = rglru_scan(u_ctx, zero, params[d], reverse=bool(d), acc=h_ctx)
        h_lat, _ = rglru_scan(u_lat, last, params[d], reverse=bool(d), acc=h_lat)
    return h_ctx, h_lat


def _gla_kernel(x_ref, s0_ref, wlr_ref, blr_ref, *rest, reverse, has_acc):
    if has_acc:
        acc_ref, o_ref, s_out_ref, st_s = rest
    else:
        o_ref, s_out_ref, st_s = rest
    tb = x_ref.shape[0]
    ch = GLA_CHUNK

    @pl.when(pl.program_id(1) == 0)
    def _():
        st_s[...] = s0_ref[...]

    r_i = lax.broadcasted_iota(jnp.int32, (ch, ch), 0)
    c_i = lax.broadcasted_iota(jnp.int32, (ch, ch), 1)
    if reverse:
        cum_mat = (c_i >= r_i).astype(F32)
        keep = c_i > r_i
    else:
        cum_mat = (c_i <= r_i).astype(F32)
        keep = c_i <= r_i
    lane = lax.broadcasted_iota(jnp.int32, (1, LANES), 1)
    head_lanes = (lane < GLA_DK, lane >= GLA_DK)
    nt_dims = (((1,), (1,)), ((), ()))
    chunks = range(tb // ch)
    for cidx in (reversed(chunks) if reverse else chunks):
        r0 = cidx * ch
        q = x_ref[r0:r0 + ch, 0:GLA_QK] * (GLA_DK ** -0.5)
        k = x_ref[r0:r0 + ch, GLA_QK:2 * GLA_QK]
        v = x_ref[r0:r0 + ch, 2 * GLA_QK:2 * GLA_QK + GLA_V]
        lr = x_ref[r0:r0 + ch, 2 * GLA_QK + GLA_V:GLA_IN_PAD]
        logits = jnp.dot(lr, wlr_ref[...], precision=HIGHEST, preferred_element_type=F32) + blr_ref[...]
        log_a = _log_sigmoid(logits) * (1.0 / GLA_TAU)
        cum = jnp.dot(cum_mat, log_a, precision=HIGHEST, preferred_element_type=F32)
        tot = cum[0:1] if reverse else cum[ch - 1:ch]
        qg = q * jnp.exp(cum)
        kg = k * jnp.exp(-cum)
        kd = k * jnp.exp(tot - cum)
        decay = jnp.exp(tot)
        outs = []
        for h in range(GLA_HEADS):
            sl = slice((h // 2) * LANES, (h // 2 + 1) * LANES)
            mine = head_lanes[h % 2]
            qm = jnp.where(mine, qg[:, sl], 0.0).astype(BF16)
            scores = lax.dot_general(qm, kg[:, sl].astype(BF16), nt_dims, preferred_element_type=F32)
            scores = jnp.where(keep, scores, 0.0)
            vh = v[:, h * GLA_DV:(h + 1) * GLA_DV]
            st = st_s[h]
            o = jnp.dot(scores.astype(BF16), vh.astype(BF16), preferred_element_type=F32)
            o = o + lax.dot_general(qm, st.astype(BF16), nt_dims, preferred_element_type=F32)
            kdm = jnp.where(mine, kd[:, sl], 0.0).astype(BF16)
            st_s[h] = st * decay[:, sl] + jnp.dot(vh.T.astype(BF16), kdm, preferred_element_type=F32)
            outs.append(o)
        o_all = jnp.concatenate(outs, axis=1)
        if has_acc:
            o_all = o_all + acc_ref[r0:r0 + ch, :]
        o_ref[r0:r0 + ch, :] = o_all
    s_out_ref[...] = st_s[...]


def gla_scan(x, s0, wlr, blr, reverse, acc=None, tb=512):
    b, n, cin = x.shape
    tb = min(tb, n)
    assert n % tb == 0 and tb % GLA_CHUNK == 0
    nblk = n // tb

    def seq(width):
        if reverse:
            return pl.BlockSpec((None, tb, width), lambda bi, i: (bi, nblk - 1 - i, 0))
        return pl.BlockSpec((None, tb, width), lambda bi, i: (bi, i, 0))

    state = pl.BlockSpec((None, GLA_HEADS, GLA_DV, LANES), lambda bi, i: (bi, 0, 0, 0))
    in_specs = [seq(cin), state,
                pl.BlockSpec(wlr.shape, lambda bi, i: (0, 0)), pl.BlockSpec(blr.shape, lambda bi, i: (0, 0))]
    args = [x, s0, wlr, blr]
    if acc is not None:
        in_specs.append(seq(GLA_V))
        args.append(acc)
    return pl.pallas_call(
        functools.partial(_gla_kernel, reverse=reverse, has_acc=acc is not None),
        grid=(b, nblk),
        in_specs=in_specs,
        out_specs=[seq(GLA_V), state],
        out_shape=[jax.ShapeDtypeStruct((b, n, GLA_V), F32),
                   jax.ShapeDtypeStruct((b, GLA_HEADS, GLA_DV, LANES), F32)],
        scratch_shapes=[pltpu.VMEM((GLA_HEADS, GLA_DV, LANES), F32)],
        compiler_params=_cparams(2),
        name="gla_bwd" if reverse else "gla_fwd",
    )(*args)


def gla_params(w_lr, b_lr, d):
    w = jnp.zeros((LANES, GLA_QK), F32).at[d * GLA_RANK:(d + 1) * GLA_RANK].set(w_lr[d])
    return w, b_lr[d].reshape(1, GLA_QK)


def gla_mix(x_ctx, x_lat_cols, w_lr, b_lr):
    b = x_ctx.shape[0]
    zero = jnp.zeros((b, GLA_HEADS, GLA_DV, LANES), F32)
    o_ctx = o_lat = None
    for d in range(2):
        w, bias = gla_params(w_lr, b_lr, d)
        o_ctx, s = gla_scan(x_ctx, zero, w, bias, reverse=bool(d), acc=o_ctx)
        o_lat, _ = gla_scan(x_lat_cols, s, w, bias, reverse=bool(d), acc=o_lat)
    return o_ctx, o_lat


def raster_to_column(t):
    b, n = t.shape[:2]
    return t.reshape(b, n // GRID_W, GRID_W, *t.shape[2:]).swapaxes(1, 2).reshape(t.shape)


def column_to_raster(t):
    b, n = t.shape[:2]
    return t.reshape(b, GRID_W, n // GRID_W, *t.shape[2:]).swapaxes(1, 2).reshape(t.shape)


def _merge_kernel(x_ref, gt_ref, mg_ref, yhy_ref, hrg_ref, grg_ref, ogla_ref, ggla_ref,
                  bm_ref, gn_ref, why_ref, wrg_ref, wgla_ref, wout_ref, o_ref):
    d = x_ref.shape[1]
    y_rg = hrg_ref[...] * _gelu_tanh(grg_ref[...])
    o = ogla_ref[...]
    heads = []
    for h in range(GLA_HEADS):
        oh = o[:, h * GLA_DV:(h + 1) * GLA_DV]
        heads.append(oh * lax.rsqrt(jnp.mean(oh * oh, axis=-1, keepdims=True) + NORM_EPS) * gn_ref[...])
    gg = ggla_ref[...]
    y_gla = jnp.concatenate(heads, axis=1) * (gg * _sigmoid(gg))
    gate = _sigmoid(mg_ref[...] + bm_ref[...])

    def proj(y, w_ref):
        return jnp.dot(y.astype(BF16), w_ref[...], preferred_element_type=F32)

    m = (gate[:, 0:d] * proj(yhy_ref[...], why_ref) + gate[:, d:2 * d] * proj(y_rg, wrg_ref)
         + gate[:, 2 * d:3 * d] * proj(y_gla, wgla_ref))
    o_ref[...] = x_ref[...] + gt_ref[...] * proj(m, wout_ref)


def merge_residual(x, gt, mg, y_hy, h_rg, g_rg, o_gla, g_gla, lw, tm=256):
    b, n, d = x.shape
    tm = min(tm, n)
    assert n % tm == 0

    def seq(a):
        return pl.BlockSpec((None, tm, a.shape[2]), lambda bi, i: (bi, i, 0))

    def par(a):
        return pl.BlockSpec(a.shape, lambda bi, i: (0,) * a.ndim)

    streams = [mg, y_hy, h_rg, g_rg, o_gla, g_gla]
    params = [lw["b_merge"], lw["gla_norm_g"], lw["w_hy_o"], lw["w_rg_o"], lw["w_gla_o"], lw["w_out"]]
    return pl.pallas_call(
        _merge_kernel,
        grid=(b, n // tm),
        in_specs=[seq(x), pl.BlockSpec((None, 1, d), lambda bi, i: (bi, 0, 0))]
                 + [seq(a) for a in streams] + [par(a) for a in params],
        out_specs=seq(x),
        out_shape=jax.ShapeDtypeStruct((b, n, d), F32),
        compiler_params=_cparams(2, 40 << 20),
        name="merge_residual",
    )(x, gt, *streams, *params)


def _shortconv_kernel(x_ref, w_ref, b_ref, o_ref, *, rows):
    n, c = x_ref.shape
    zero = jnp.zeros((SUBLANES, c), F32)
    for r0 in range(0, n, rows):
        cur = x_ref[r0:r0 + rows, :]
        before = x_ref[r0 - SUBLANES:r0, :] if r0 > 0 else zero
        after = x_ref[r0 + rows:r0 + rows + SUBLANES, :] if r0 + rows < n else zero
        o_ref[r0:r0 + rows, :] = (b_ref[...] + w_ref[0:1, :] * _shift_rows(cur, before, 1, False)
                                  + w_ref[1:2, :] * cur + w_ref[2:3, :] * _shift_rows(cur, after, 1, True))


def shortconv(x, w, bias):
    b, n, c = x.shape
    blk = pl.BlockSpec((None, n, LANES), lambda bi, j: (bi, 0, j))
    return pl.pallas_call(
        functools.partial(_shortconv_kernel, rows=min(n, 1024)),
        grid=(b, c // LANES),
        in_specs=[blk, pl.BlockSpec((3, LANES), lambda bi, j: (0, j)),
                  pl.BlockSpec((1, LANES), lambda bi, j: (0, j))],
        out_specs=blk,
        out_shape=jax.ShapeDtypeStruct((b, n, c), F32),
        compiler_params=_cparams(2, 40 << 20),
        name="shortconv",
    )(x, w, bias.reshape(1, c))


def _hy_filter_kernel(w1_ref, b1_ref, w2_ref, b2_ref, w3_ref, freq_ref, band_ref, delta_ref,
                      h_ref, asum_ref, *, n):
    tb = h_ref.shape[0]
    i = pl.program_id(0)
    idx = (i * tb + lax.broadcasted_iota(jnp.int32, (tb, LANES), 0)).astype(F32)
    lane = lax.broadcasted_iota(jnp.int32, (tb, LANES), 1)
    tn = idx / (n - 1)
    ang = (2.0 * math.pi / n) * idx * band_ref[...]
    feats = jnp.where(lane == 0, tn,
                      jnp.where(lane <= HY_BANDS, jnp.cos(ang),
                                jnp.where(lane <= 2 * HY_BANDS, -jnp.sin(ang), 0.0)))
    fr = freq_ref[...]

    def dense(x, w_ref):
        return jnp.dot(x, w_ref[...], precision=HIGHEST, preferred_element_type=F32)

    h = jnp.sin(fr * (dense(feats, w1_ref) + b1_ref[...]))
    h = jnp.sin(fr * (dense(h, w2_ref) + b2_ref[...]))
    h = dense(h, w3_ref) * jnp.exp(-tn[:, 0:1] * delta_ref[...])
    h_ref[...] = h

    @pl.when(i == 0)
    def _():
        asum_ref[...] = jnp.zeros_like(asum_ref)

    asum_ref[...] += jnp.sum(jnp.abs(h), axis=0, keepdims=True)


def hyena_filter(n, w1, b1, w2, b2, w3, freq, tb=256):
    nout = w3.shape[1]
    tb = min(tb, n)
    bands = np.zeros((1, LANES), np.float32)
    lin = np.linspace(1e-4, HY_BANDS - 1, HY_BANDS, dtype=np.float32)
    bands[0, 1:1 + HY_BANDS] = lin
    bands[0, 1 + HY_BANDS:1 + 2 * HY_BANDS] = lin
    deltas = np.abs(np.linspace(HY_FAST_RATE, HY_SLOW_RATE, HY_W, dtype=np.float32))
    deltas = np.tile(deltas, nout // HY_W).reshape(1, nout)
    w1p = jnp.zeros((LANES, HY_FFN), F32).at[:w1.shape[0]].set(w1)
    params = [w1p, b1.reshape(1, -1), w2, b2.reshape(1, -1), w3, freq.reshape(1, -1),
              jnp.asarray(bands), jnp.asarray(deltas)]
    return pl.pallas_call(
        functools.partial(_hy_filter_kernel, n=n),
        grid=(n // tb,),
        in_specs=[pl.BlockSpec(a.shape, lambda i: (0, 0)) for a in params],
        out_specs=[pl.BlockSpec((tb, nout), lambda i: (i, 0)), pl.BlockSpec((1, nout), lambda i: (0, 0))],
        out_shape=[jax.ShapeDtypeStruct((n, nout), F32), jax.ShapeDtypeStruct((1, nout), F32)],
        compiler_params=_cparams(1),
        name="hyena_filter",
    )(*params)


def _filter_taps(hraw, n):
    h = hraw.reshape(n, 2, 2, HY_W)
    h_fwd = jnp.moveaxis(h[:, :, 0], 1, 0)
    h_bwd = jnp.moveaxis(h[:, :, 1], 1, 0)
    zeros = jnp.zeros_like(h_fwd)
    taps_f = jnp.concatenate([h_fwd, zeros], axis=1)
    taps_b = jnp.concatenate([h_bwd[:, :1], zeros, h_bwd[:, :0:-1]], axis=1)
    return taps_f, taps_b


def _cis(num, den):
    ang = (2.0 * math.pi / den) * (num % den).astype(F32)
    return jnp.cos(ang), jnp.sin(ang)


def _dft_tables(n):
    big = 2 * n
    q = int(round(math.sqrt(big)))
    assert q * q == big and q % (2 * SUBLANES) == 0
    ar = jnp.arange(q, dtype=jnp.int32)
    num = ar[None, :, None] * (q * ar[None, None, :] + ar[:, None, None])
    c, s = _cis(num, big)
    w1 = jnp.concatenate([c, -s], axis=1)
    ct, st = jnp.swapaxes(c, 1, 2)[:, :q // 2], jnp.swapaxes(s, 1, 2)[:, :q // 2]
    v = jnp.concatenate([ct, -st], axis=2) * (1.0 / big)
    c2, s2 = _cis(ar[:, None] * ar[None, :], q)
    f2 = jnp.concatenate([jnp.concatenate([c2, s2], axis=1), jnp.concatenate([-s2, c2], axis=1)], axis=0)
    g2 = jnp.concatenate([jnp.concatenate([c2, -s2], axis=1), jnp.concatenate([s2, c2], axis=1)], axis=0)
    return {"q": q, "w1": w1, "v": v, "f2": f2, "g2": g2}


def _level_kernel(w_ref, x_ref, *rest, n_add, has_gate, precise):
    rest = list(rest)
    x2_ref = rest.pop(0) if n_add else None
    if has_gate:
        src_ref, gate_ref, skip_ref = rest[:3]
        rest = rest[3:]
    o_ref = rest[0]
    for s in range(w_ref.shape[0]):
        x = x_ref[s]
        if n_add:
            x = x + x2_ref[s]
        if precise:
            y = jnp.dot(w_ref[s], x, precision=HIGHEST, preferred_element_type=F32)
        else:
            y = jnp.dot(w_ref[s], x.astype(BF16), preferred_element_type=F32)
        if has_gate:
            y = gate_ref[s] * (y + src_ref[s] * skip_ref[...])
        o_ref[s] = y.astype(o_ref.dtype)


def dft_level(w, x, x_group=0, add=None, gate=None, out_dtype=F32, precise=False, sb=8):
    b, q, k, _ = x.shape
    m = w.shape[1]
    c = HY_W
    sb = min(sb, q)

    def seq(rows, group):
        return pl.BlockSpec((None, sb, rows, c), lambda i, bi: (bi, i, 0, group))

    in_specs = [pl.BlockSpec((sb, m, k), lambda i, bi: (i, 0, 0)), seq(k, x_group)]
    args = [w, x]
    if add is not None:
        in_specs.append(seq(k, 0))
        args.append(add)
    if gate is not None:
        src, src_group, gates, gate_group, skip = gate
        in_specs += [seq(m, src_group), seq(m, gate_group), pl.BlockSpec((1, c), lambda i, bi: (0, 0))]
        args += [src, gates, skip]
    return pl.pallas_call(
        functools.partial(_level_kernel, n_add=add is not None, has_gate=gate is not None,
                          precise=precise),
        grid=(q // sb, b),
        in_specs=in_specs,
        out_specs=seq(m, 0),
        out_shape=jax.ShapeDtypeStruct((b, q, m, c), out_dtype),
        compiler_params=_cparams(2, 40 << 20),
        name="dft_level",
    )(*args)


def _dft_mid_kernel(a_ref, h_ref, f_ref, g_ref, o_ref):
    q = a_ref.shape[1] // 2
    for j in range(a_ref.shape[0]):
        x = jnp.dot(f_ref[...], a_ref[j], preferred_element_type=F32)
        xr, xi = x[:q], x[q:]
        hr, hi = h_ref[j, :q, :], h_ref[j, q:, :]
        y = jnp.concatenate([xr * hr - xi * hi, xr * hi + xi * hr], axis=0)
        o_ref[j] = jnp.dot(g_ref[...], y.astype(BF16), preferred_element_type=F32).astype(o_ref.dtype)


def dft_mid(a, spec, f2, g2, kb=8):
    b, q, q2, c = a.shape
    blk = pl.BlockSpec((None, kb, q2, c), lambda i, bi: (bi, i, 0, 0))
    mat = pl.BlockSpec((q2, q2), lambda i, bi: (0, 0))
    return pl.pallas_call(
        _dft_mid_kernel,
        grid=(q // kb, b),
        in_specs=[blk, pl.BlockSpec((kb, q2, c), lambda i, bi: (i, 0, 0)), mat, mat],
        out_specs=blk,
        out_shape=jax.ShapeDtypeStruct(a.shape, BF16),
        compiler_params=_cparams(2, 40 << 20),
        name="dft_mid",
    )(a, spec, f2, g2)


def _dft_spec_kernel(a_ref, f_ref, asum_ref, o_ref):
    inv = 1.0 / (asum_ref[0:1, :] + asum_ref[1:2, :] + 1e-6)
    for j in range(a_ref.shape[0]):
        o_ref[j] = jnp.dot(f_ref[...], a_ref[j], precision=HIGHEST, preferred_element_type=F32) * inv


def dft_spec(a, f2, asum, kb=8):
    o, q, q2, c = a.shape
    blk = pl.BlockSpec((None, kb, q2, c), lambda i, oi: (oi, i, 0, 0))
    return pl.pallas_call(
        _dft_spec_kernel,
        grid=(q // kb, o),
        in_specs=[blk, pl.BlockSpec((q2, q2), lambda i, oi: (0, 0)),
                  pl.BlockSpec((None, 2, c), lambda i, oi: (oi, 0, 0))],
        out_specs=blk,
        out_shape=jax.ShapeDtypeStruct(a.shape, F32),
        compiler_params=_cparams(2, 40 << 20),
        name="dft_spec",
    )(a, f2, asum)


def _swap_levels(a):
    b, q, q2, c = a.shape
    return a.reshape(b, q, 2, q, c).transpose(0, 3, 2, 1, 4).reshape(b, q, q2, c)


def hyena_long(u, hraw, asum, skip, tabs):
    b, n, _ = u.shape
    q = tabs["q"]
    c = HY_W
    taps_f, taps_b = _filter_taps(hraw, n)

    def to_levels(t, rows):
        return t.reshape(t.shape[0], rows, q, t.shape[2]).swapaxes(1, 2)

    a = dft_level(tabs["w1"], to_levels(taps_f, q), add=to_levels(taps_b, q), precise=True)
    spec = dft_spec(_swap_levels(a), tabs["f2"], asum.reshape(2, 2, c))
    w1d = tabs["w1"][:, :, :q // 2].astype(BF16)
    vd = tabs["v"].astype(BF16)
    f2, g2 = tabs["f2"].astype(BF16), tabs["g2"].astype(BF16)
    u_t = to_levels(u, q // 2)
    src, group = u_t, 0
    for order in range(2):
        a = dft_level(w1d, src, x_group=group, out_dtype=BF16)
        cm = dft_mid(_swap_levels(a), spec[order], f2, g2)
        src = dft_level(vd, _swap_levels(cm),
                        gate=(src, group, u_t, order + 1, skip[order].reshape(1, c)))
        group = 0
    return src.swapaxes(1, 2).reshape(b, n, c)


def _dft_small_kernel(u_ref, gate_ref, skip_ref, h_ref, f_ref, g_ref, o_ref):
    u = u_ref[...]
    nb = h_ref.shape[0] // 2
    x = jnp.dot(f_ref[...], u, precision=HIGHEST, preferred_element_type=F32)
    xr, xi = x[:nb], x[nb:]
    hr, hi = h_ref[:nb, :], h_ref[nb:, :]
    y = jnp.concatenate([xr * hr - xi * hi, xr * hi + xi * hr], axis=0)
    conv = jnp.dot(g_ref[...], y, precision=HIGHEST, preferred_element_type=F32)
    o_ref[...] = gate_ref[...] * (conv + u * skip_ref[...])


def _dft_small_spec_kernel(tf_ref, tb_ref, f_ref, asum_ref, o_ref):
    inv = 1.0 / (asum_ref[0:1, :] + asum_ref[1:2, :] + 1e-6)
    o_ref[...] = jnp.dot(f_ref[...], tf_ref[...] + tb_ref[...], precision=HIGHEST,
                         preferred_element_type=F32) * inv


def hyena_short_seq(u, hraw, asum, skip):
    b, n, _ = u.shape
    big = 2 * n
    c = HY_W
    ar = jnp.arange(big, dtype=jnp.int32)
    cs, sn = _cis(ar[:, None] * ar[None, :], big)
    f_full = jnp.concatenate([cs, -sn], axis=0)
    g_half = jnp.concatenate([cs[:n], -sn[:n]], axis=1) * (1.0 / big)
    taps_f, taps_b = _filter_taps(hraw, n)
    tap = pl.BlockSpec((None, big, c), lambda o: (o, 0, 0))
    spec = pl.pallas_call(
        _dft_small_spec_kernel,
        grid=(2,),
        in_specs=[tap, tap, pl.BlockSpec((2 * big, big), lambda o: (0, 0)),
                  pl.BlockSpec((None, 2, c), lambda o: (o, 0, 0))],
        out_specs=pl.BlockSpec((None, 2 * big, c), lambda o: (o, 0, 0)),
        out_shape=jax.ShapeDtypeStruct((2, 2 * big, c), F32),
        compiler_params=_cparams(1),
        name="dft_small_spec",
    )(taps_f, taps_b, f_full, asum.reshape(2, 2, c))
    f_data = f_full[:, :n]
    src, group = u, 0
    for order in range(2):
        src = pl.pallas_call(
            _dft_small_kernel,
            grid=(b,),
            in_specs=[pl.BlockSpec((None, n, c), functools.partial(lambda bi, g: (bi, 0, g), g=group)),
                      pl.BlockSpec((None, n, c), functools.partial(lambda bi, g: (bi, 0, g), g=order + 1)),
                      pl.BlockSpec((1, c), lambda bi: (0, 0)),
                      pl.BlockSpec((2 * big, c), lambda bi: (0, 0)),
                      pl.BlockSpec((2 * big, n), lambda bi: (0, 0)),
                      pl.BlockSpec((n, 2 * big), lambda bi: (0, 0))],
            out_specs=pl.BlockSpec((None, n, c), lambda bi: (bi, 0, 0)),
            out_shape=jax.ShapeDtypeStruct((b, n, c), F32),
            compiler_params=_cparams(1),
            name="dft_small",
        )(src, u, skip[order].reshape(1, c), spec[order], f_data, g_half)
        group = 0
    return src


HY_IN = 3 * HY_W
IN_GROUPS = (HY_IN, RG_W, RG_W, GLA_IN, GLA_V, 3 * D_MODEL)


def _split_w_in(w):
    parts, start = [], 0
    for width in IN_GROUPS:
        parts.append(w[:, start:start + width])
        start += width
    parts[3] = jnp.pad(parts[3], ((0, 0), (0, GLA_IN_PAD - GLA_IN)))
    return [p.astype(BF16) for p in parts]


def _project(h, w):
    b, n, d = h.shape
    return matmul(h.reshape(b * n, d), w).reshape(b, n, w.shape[1])


def kernel(x, c, ctx, c_ctx, w_mod, b_mod, g_norm_mix, g_norm_ffn, w_in, hy_conv_w, hy_conv_b,
           hy_w1, hy_b1, hy_w2, hy_b2, hy_w3, hy_freq, hy_skip, rg_conv_w, rg_conv_b, rg_wa, rg_ba,
           rg_wx, rg_bx, rg_lambda, gla_w_lr, gla_b_lr, gla_norm_g, w_hy_o, w_rg_o, w_gla_o, b_merge,
           w_out, peer_wq, peer_keys, peer_u, peer_v, g_final):
    b, n, d = x.shape
    n_ctx = ctx.shape[1]
    depth = w_mod.shape[0]
    cond = jnp.concatenate([c, c_ctx[None, :]], axis=0)
    cond = jnp.pad(cond, ((0, -(b + 1) % SUBLANES), (0, 0)))
    tabs = _dft_tables(n)
    x_lat, x_ctx = x, ctx
    for l in range(depth):
        need_ctx = l < depth - 1
        mod = matmul(cond, w_mod[l].astype(BF16), bias=b_mod[l], silu_in=True)
        sh1, sc1, gt1, sh2, sc2, gt2 = [mod[:b, i * d:(i + 1) * d].reshape(b, 1, d) for i in range(6)]
        csh1, csc1, cgt1, csh2, csc2, cgt2 = [
            jnp.broadcast_to(mod[b:b + 1, i * d:(i + 1) * d].reshape(1, 1, d), (b, 1, d)) for i in range(6)]
        w_groups = _split_w_in(w_in[l])
        lw = {"b_merge": b_merge[l].reshape(1, 3 * d), "gla_norm_g": gla_norm_g[l].reshape(1, GLA_DV),
              "w_hy_o": w_hy_o[l].astype(BF16), "w_rg_o": w_rg_o[l].astype(BF16),
              "w_gla_o": w_gla_o[l].astype(BF16), "w_out": w_out[l].astype(BF16)}
        rg_par = [rglru_params(rg_conv_w[l], rg_conv_b[l], rg_wa[l], rg_ba[l], rg_wx[l], rg_bx[l],
                               rg_lambda[l], dd) for dd in range(2)]
        filt = (hy_w1[l], hy_b1[l], hy_w2[l], hy_b2[l], hy_w3[l], hy_freq[l])
        u_pack, v_pack = _pack_table(peer_u[l]), _pack_table(peer_v[l])

        h_lat = normmod(x_lat, g_norm_mix[l], sh1, sc1, BF16)
        h_ctx = normmod(x_ctx, g_norm_mix[l], csh1, csc1, BF16)
        hy_l, rgx_l, rgg_l, gla_l, glag_l, mg_l = [_project(h_lat, w) for w in w_groups]
        ctx_groups = range(6) if need_ctx else (1, 3)
        ctx_proj = {i: _project(h_ctx, w_groups[i]) for i in ctx_groups}

        hraw, asum = hyena_filter(n, *filt)
        y_hy_l = hyena_long(shortconv(hy_l, hy_conv_w[l], hy_conv_b[l]), hraw, asum, hy_skip[l], tabs)
        h_rg_c, h_rg_l = rglru_mix(ctx_proj[1], rgx_l, rg_par)
        o_gla_c, o_gla_l = gla_mix(ctx_proj[3], raster_to_column(gla_l), gla_w_lr[l], gla_b_lr[l])
        x_lat = merge_residual(x_lat, gt1, mg_l, y_hy_l, h_rg_l, rgg_l, column_to_raster(o_gla_l),
                               glag_l, lw)
        h2 = normmod(x_lat, g_norm_ffn[l], sh2, sc2, F32)
        x_lat = peer_residual(x_lat, gt2, h2, peer_wq[l], peer_keys[l], u_pack, v_pack)
        if need_ctx:
            hraw_c, asum_c = hyena_filter(n_ctx, *filt)
            y_hy_c = hyena_short_seq(shortconv(ctx_proj[0], hy_conv_w[l], hy_conv_b[l]), hraw_c, asum_c,
                                     hy_skip[l])
            x_ctx = merge_residual(x_ctx, cgt1, ctx_proj[5], y_hy_c, h_rg_c, ctx_proj[2], o_gla_c,
                                   ctx_proj[4], lw)
            h2c = normmod(x_ctx, g_norm_ffn[l], csh2, csc2, F32)
            x_ctx = peer_residual(x_ctx, cgt2, h2c, peer_wq[l], peer_keys[l], u_pack, v_pack)
    zero = jnp.zeros((b, 1, d), F32)
    return normmod(x_lat, g_final, zero, zero, F32, mod=False)
```

```python
import functools
import math

import numpy as np
import jax
import jax.numpy as jnp
from jax import lax
from jax.experimental import pallas as pl
from jax.experimental.pallas import tpu as pltpu

F32 = jnp.float32
BF16 = jnp.bfloat16
HIGHEST = lax.Precision.HIGHEST

D_MODEL = 1024
DEPTH = 4
GRID_W = 64
NORM_EPS = 1e-6

HY_W = 512
HY_BANDS = 16
HY_FFN = 64
HY_FAST_RATE = math.log(1e-2) / 0.3
HY_SLOW_RATE = math.log(1e-2) / 1.5

RG_W = 512
RG_BLOCKS = 8
RG_C = 8.0

GLA_HEADS = 4
GLA_DK = 64
GLA_DV = 128
GLA_QK = GLA_HEADS * GLA_DK
GLA_V = GLA_HEADS * GLA_DV
GLA_RANK = 16
GLA_TAU = 16.0
GLA_CHUNK = 64
GLA_IN = 2 * GLA_QK + GLA_V + 2 * GLA_RANK
GLA_IN_PAD = 2 * GLA_QK + GLA_V + 128

PEER_HEADS = 8
PEER_NKEYS = 128
PEER_TOPK = 16
PEER_HALF = 128
PEER_SEL = PEER_HEADS * PEER_TOPK

V7X_VMEM_BYTES = 64 * 1024 * 1024
SUBLANES = 8
LANES = 128


def _cparams(n_grid, vmem_bytes=None):
    kw = dict(dimension_semantics=("arbitrary",) * n_grid)
    if vmem_bytes is not None:
        assert vmem_bytes < V7X_VMEM_BYTES
        kw["vmem_limit_bytes"] = int(vmem_bytes)
    return pltpu.CompilerParams(**kw)


def _gelu_tanh(x):
    return 0.5 * x * (1.0 + jnp.tanh(math.sqrt(2.0 / math.pi) * (x + 0.044715 * (x * x * x))))


def _sigmoid(x):
    return 1.0 / (1.0 + jnp.exp(-x))


def _log_sigmoid(x):
    return jnp.minimum(x, 0.0) - jnp.log(1.0 + jnp.exp(-jnp.abs(x)))


def _mm_kernel(x_ref, w_ref, *rest, silu_in, has_bias):
    o_ref = rest[-1]
    x = x_ref[...]
    if silu_in:
        x = x * _sigmoid(x)
    acc = jnp.dot(x.astype(BF16), w_ref[...], preferred_element_type=F32)
    if has_bias:
        acc = acc + rest[0][...]
    o_ref[...] = acc.astype(o_ref.dtype)


def matmul(x, w, bias=None, silu_in=False, tm=512):
    m, k = x.shape
    n = w.shape[1]
    tm = min(tm, m)
    assert m % tm == 0
    in_specs = [pl.BlockSpec((tm, k), lambda i: (i, 0)), pl.BlockSpec((k, n), lambda i: (0, 0))]
    args = [x, w]
    if bias is not None:
        in_specs.append(pl.BlockSpec((1, n), lambda i: (0, 0)))
        args.append(bias.reshape(1, n))
    est = 2 * (tm * k * x.dtype.itemsize + k * n * 2 + tm * n * 4) + (4 << 20)
    return pl.pallas_call(
        functools.partial(_mm_kernel, silu_in=silu_in, has_bias=bias is not None),
        grid=(m // tm,),
        in_specs=in_specs,
        out_specs=pl.BlockSpec((tm, n), lambda i: (i, 0)),
        out_shape=jax.ShapeDtypeStruct((m, n), F32),
        compiler_params=_cparams(1, est),
        name="matmul",
    )(*args)


def _normmod_kernel(x_ref, g_ref, sh_ref, sc_ref, o_ref, *, mod):
    x = x_ref[...]
    y = x * lax.rsqrt(jnp.mean(x * x, axis=-1, keepdims=True) + NORM_EPS) * g_ref[...]
    if mod:
        y = y * (1.0 + sc_ref[...]) + sh_ref[...]
    o_ref[...] = y.astype(o_ref.dtype)


def normmod(x, g, shift, scale, out_dtype, mod=True, tm=512):
    b, l, d = x.shape
    tm = min(tm, l)
    assert l % tm == 0
    vec = pl.BlockSpec((None, 1, d), lambda bi, i: (bi, 0, 0))
    return pl.pallas_call(
        functools.partial(_normmod_kernel, mod=mod),
        grid=(b, l // tm),
        in_specs=[pl.BlockSpec((None, tm, d), lambda bi, i: (bi, i, 0)),
                  pl.BlockSpec((1, d), lambda bi, i: (0, 0)), vec, vec],
        out_specs=pl.BlockSpec((None, tm, d), lambda bi, i: (bi, i, 0)),
        out_shape=jax.ShapeDtypeStruct((b, l, d), out_dtype),
        compiler_params=_cparams(2),
        name="normmod",
    )(x, g.reshape(1, d), shift, scale)


def _assemble_rows(rows, n):
    t = rows[0].shape[1]
    rid = lax.broadcasted_iota(jnp.int32, (n, t), 0)
    out = jnp.zeros((n, t), rows[0].dtype)
    for r in range(n):
        out = jnp.where(rid == r, rows[r], out)
    return out


def _extract_topk(s, rowid, k):
    vals, ids = [], []
    for _ in range(k):
        m = jnp.max(s, axis=0, keepdims=True)
        first = jnp.min(jnp.where(s == m, rowid, 1e9), axis=0, keepdims=True)
        s = jnp.where(rowid == first, -jnp.inf, s)
        vals.append(m)
        ids.append(first)
    return vals, ids


def _split_bf16(x):
    hi = x.astype(BF16)
    return hi, (x - hi.astype(F32)).astype(BF16)


def _dot3(a_hi, a_lo, b_hi, b_lo, dims):
    def one(a, b):
        return lax.dot_general(a, b, dims, preferred_element_type=F32)
    return one(a_hi, b_hi) + (one(a_hi, b_lo) + one(a_lo, b_hi))


def _peer_select_kernel(h_ref, wqh_ref, wql_ref, kh_ref, kl_ref, exp_ref, wgt_ref, q_ref):
    h_hi, h_lo = _split_bf16(h_ref[...])
    q_ref[...] = _dot3(h_hi, h_lo, wqh_ref[...], wql_ref[...], (((1,), (0,)), ((), ())))

    def lane_block(j, carry):
        r0 = pl.multiple_of(j * LANES, LANES)
        rows, weights = _select_tokens(q_ref[pl.ds(r0, LANES), :], kh_ref, kl_ref)
        exp_ref[pl.ds(r0, LANES), :] = rows
        wgt_ref[pl.ds(r0, LANES), :] = weights
        return carry

    lax.fori_loop(0, h_ref.shape[0] // LANES, lane_block, 0)


def _select_tokens(q, kh_ref, kl_ref):
    tt = q.shape[0]
    k = PEER_TOPK
    key_id = lax.broadcasted_iota(jnp.int32, (PEER_NKEYS, tt), 0).astype(F32)
    row8 = lax.broadcasted_iota(jnp.int32, (SUBLANES, tt), 0)
    row16 = lax.broadcasted_iota(jnp.int32, (2 * SUBLANES, tt), 0)
    exp_blocks, wgt_blocks = [], []
    for h in range(PEER_HEADS):
        tops = []
        for p in range(2):
            col = (h * 2 + p) * PEER_HALF
            q_hi, q_lo = _split_bf16(q[:, col:col + PEER_HALF])
            s = _dot3(kh_ref[h, p], kl_ref[h, p], q_hi, q_lo, (((1,), (1,)), ((), ())))
            vals, ids = _extract_topk(s, key_id, k)
            tops.append((_assemble_rows(vals, k), _assemble_rows(ids, k)))
        (a, ia), (b, ib) = tops
        blocks, flat = [a[0:1] + b], [row16.astype(F32)]
        for i in range(1, 8):
            nj = k // (i + 1)
            blocks.append(jnp.where(row8 < nj, a[i:i + 1] + b[0:8], -jnp.inf))
            flat.append((row8 + i * k).astype(F32))
        blocks.append(a[8:16] + b[0:1])
        flat.append(((row8 + 8) * k).astype(F32))
        cand = jnp.concatenate(blocks, axis=0)
        cand_id = jnp.concatenate(flat, axis=0)
        vals, ids = _extract_topk(cand, cand_id, k)
        best = _assemble_rows(vals, k)
        fl = _assemble_rows(ids, k)
        fi = jnp.floor(fl * (1.0 / k))
        fj = fl - fi * k
        ei = jnp.zeros_like(fl)
        ej = jnp.zeros_like(fl)
        for r in range(k):
            ei = jnp.where(fi == r, ia[r:r + 1], ei)
            ej = jnp.where(fj == r, ib[r:r + 1], ej)
        e = jnp.exp(best - jnp.max(best, axis=0, keepdims=True))
        wgt_blocks.append(e / jnp.sum(e, axis=0, keepdims=True))
        exp_blocks.append((ei * PEER_NKEYS + ej) * ROW_WORDS)
    return (jnp.concatenate(exp_blocks, axis=0).T.astype(jnp.int32),
            jnp.concatenate(wgt_blocks, axis=0).T)


def peer_select(h, wq, keys, tt=512):
    t, d = h.shape
    tt = min(tt, t)
    assert t % tt == 0 and tt % LANES == 0
    nq = wq.shape[1]
    wq_hi, wq_lo = _split_bf16(wq)
    k_hi, k_lo = _split_bf16(keys)
    wspec = pl.BlockSpec((d, nq), lambda i: (0, 0))
    kspec = pl.BlockSpec(keys.shape, lambda i: (0, 0, 0, 0))
    return pl.pallas_call(
        _peer_select_kernel,
        grid=(t // tt,),
        in_specs=[pl.BlockSpec((tt, d), lambda i: (i, 0)), wspec, wspec, kspec, kspec],
        out_specs=[pl.BlockSpec((tt, PEER_SEL), lambda i: (i, 0)),
                   pl.BlockSpec((tt, PEER_SEL), lambda i: (i, 0))],
        out_shape=[jax.ShapeDtypeStruct((t, PEER_SEL), jnp.int32),
                   jax.ShapeDtypeStruct((t, PEER_SEL), F32)],
        scratch_shapes=[pltpu.VMEM((tt, nq), F32)],
        compiler_params=_cparams(1, 2 * (2 * d * nq + 2 * keys.size) * 2 + 3 * tt * (d + nq) * 4 + (8 << 20)),
        name="peer_select",
    )(h, wq_hi, wq_lo, k_hi, k_lo)


ROW_WORDS = D_MODEL // 2 // LANES


def _unpack_pair(w):
    lo = lax.bitcast_convert_type(lax.shift_left(w, jnp.int32(16)), F32)
    hi = lax.bitcast_convert_type(jnp.bitwise_and(w, jnp.int32(-65536)), F32)
    return lo, hi


PEER_CHUNK = 128
PEER_NCHUNK = PEER_SEL // PEER_CHUNK
PEER_GROUP = 8


def _rows_to_tiles(x8):
    row8 = lax.broadcasted_iota(jnp.int32, (SUBLANES, LANES), 0)
    tiles = []
    for tk in range(SUBLANES):
        tile = jnp.zeros((SUBLANES, LANES), x8.dtype)
        for r in range(SUBLANES):
            tile = jnp.where(row8 == r, x8[tk:tk + 1, r * LANES:(r + 1) * LANES], tile)
        tiles.append(tile)
    return tiles


def _tiles_to_rows(tiles):
    row8 = lax.broadcasted_iota(jnp.int32, (SUBLANES, LANES), 0)
    cols = []
    for r in range(SUBLANES):
        col = jnp.zeros((SUBLANES, LANES), tiles[0].dtype)
        for tk in range(SUBLANES):
            col = jnp.where(row8 == tk, tiles[tk][r:r + 1, :], col)
        cols.append(col)
    return jnp.concatenate(cols, axis=1)


def _peer_act_kernel(idx_ref, x_ref, wgt_ref, tab_ref, coef_ref, p_ref, xs_ref):
    tt = x_ref.shape[0]

    def group(g, carry):
        t0 = pl.multiple_of(g * PEER_GROUP, PEER_GROUP)
        for tk, tile in enumerate(_rows_to_tiles(x_ref[pl.ds(t0, PEER_GROUP), :])):
            xs_ref[tk] = tile

        def token(tk, c):
            xl = xs_ref[tk, 0:ROW_WORDS, :]
            xh = xs_ref[tk, ROW_WORDS:2 * ROW_WORDS, :]

            def chunk(ci, c2):
                base = ((t0 + tk) * PEER_NCHUNK + ci) * PEER_CHUNK
                for j in range(PEER_CHUNK):
                    r = pl.multiple_of(idx_ref[base + j], ROW_WORDS)
                    lo, hi = _unpack_pair(tab_ref[pl.ds(r, ROW_WORDS), :])
                    p_ref[tk * PEER_NCHUNK + ci, j * ROW_WORDS:(j + 1) * ROW_WORDS, :] = lo * xl + hi * xh
                return c2

            return lax.fori_loop(0, PEER_NCHUNK, chunk, c)

        lax.fori_loop(0, PEER_GROUP, token, 0)
        for tk in range(PEER_GROUP):
            tok = p_ref.at[tk * PEER_NCHUNK:(tk + 1) * PEER_NCHUNK]
            parts = [tok[:, pl.ds(r, PEER_CHUNK, stride=ROW_WORDS), :].reshape(PEER_SEL, LANES)
                     for r in range(ROW_WORDS)]
            per_lane = (parts[0] + parts[1]) + (parts[2] + parts[3])
            act = jnp.sum(per_lane.T, axis=0, keepdims=True)
            coef_ref[t0 + tk] = wgt_ref[t0 + tk] * _gelu_tanh(act)
        return carry

    lax.fori_loop(0, tt // PEER_GROUP, group, 0)


def _peer_out_kernel(idx_ref, coef_ref, res_ref, gate_ref, tab_ref, out_ref, ys_ref):
    tt = out_ref.shape[0]
    zero = jnp.zeros((ROW_WORDS, LANES), F32)

    def group(g, carry):
        t0 = pl.multiple_of(g * PEER_GROUP, PEER_GROUP)

        def token(tk, c):
            def chunk(ci, acc):
                acc = list(acc)
                base = ((t0 + tk) * PEER_NCHUNK + ci) * PEER_CHUNK
                for j in range(PEER_CHUNK):
                    r = pl.multiple_of(idx_ref[base + j], ROW_WORDS)
                    lo, hi = _unpack_pair(tab_ref[pl.ds(r, ROW_WORDS), :])
                    cf = coef_ref[base + j]
                    acc[2 * (j % 2)] = acc[2 * (j % 2)] + cf * lo
                    acc[2 * (j % 2) + 1] = acc[2 * (j % 2) + 1] + cf * hi
                return tuple(acc)

            acc = lax.fori_loop(0, PEER_NCHUNK, chunk, (zero, zero, zero, zero))
            ys_ref[tk, 0:ROW_WORDS, :] = acc[0] + acc[2]
            ys_ref[tk, ROW_WORDS:2 * ROW_WORDS, :] = acc[1] + acc[3]
            return c

        lax.fori_loop(0, PEER_GROUP, token, 0)
        rows = _tiles_to_rows([ys_ref[tk] for tk in range(PEER_GROUP)])
        out_ref[pl.ds(t0, PEER_GROUP), :] = res_ref[pl.ds(t0, PEER_GROUP), :] + gate_ref[...] * rows
        return carry

    lax.fori_loop(0, tt // PEER_GROUP, group, 0)


def _pack_table(tab):
    e, d = tab.shape
    pairs = jnp.moveaxis(tab.astype(BF16).reshape(e, 2, d // 2), 1, 2)
    return lax.bitcast_convert_type(pairs, jnp.int32).reshape(e * ROW_WORDS, LANES)


def _table_spec(shape):
    return pl.BlockSpec(shape, lambda i: (0, 0), pipeline_mode=pl.Buffered(1))


def peer_experts(h, expert, weight, u_pack, v_pack, res, gate, tokens_per_gate, tt=128):
    t, d = h.shape
    assert t % tt == 0 and tokens_per_gate % tt == 0 and tt % PEER_GROUP == 0
    assert PEER_GROUP == SUBLANES and d == 2 * ROW_WORDS * LANES
    vmem = u_pack.size * 4 + (16 << 20)
    smem_flat = pl.BlockSpec((tt * PEER_SEL,), lambda i: (i,), memory_space=pltpu.SMEM)
    xspec = pl.BlockSpec((tt, d), lambda i: (i, 0))
    rowspec = pl.BlockSpec((tt, 1, PEER_SEL), lambda i: (i, 0, 0))
    tile_scratch = pltpu.VMEM((PEER_GROUP, SUBLANES, LANES), F32)
    rows = expert.reshape(t * PEER_SEL)
    coef = pl.pallas_call(
        _peer_act_kernel,
        grid=(t // tt,),
        in_specs=[smem_flat, xspec, rowspec, _table_spec(u_pack.shape)],
        out_specs=rowspec,
        out_shape=jax.ShapeDtypeStruct((t, 1, PEER_SEL), F32),
        scratch_shapes=[pltpu.VMEM((PEER_GROUP * PEER_NCHUNK, PEER_CHUNK * ROW_WORDS, LANES), F32),
                        tile_scratch],
        compiler_params=_cparams(1, vmem),
        name="peer_act",
    )(rows, h, weight.reshape(t, 1, PEER_SEL), u_pack)
    return pl.pallas_call(
        _peer_out_kernel,
        grid=(t // tt,),
        in_specs=[smem_flat, smem_flat, xspec,
                  pl.BlockSpec((None, 1, d), lambda i: (i * tt // tokens_per_gate, 0, 0)),
                  _table_spec(v_pack.shape)],
        out_specs=xspec,
        out_shape=jax.ShapeDtypeStruct((t, d), F32),
        scratch_shapes=[tile_scratch],
        compiler_params=_cparams(1, vmem),
        name="peer_out",
    )(rows, coef.reshape(t * PEER_SEL), res, gate, v_pack)


def peer_residual(x, gate, h, wq, keys, u_pack, v_pack):
    b, n, d = h.shape
    hf = h.reshape(b * n, d)
    expert, weight = peer_select(hf, wq, keys)
    out = peer_experts(hf, expert, weight, u_pack, v_pack, x.reshape(b * n, d), gate, n)
    return out.reshape(b, n, d)


RG_CONV = 4


def _shift_rows(cur, halo, k, reverse):
    tb = cur.shape[0]
    row8 = lax.broadcasted_iota(jnp.int32, (SUBLANES, cur.shape[1]), 0)
    if not reverse:
        rolled = pltpu.roll(cur, k, axis=0)
        first = jnp.where(row8 < k, pltpu.roll(halo, k, axis=0), rolled[0:SUBLANES])
        return jnp.concatenate([first, rolled[SUBLANES:]], axis=0)
    rolled = pltpu.roll(cur, tb - k, axis=0)
    last = jnp.where(row8 >= SUBLANES - k, pltpu.roll(halo, SUBLANES - k, axis=0),
                     rolled[tb - SUBLANES:])
    return jnp.concatenate([rolled[:tb - SUBLANES], last], axis=0)


def _rglru_kernel(u_ref, h0_ref, cw_ref, cb_ref, wa_ref, ba_ref, wx_ref, bx_ref, lam_ref, *rest,
                  reverse, has_acc):
    if has_acc:
        acc_ref, out_ref, hlast_ref, a_s, b_s, hp_s, halo_s = rest
    else:
        out_ref, hlast_ref, a_s, b_s, hp_s, halo_s = rest
    tb, c = u_ref.shape

    @pl.when(pl.program_id(1) == 0)
    def _():
        hp_s[...] = h0_ref[...]
        halo_s[...] = jnp.zeros_like(halo_s)

    cur = u_ref[...]
    halo = halo_s[...]
    xc = cb_ref[...] + cw_ref[RG_CONV - 1:RG_CONV, :] * cur
    for k in range(1, RG_CONV):
        xc = xc + cw_ref[RG_CONV - 1 - k:RG_CONV - k, :] * _shift_rows(cur, halo, k, reverse)
    halo_s[...] = cur[0:SUBLANES] if reverse else cur[tb - SUBLANES:]

    xb = xc.astype(BF16)
    gate_r = _sigmoid(jnp.dot(xb, wa_ref[...], preferred_element_type=F32) + ba_ref[...])
    gate_i = _sigmoid(jnp.dot(xb, wx_ref[...], preferred_element_type=F32) + bx_ref[...])
    lam = lam_ref[...]
    softplus_neg = jnp.maximum(-lam, 0.0) + jnp.log(1.0 + jnp.exp(-jnp.abs(lam)))
    a = jnp.exp(-RG_C * gate_r * softplus_neg)
    a_s[...] = a
    b_s[...] = jnp.sqrt(1.0 - a * a) * (gate_i * xc)

    row8 = lax.broadcasted_iota(jnp.int32, (SUBLANES, c), 0)
    nt = tb // SUBLANES

    def step(j, hp):
        jj = nt - 1 - j if reverse else j
        r0 = pl.multiple_of(jj * SUBLANES, SUBLANES)
        av = a_s[pl.ds(r0, SUBLANES), :]
        bv = b_s[pl.ds(r0, SUBLANES), :]
        for k in (1, 2, 4):
            if reverse:
                ok = row8 < SUBLANES - k
                sh = SUBLANES - k
            else:
                ok = row8 >= k
                sh = k
            a_prev = jnp.where(ok, pltpu.roll(av, sh, axis=0), 1.0)
            b_prev = jnp.where(ok, pltpu.roll(bv, sh, axis=0), 0.0)
            bv = av * b_prev + bv
            av = av * a_prev
        h = av * hp + bv
        if has_acc:
            out_ref[pl.ds(r0, SUBLANES), :] = h + acc_ref[pl.ds(r0, SUBLANES), :]
        else:
            out_ref[pl.ds(r0, SUBLANES), :] = h
        return h[0:1] if reverse else h[SUBLANES - 1:SUBLANES]

    hp = lax.fori_loop(0, nt, step, hp_s[...])
    hp_s[...] = hp
    hlast_ref[...] = hp


def rglru_scan(u, h0, p, reverse, acc=None, tb=512):
    b, n, c = u.shape
    tb = min(tb, n)
    assert n % tb == 0
    nblk = n // tb
    if reverse:
        seq = pl.BlockSpec((None, tb, c), lambda bi, i: (bi, nblk - 1 - i, 0))
    else:
        seq = pl.BlockSpec((None, tb, c), lambda bi, i: (bi, i, 0))
    state = pl.BlockSpec((None, 1, c), lambda bi, i: (bi, 0, 0))

    def par(a):
        return pl.BlockSpec(a.shape, lambda bi, i: (0,) * a.ndim)

    params = [p["conv_w"], p["conv_b"], p["wa"], p["ba"], p["wx"], p["bx"], p["lam"]]
    in_specs = [seq, state] + [par(a) for a in params]
    args = [u, h0] + params
    if acc is not None:
        in_specs.append(seq)
        args.append(acc)
    return pl.pallas_call(
        functools.partial(_rglru_kernel, reverse=reverse, has_acc=acc is not None),
        grid=(b, nblk),
        in_specs=in_specs,
        out_specs=[seq, state],
        out_shape=[jax.ShapeDtypeStruct((b, n, c), F32), jax.ShapeDtypeStruct((b, 1, c), F32)],
        scratch_shapes=[pltpu.VMEM((tb, c), F32), pltpu.VMEM((tb, c), F32),
                        pltpu.VMEM((1, c), F32), pltpu.VMEM((SUBLANES, c), F32)],
        compiler_params=_cparams(2),
        name="rglru_bwd" if reverse else "rglru_fwd",
    )(*args)


def _block_diag(w):
    g, bs, _ = w.shape
    eye = jnp.eye(g, dtype=w.dtype)
    return (eye[:, None, :, None] * w[:, :, None, :]).reshape(g * bs, g * bs)


def rglru_params(conv_w, conv_b, wa, ba, wx, bx, lam, d):
    c = conv_b.shape[-1]
    return {"conv_w": conv_w[d], "conv_b": conv_b[d].reshape(1, c),
            "wa": _block_diag(wa[d]).astype(BF16), "ba": ba[d].reshape(1, c),
            "wx": _block_diag(wx[d]).astype(BF16), "bx": bx[d].reshape(1, c),
            "lam": lam[d].reshape(1, c)}


def rglru_mix(u_ctx, u_lat, params):
    b, _, c = u_ctx.shape
    zero = jnp.zeros((b, 1, c), F32)
    h_ctx = h_lat = None
    for d in range(2):
        h_ctx, last = rglru_scan(u_ctx, zero, params[d], reverse=bool(d), acc=h_ctx)
        h_lat, _ = rglru_scan(u_lat, last, params[d], reverse=bool(d), acc=h_lat)
    return h_ctx, h_lat


def _gla_kernel(x_ref, s0_ref, wlr_ref, blr_ref, *rest, reverse, has_acc):
    if has_acc:
        acc_ref, o_ref, s_out_ref, st_s = rest
    else:
        o_ref, s_out_ref, st_s = rest
    nb, tb = x_ref.shape[0], x_ref.shape[1]
    ch = GLA_CHUNK

    @pl.when(pl.program_id(1) == 0)
    def _():
        st_s[...] = s0_ref[...]

    r_i = lax.broadcasted_iota(jnp.int32, (ch, ch), 0)
    c_i = lax.broadcasted_iota(jnp.int32, (ch, ch), 1)
    if reverse:
        cum_mat = (c_i >= r_i).astype(BF16)
        keep = c_i > r_i
    else:
        cum_mat = (c_i <= r_i).astype(BF16)
        keep = c_i <= r_i
    lane = lax.broadcasted_iota(jnp.int32, (1, LANES), 1)
    head_lanes = (lane < GLA_DK, lane >= GLA_DK)
    nt_dims = (((1,), (1,)), ((), ()))
    chunks = range(tb // ch)
    steps = [(cidx, bb) for cidx in (reversed(chunks) if reverse else chunks) for bb in range(nb)]
    for cidx, bb in steps:
        r0 = cidx * ch
        q = x_ref[bb, r0:r0 + ch, 0:GLA_QK] * (GLA_DK ** -0.5)
        k = x_ref[bb, r0:r0 + ch, GLA_QK:2 * GLA_QK]
        v = x_ref[bb, r0:r0 + ch, 2 * GLA_QK:2 * GLA_QK + GLA_V]
        lr = x_ref[bb, r0:r0 + ch, 2 * GLA_QK + GLA_V:GLA_IN_PAD]
        logits = jnp.dot(lr, wlr_ref[...], precision=HIGHEST, preferred_element_type=F32) + blr_ref[...]
        log_a = _log_sigmoid(logits) * (1.0 / GLA_TAU)
        la_hi = log_a.astype(BF16)
        la_mid, la_lo = _split_bf16(log_a - la_hi.astype(F32))
        cum = (jnp.dot(cum_mat, la_hi, preferred_element_type=F32)
               + (jnp.dot(cum_mat, la_mid, preferred_element_type=F32)
                  + jnp.dot(cum_mat, la_lo, preferred_element_type=F32)))
        tot = cum[0:1] if reverse else cum[ch - 1:ch]
        qg = q * jnp.exp(cum)
        kg = k * jnp.exp(-cum)
        kd = k * jnp.exp(tot - cum)
        decay = jnp.exp(tot)
        outs = []
        for h in range(GLA_HEADS):
            sl = slice((h // 2) * LANES, (h // 2 + 1) * LANES)
            mine = head_lanes[h % 2]
            qm = jnp.where(mine, qg[:, sl], 0.0).astype(BF16)
            scores = lax.dot_general(qm, kg[:, sl].astype(BF16), nt_dims, preferred_element_type=F32)
            scores = jnp.where(keep, scores, 0.0)
            vh = v[:, h * GLA_DV:(h + 1) * GLA_DV]
            st = st_s[bb, h]
            o = jnp.dot(scores.astype(BF16), vh.astype(BF16), preferred_element_type=F32)
            o = o + lax.dot_general(qm, st.astype(BF16), nt_dims, preferred_element_type=F32)
            kdm = jnp.where(mine, kd[:, sl], 0.0).astype(BF16)
            st_s[bb, h] = st * decay[:, sl] + jnp.dot(vh.T.astype(BF16), kdm, preferred_element_type=F32)
            outs.append(o)
        o_all = jnp.concatenate(outs, axis=1)
        if has_acc:
            o_all = o_all + acc_ref[bb, r0:r0 + ch, :]
        o_ref[bb, r0:r0 + ch, :] = o_all
    s_out_ref[...] = st_s[...]


def gla_scan(x, s0, wlr, blr, reverse, acc=None, tb=512, nb=2):
    b, n, cin = x.shape
    tb = min(tb, n)
    nb = min(nb, b)
    assert n % tb == 0 and tb % GLA_CHUNK == 0 and b % nb == 0
    nblk = n // tb

    def seq(width):
        if reverse:
            return pl.BlockSpec((nb, tb, width), lambda bi, i: (bi, nblk - 1 - i, 0))
        return pl.BlockSpec((nb, tb, width), lambda bi, i: (bi, i, 0))

    state = pl.BlockSpec((nb, GLA_HEADS, GLA_DV, LANES), lambda bi, i: (bi, 0, 0, 0))
    in_specs = [seq(cin), state,
                pl.BlockSpec(wlr.shape, lambda bi, i: (0, 0)), pl.BlockSpec(blr.shape, lambda bi, i: (0, 0))]
    args = [x, s0, wlr, blr]
    if acc is not None:
        in_specs.append(seq(GLA_V))
        args.append(acc)
    return pl.pallas_call(
        functools.partial(_gla_kernel, reverse=reverse, has_acc=acc is not None),
        grid=(b // nb, nblk),
        in_specs=in_specs,
        out_specs=[seq(GLA_V), state],
        out_shape=[jax.ShapeDtypeStruct((b, n, GLA_V), F32),
                   jax.ShapeDtypeStruct((b, GLA_HEADS, GLA_DV, LANES), F32)],
        scratch_shapes=[pltpu.VMEM((nb, GLA_HEADS, GLA_DV, LANES), F32)],
        compiler_params=_cparams(2, 40 << 20),
        name="gla_bwd" if reverse else "gla_fwd",
    )(*args)


def gla_params(w_lr, b_lr, d):
    w = jnp.zeros((LANES, GLA_QK), F32).at[d * GLA_RANK:(d + 1) * GLA_RANK].set(w_lr[d])
    return w, b_lr[d].reshape(1, GLA_QK)


def gla_mix(x_ctx, x_lat_cols, w_lr, b_lr):
    b = x_ctx.shape[0]
    zero = jnp.zeros((b, GLA_HEADS, GLA_DV, LANES), F32)
    o_ctx = o_lat = None
    for d in range(2):
        w, bias = gla_params(w_lr, b_lr, d)
        o_ctx, s = gla_scan(x_ctx, zero, w, bias, reverse=bool(d), acc=o_ctx)
        o_lat, _ = gla_scan(x_lat_cols, s, w, bias, reverse=bool(d), acc=o_lat)
    return o_ctx, o_lat


def raster_to_column(t):
    b, n = t.shape[:2]
    return t.reshape(b, n // GRID_W, GRID_W, *t.shape[2:]).swapaxes(1, 2).reshape(t.shape)


def column_to_raster(t):
    b, n = t.shape[:2]
    return t.reshape(b, GRID_W, n // GRID_W, *t.shape[2:]).swapaxes(1, 2).reshape(t.shape)


def _merge_kernel(x_ref, gt_ref, h_ref, yhy_ref, hrg_ref, ogla_ref,
                  wg_ref, bm_ref, gn_ref, why_ref, wrg_ref, wgla_ref, wout_ref, o_ref):
    d = x_ref.shape[1]

    def proj(y, w_ref):
        return jnp.dot(y.astype(BF16), w_ref[...], preferred_element_type=F32)

    gates_in = proj(h_ref[...], wg_ref)
    y_rg = hrg_ref[...] * _gelu_tanh(gates_in[:, 0:RG_W])
    o = ogla_ref[...]
    heads = []
    for h in range(GLA_HEADS):
        oh = o[:, h * GLA_DV:(h + 1) * GLA_DV]
        heads.append(oh * lax.rsqrt(jnp.mean(oh * oh, axis=-1, keepdims=True) + NORM_EPS) * gn_ref[...])
    gg = gates_in[:, RG_W:RG_W + GLA_V]
    y_gla = jnp.concatenate(heads, axis=1) * (gg * _sigmoid(gg))
    gate = _sigmoid(gates_in[:, RG_W + GLA_V:] + bm_ref[...])
    m = (gate[:, 0:d] * proj(yhy_ref[...], why_ref) + gate[:, d:2 * d] * proj(y_rg, wrg_ref)
         + gate[:, 2 * d:3 * d] * proj(y_gla, wgla_ref))
    o_ref[...] = x_ref[...] + gt_ref[...] * proj(m, wout_ref)


def merge_residual(x, gt, h, y_hy, h_rg, o_gla, lw, tm=256):
    b, n, d = x.shape
    tm = min(tm, n)
    assert n % tm == 0

    def seq(a):
        return pl.BlockSpec((None, tm, a.shape[2]), lambda bi, i: (bi, i, 0))

    def par(a):
        return pl.BlockSpec(a.shape, lambda bi, i: (0,) * a.ndim, pipeline_mode=pl.Buffered(1))

    streams = [h, y_hy, h_rg, o_gla]
    params = [lw["w_gates"], lw["b_merge"], lw["gla_norm_g"], lw["w_hy_o"], lw["w_rg_o"],
              lw["w_gla_o"], lw["w_out"]]
    return pl.pallas_call(
        _merge_kernel,
        grid=(b, n // tm),
        in_specs=[seq(x), pl.BlockSpec((None, 1, d), lambda bi, i: (bi, 0, 0))]
                 + [seq(a) for a in streams] + [par(a) for a in params],
        out_specs=seq(x),
        out_shape=jax.ShapeDtypeStruct((b, n, d), F32),
        compiler_params=_cparams(2, 48 << 20),
        name="merge_residual",
    )(x, gt, *streams, *params)


def _shortconv_kernel(x_ref, w_ref, b_ref, o_ref, *, rows):
    n, c = x_ref.shape
    zero = jnp.zeros((SUBLANES, c), F32)
    for r0 in range(0, n, rows):
        cur = x_ref[r0:r0 + rows, :]
        before = x_ref[r0 - SUBLANES:r0, :] if r0 > 0 else zero
        after = x_ref[r0 + rows:r0 + rows + SUBLANES, :] if r0 + rows < n else zero
        o_ref[r0:r0 + rows, :] = (b_ref[...] + w_ref[0:1, :] * _shift_rows(cur, before, 1, False)
                                  + w_ref[1:2, :] * cur + w_ref[2:3, :] * _shift_rows(cur, after, 1, True))


def shortconv(x, w, bias):
    b, n, c = x.shape
    blk = pl.BlockSpec((None, n, LANES), lambda bi, j: (bi, 0, j))
    return pl.pallas_call(
        functools.partial(_shortconv_kernel, rows=min(n, 1024)),
        grid=(b, c // LANES),
        in_specs=[blk, pl.BlockSpec((3, LANES), lambda bi, j: (0, j)),
                  pl.BlockSpec((1, LANES), lambda bi, j: (0, j))],
        out_specs=blk,
        out_shape=jax.ShapeDtypeStruct((b, n, c), F32),
        compiler_params=_cparams(2, 40 << 20),
        name="shortconv",
    )(x, w, bias.reshape(1, c))


def _hy_filter_kernel(w1_ref, b1_ref, w2_ref, b2_ref, w3_ref, freq_ref, band_ref, delta_ref,
                      h_ref, asum_ref, *, n):
    tb = h_ref.shape[0]
    i = pl.program_id(0)
    idx = (i * tb + lax.broadcasted_iota(jnp.int32, (tb, LANES), 0)).astype(F32)
    lane = lax.broadcasted_iota(jnp.int32, (tb, LANES), 1)
    tn = idx / (n - 1)
    ang = (2.0 * math.pi / n) * idx * band_ref[...]
    feats = jnp.where(lane == 0, tn,
                      jnp.where(lane <= HY_BANDS, jnp.cos(ang),
                                jnp.where(lane <= 2 * HY_BANDS, -jnp.sin(ang), 0.0)))
    fr = freq_ref[...]

    def dense(x, w_ref):
        return jnp.dot(x, w_ref[...], precision=HIGHEST, preferred_element_type=F32)

    h = jnp.sin(fr * (dense(feats, w1_ref) + b1_ref[...]))
    h = jnp.sin(fr * (dense(h, w2_ref) + b2_ref[...]))
    h = dense(h, w3_ref) * jnp.exp(-tn[:, 0:1] * delta_ref[...])
    h_ref[...] = h

    @pl.when(i == 0)
    def _():
        asum_ref[...] = jnp.zeros_like(asum_ref)

    asum_ref[...] += jnp.sum(jnp.abs(h), axis=0, keepdims=True)


def hyena_filter(n, w1, b1, w2, b2, w3, freq, tb=256):
    nout = w3.shape[1]
    tb = min(tb, n)
    bands = np.zeros((1, LANES), np.float32)
    lin = np.linspace(1e-4, HY_BANDS - 1, HY_BANDS, dtype=np.float32)
    bands[0, 1:1 + HY_BANDS] = lin
    bands[0, 1 + HY_BANDS:1 + 2 * HY_BANDS] = lin
    deltas = np.abs(np.linspace(HY_FAST_RATE, HY_SLOW_RATE, HY_W, dtype=np.float32))
    deltas = np.tile(deltas, nout // HY_W).reshape(1, nout)
    w1p = jnp.zeros((LANES, HY_FFN), F32).at[:w1.shape[0]].set(w1)
    params = [w1p, b1.reshape(1, -1), w2, b2.reshape(1, -1), w3, freq.reshape(1, -1),
              jnp.asarray(bands), jnp.asarray(deltas)]
    return pl.pallas_call(
        functools.partial(_hy_filter_kernel, n=n),
        grid=(n // tb,),
        in_specs=[pl.BlockSpec(a.shape, lambda i: (0, 0)) for a in params],
        out_specs=[pl.BlockSpec((tb, nout), lambda i: (i, 0)), pl.BlockSpec((1, nout), lambda i: (0, 0))],
        out_shape=[jax.ShapeDtypeStruct((n, nout), F32), jax.ShapeDtypeStruct((1, nout), F32)],
        compiler_params=_cparams(1),
        name="hyena_filter",
    )(*params)


def _filter_taps(hraw, n):
    h = hraw.reshape(n, 2, 2, HY_W)
    h_fwd = jnp.moveaxis(h[:, :, 0], 1, 0)
    h_bwd = jnp.moveaxis(h[:, :, 1], 1, 0)
    zeros = jnp.zeros_like(h_fwd)
    taps_f = jnp.concatenate([h_fwd, zeros], axis=1)
    taps_b = jnp.concatenate([h_bwd[:, :1], zeros, h_bwd[:, :0:-1]], axis=1)
    return taps_f, taps_b


def _cis(num, den):
    ang = (2.0 * math.pi / den) * (num % den).astype(F32)
    return jnp.cos(ang), jnp.sin(ang)


def _dft_tables(n):
    big = 2 * n
    q = int(round(math.sqrt(big)))
    assert q * q == big and q % (2 * SUBLANES) == 0
    ar = jnp.arange(q, dtype=jnp.int32)
    num = ar[None, :, None] * (q * ar[None, None, :] + ar[:, None, None])
    c, s = _cis(num, big)
    w1 = jnp.concatenate([c, -s], axis=1)
    ct, st = jnp.swapaxes(c, 1, 2)[:, :q // 2], jnp.swapaxes(s, 1, 2)[:, :q // 2]
    v = jnp.concatenate([ct, -st], axis=2) * (1.0 / big)
    c2, s2 = _cis(ar[:, None] * ar[None, :], q)
    f2 = jnp.concatenate([jnp.concatenate([c2, s2], axis=1), jnp.concatenate([-s2, c2], axis=1)], axis=0)
    g2 = jnp.concatenate([jnp.concatenate([c2, -s2], axis=1), jnp.concatenate([s2, c2], axis=1)], axis=0)
    return {"q": q, "w1": w1, "v": v, "f2": f2, "g2": g2}


def _level_kernel(w_ref, x_ref, *rest, n_add, has_gate, precise):
    rest = list(rest)
    x2_ref = rest.pop(0) if n_add else None
    if has_gate:
        src_ref, gate_ref, skip_ref = rest[:3]
        rest = rest[3:]
    o_ref = rest[0]
    for s in range(w_ref.shape[0]):
        x = x_ref[s]
        if n_add:
            x = x + x2_ref[s]
        if precise:
            y = jnp.dot(w_ref[s], x, precision=HIGHEST, preferred_element_type=F32)
        else:
            y = jnp.dot(w_ref[s], x.astype(BF16), preferred_element_type=F32)
        if has_gate:
            y = gate_ref[s] * (y + src_ref[s] * skip_ref[...])
        o_ref[s] = y.astype(o_ref.dtype)


def dft_level(w, x, x_group=0, add=None, gate=None, out_dtype=F32, precise=False, sb=8):
    b, q, k, _ = x.shape
    m = w.shape[1]
    c = HY_W
    sb = min(sb, q)

    def seq(rows, group):
        return pl.BlockSpec((None, sb, rows, c), lambda i, bi: (bi, i, 0, group))

    in_specs = [pl.BlockSpec((sb, m, k), lambda i, bi: (i, 0, 0)), seq(k, x_group)]
    args = [w, x]
    if add is not None:
        in_specs.append(seq(k, 0))
        args.append(add)
    if gate is not None:
        src, src_group, gates, gate_group, skip = gate
        in_specs += [seq(m, src_group), seq(m, gate_group), pl.BlockSpec((1, c), lambda i, bi: (0, 0))]
        args += [src, gates, skip]
    return pl.pallas_call(
        functools.partial(_level_kernel, n_add=add is not None, has_gate=gate is not None,
                          precise=precise),
        grid=(q // sb, b),
        in_specs=in_specs,
        out_specs=seq(m, 0),
        out_shape=jax.ShapeDtypeStruct((b, q, m, c), out_dtype),
        compiler_params=_cparams(2, 40 << 20),
        name="dft_level",
    )(*args)


def _dft_mid_kernel(a_ref, h_ref, f_ref, g_ref, o_ref):
    q = a_ref.shape[1] // 2
    for j in range(a_ref.shape[0]):
        x = jnp.dot(f_ref[...], a_ref[j], preferred_element_type=F32)
        xr, xi = x[:q], x[q:]
        hr, hi = h_ref[j, :q, :], h_ref[j, q:, :]
        y = jnp.concatenate([xr * hr - xi * hi, xr * hi + xi * hr], axis=0)
        o_ref[j] = jnp.dot(g_ref[...], y.astype(BF16), preferred_element_type=F32).astype(o_ref.dtype)


def dft_mid(a, spec, f2, g2, kb=8):
    b, q, q2, c = a.shape
    blk = pl.BlockSpec((None, kb, q2, c), lambda i, bi: (bi, i, 0, 0))
    mat = pl.BlockSpec((q2, q2), lambda i, bi: (0, 0))
    return pl.pallas_call(
        _dft_mid_kernel,
        grid=(q // kb, b),
        in_specs=[blk, pl.BlockSpec((kb, q2, c), lambda i, bi: (i, 0, 0)), mat, mat],
        out_specs=blk,
        out_shape=jax.ShapeDtypeStruct(a.shape, BF16),
        compiler_params=_cparams(2, 40 << 20),
        name="dft_mid",
    )(a, spec, f2, g2)


def _dft_spec_kernel(a_ref, f_ref, asum_ref, o_ref):
    inv = 1.0 / (asum_ref[0:1, :] + asum_ref[1:2, :] + 1e-6)
    for j in range(a_ref.shape[0]):
        o_ref[j] = jnp.dot(f_ref[...], a_ref[j], precision=HIGHEST, preferred_element_type=F32) * inv


def dft_spec(a, f2, asum, kb=8):
    o, q, q2, c = a.shape
    blk = pl.BlockSpec((None, kb, q2, c), lambda i, oi: (oi, i, 0, 0))
    return pl.pallas_call(
        _dft_spec_kernel,
        grid=(q // kb, o),
        in_specs=[blk, pl.BlockSpec((q2, q2), lambda i, oi: (0, 0)),
                  pl.BlockSpec((None, 2, c), lambda i, oi: (oi, 0, 0))],
        out_specs=blk,
        out_shape=jax.ShapeDtypeStruct(a.shape, F32),
        compiler_params=_cparams(2, 40 << 20),
        name="dft_spec",
    )(a, f2, asum)


def _swap_levels(a):
    b, q, q2, c = a.shape
    return a.reshape(b, q, 2, q, c).transpose(0, 3, 2, 1, 4).reshape(b, q, q2, c)


def hyena_long(u, hraw, asum, skip, tabs):
    b, n, _ = u.shape
    q = tabs["q"]
    c = HY_W
    taps_f, taps_b = _filter_taps(hraw, n)

    def to_levels(t, rows):
        return t.reshape(t.shape[0], rows, q, t.shape[2]).swapaxes(1, 2)

    a = dft_level(tabs["w1"], to_levels(taps_f, q), add=to_levels(taps_b, q), precise=True)
    spec = dft_spec(_swap_levels(a), tabs["f2"], asum.reshape(2, 2, c))
    w1d = tabs["w1"][:, :, :q // 2].astype(BF16)
    vd = tabs["v"].astype(BF16)
    f2, g2 = tabs["f2"].astype(BF16), tabs["g2"].astype(BF16)
    u_t = to_levels(u, q // 2)
    src, group = u_t, 0
    for order in range(2):
        a = dft_level(w1d, src, x_group=group, out_dtype=BF16)
        cm = dft_mid(_swap_levels(a), spec[order], f2, g2)
        src = dft_level(vd, _swap_levels(cm),
                        gate=(src, group, u_t, order + 1, skip[order].reshape(1, c)))
        group = 0
    return src.swapaxes(1, 2).reshape(b, n, c)


def _dft_small_kernel(u_ref, gate_ref, skip_ref, h_ref, f_ref, g_ref, o_ref):
    u = u_ref[...]
    nb = h_ref.shape[0] // 2
    x = jnp.dot(f_ref[...], u, precision=HIGHEST, preferred_element_type=F32)
    xr, xi = x[:nb], x[nb:]
    hr, hi = h_ref[:nb, :], h_ref[nb:, :]
    y = jnp.concatenate([xr * hr - xi * hi, xr * hi + xi * hr], axis=0)
    conv = jnp.dot(g_ref[...], y, precision=HIGHEST, preferred_element_type=F32)
    o_ref[...] = gate_ref[...] * (conv + u * skip_ref[...])


def _dft_small_spec_kernel(tf_ref, tb_ref, f_ref, asum_ref, o_ref):
    inv = 1.0 / (asum_ref[0:1, :] + asum_ref[1:2, :] + 1e-6)
    o_ref[...] = jnp.dot(f_ref[...], tf_ref[...] + tb_ref[...], precision=HIGHEST,
                         preferred_element_type=F32) * inv


def hyena_short_seq(u, hraw, asum, skip):
    b, n, _ = u.shape
    big = 2 * n
    c = HY_W
    ar = jnp.arange(big, dtype=jnp.int32)
    cs, sn = _cis(ar[:, None] * ar[None, :], big)
    f_full = jnp.concatenate([cs, -sn], axis=0)
    g_half = jnp.concatenate([cs[:n], -sn[:n]], axis=1) * (1.0 / big)
    taps_f, taps_b = _filter_taps(hraw, n)
    tap = pl.BlockSpec((None, big, c), lambda o: (o, 0, 0))
    spec = pl.pallas_call(
        _dft_small_spec_kernel,
        grid=(2,),
        in_specs=[tap, tap, pl.BlockSpec((2 * big, big), lambda o: (0, 0)),
                  pl.BlockSpec((None, 2, c), lambda o: (o, 0, 0))],
        out_specs=pl.BlockSpec((None, 2 * big, c), lambda o: (o, 0, 0)),
        out_shape=jax.ShapeDtypeStruct((2, 2 * big, c), F32),
        compiler_params=_cparams(1),
        name="dft_small_spec",
    )(taps_f, taps_b, f_full, asum.reshape(2, 2, c))
    f_data = f_full[:, :n]
    src, group = u, 0
    for order in range(2):
        src = pl.pallas_call(
            _dft_small_kernel,
            grid=(b,),
            in_specs=[pl.BlockSpec((None, n, c), functools.partial(lambda bi, g: (bi, 0, g), g=group)),
                      pl.BlockSpec((None, n, c), functools.partial(lambda bi, g: (bi, 0, g), g=order + 1)),
                      pl.BlockSpec((1, c), lambda bi: (0, 0)),
                      pl.BlockSpec((2 * big, c), lambda bi: (0, 0)),
                      pl.BlockSpec((2 * big, n), lambda bi: (0, 0)),
                      pl.BlockSpec((n, 2 * big), lambda bi: (0, 0))],
            out_specs=pl.BlockSpec((None, n, c), lambda bi: (bi, 0, 0)),
            out_shape=jax.ShapeDtypeStruct((b, n, c), F32),
            compiler_params=_cparams(1),
            name="dft_small",
        )(src, u, skip[order].reshape(1, c), spec[order], f_data, g_half)
        group = 0
    return src


HY_IN = 3 * HY_W
IN_GROUPS = (HY_IN, RG_W, RG_W, GLA_IN, GLA_V, 3 * D_MODEL)


def _split_w_in(w):
    parts, start = [], 0
    for width in IN_GROUPS:
        parts.append(w[:, start:start + width])
        start += width
    parts[3] = jnp.pad(parts[3], ((0, 0), (0, GLA_IN_PAD - GLA_IN)))
    return [p.astype(BF16) for p in parts]


def _project(h, w):
    b, n, d = h.shape
    return matmul(h.reshape(b * n, d), w).reshape(b, n, w.shape[1])


def kernel(x, c, ctx, c_ctx, w_mod, b_mod, g_norm_mix, g_norm_ffn, w_in, hy_conv_w, hy_conv_b,
           hy_w1, hy_b1, hy_w2, hy_b2, hy_w3, hy_freq, hy_skip, rg_conv_w, rg_conv_b, rg_wa, rg_ba,
           rg_wx, rg_bx, rg_lambda, gla_w_lr, gla_b_lr, gla_norm_g, w_hy_o, w_rg_o, w_gla_o, b_merge,
           w_out, peer_wq, peer_keys, peer_u, peer_v, g_final):
    b, n, d = x.shape
    n_ctx = ctx.shape[1]
    depth = w_mod.shape[0]
    cond = jnp.concatenate([c, c_ctx[None, :]], axis=0)
    cond = jnp.pad(cond, ((0, -(b + 1) % SUBLANES), (0, 0)))
    tabs = _dft_tables(n)
    x_lat, x_ctx = x, ctx
    for l in range(depth):
        need_ctx = l < depth - 1
        mod = matmul(cond, w_mod[l].astype(BF16), bias=b_mod[l], silu_in=True)
        sh1, sc1, gt1, sh2, sc2, gt2 = [mod[:b, i * d:(i + 1) * d].reshape(b, 1, d) for i in range(6)]
        csh1, csc1, cgt1, csh2, csc2, cgt2 = [
            jnp.broadcast_to(mod[b:b + 1, i * d:(i + 1) * d].reshape(1, 1, d), (b, 1, d)) for i in range(6)]
        w_hy, w_rgx, w_rgg, w_gla, w_glag, w_mg = _split_w_in(w_in[l])
        lw = {"b_merge": b_merge[l].reshape(1, 3 * d), "gla_norm_g": gla_norm_g[l].reshape(1, GLA_DV),
              "w_gates": jnp.concatenate([w_rgg, w_glag, w_mg], axis=1),
              "w_hy_o": w_hy_o[l].astype(BF16), "w_rg_o": w_rg_o[l].astype(BF16),
              "w_gla_o": w_gla_o[l].astype(BF16), "w_out": w_out[l].astype(BF16)}
        rg_par = [rglru_params(rg_conv_w[l], rg_conv_b[l], rg_wa[l], rg_ba[l], rg_wx[l], rg_bx[l],
                               rg_lambda[l], dd) for dd in range(2)]
        filt = (hy_w1[l], hy_b1[l], hy_w2[l], hy_b2[l], hy_w3[l], hy_freq[l])
        u_pack, v_pack = _pack_table(peer_u[l]), _pack_table(peer_v[l])

        h_lat = normmod(x_lat, g_norm_mix[l], sh1, sc1, BF16)
        h_ctx = normmod(x_ctx, g_norm_mix[l], csh1, csc1, BF16)
        hy_l, rgx_l, gla_l = [_project(h_lat, w) for w in (w_hy, w_rgx, w_gla)]
        rgx_c, gla_c = [_project(h_ctx, w) for w in (w_rgx, w_gla)]

        hraw, asum = hyena_filter(n, *filt)
        y_hy_l = hyena_long(shortconv(hy_l, hy_conv_w[l], hy_conv_b[l]), hraw, asum, hy_skip[l], tabs)
        h_rg_c, h_rg_l = rglru_mix(rgx_c, rgx_l, rg_par)
        o_gla_c, o_gla_l = gla_mix(gla_c, raster_to_column(gla_l), gla_w_lr[l], gla_b_lr[l])
        x_lat = merge_residual(x_lat, gt1, h_lat, y_hy_l, h_rg_l, column_to_raster(o_gla_l), lw)
        h2 = normmod(x_lat, g_norm_ffn[l], sh2, sc2, F32)
        x_lat = peer_residual(x_lat, gt2, h2, peer_wq[l], peer_keys[l], u_pack, v_pack)
        if need_ctx:
            hraw_c, asum_c = hyena_filter(n_ctx, *filt)
            y_hy_c = hyena_short_seq(shortconv(_project(h_ctx, w_hy), hy_conv_w[l], hy_conv_b[l]),
                                     hraw_c, asum_c, hy_skip[l])
            x_ctx = merge_residual(x_ctx, cgt1, h_ctx, y_hy_c, h_rg_c, o_gla_c, lw)
            h2c = normmod(x_ctx, g_norm_ffn[l], csh2, csc2, F32)
            x_ctx = peer_residual(x_ctx, cgt2, h2c, peer_wq[l], peer_keys[l], u_pack, v_pack)
    zero = jnp.zeros((b, 1, d), F32)
    return normmod(x_lat, g_final, zero, zero, F32, mod=False)
```

```python
import functools
import math

import numpy as np
import jax
import jax.numpy as jnp
from jax import lax
from jax.experimental import pallas as pl
from jax.experimental.pallas import tpu as pltpu

F32 = jnp.float32
BF16 = jnp.bfloat16
HIGHEST = lax.Precision.HIGHEST

D_MODEL = 1024
DEPTH = 4
GRID_W = 64
NORM_EPS = 1e-6

HY_W = 512
HY_BANDS = 16
HY_FFN = 64
HY_FAST_RATE = math.log(1e-2) / 0.3
HY_SLOW_RATE = math.log(1e-2) / 1.5

RG_W = 512
RG_BLOCKS = 8
RG_C = 8.0

GLA_HEADS = 4
GLA_DK = 64
GLA_DV = 128
GLA_QK = GLA_HEADS * GLA_DK
GLA_V = GLA_HEADS * GLA_DV
GLA_RANK = 16
GLA_TAU = 16.0
GLA_CHUNK = 64
GLA_IN = 2 * GLA_QK + GLA_V + 2 * GLA_RANK
GLA_IN_PAD = 2 * GLA_QK + GLA_V + 128

PEER_HEADS = 8
PEER_NKEYS = 128
PEER_TOPK = 16
PEER_HALF = 128
PEER_SEL = PEER_HEADS * PEER_TOPK

V7X_VMEM_BYTES = 64 * 1024 * 1024
SUBLANES = 8
LANES = 128


def _cparams(n_grid, vmem_bytes=None):
    kw = dict(dimension_semantics=("arbitrary",) * n_grid)
    if vmem_bytes is not None:
        assert vmem_bytes < V7X_VMEM_BYTES
        kw["vmem_limit_bytes"] = int(vmem_bytes)
    return pltpu.CompilerParams(**kw)


def _gelu_tanh(x):
    return 0.5 * x * (1.0 + jnp.tanh(math.sqrt(2.0 / math.pi) * (x + 0.044715 * (x * x * x))))


def _sigmoid(x):
    return 1.0 / (1.0 + jnp.exp(-x))


def _log_sigmoid(x):
    return jnp.minimum(x, 0.0) - jnp.log(1.0 + jnp.exp(-jnp.abs(x)))


def _mm_kernel(x_ref, w_ref, *rest, silu_in, has_bias):
    o_ref = rest[-1]
    x = x_ref[...]
    if silu_in:
        x = x * _sigmoid(x)
    acc = jnp.dot(x.astype(BF16), w_ref[...], preferred_element_type=F32)
    if has_bias:
        acc = acc + rest[0][...]
    o_ref[...] = acc.astype(o_ref.dtype)


def matmul(x, w, bias=None, silu_in=False, tm=512):
    m, k = x.shape
    n = w.shape[1]
    tm = min(tm, m)
    assert m % tm == 0
    in_specs = [pl.BlockSpec((tm, k), lambda i: (i, 0)), pl.BlockSpec((k, n), lambda i: (0, 0))]
    args = [x, w]
    if bias is not None:
        in_specs.append(pl.BlockSpec((1, n), lambda i: (0, 0)))
        args.append(bias.reshape(1, n))
    est = 2 * (tm * k * x.dtype.itemsize + k * n * 2 + tm * n * 4) + (4 << 20)
    return pl.pallas_call(
        functools.partial(_mm_kernel, silu_in=silu_in, has_bias=bias is not None),
        grid=(m // tm,),
        in_specs=in_specs,
        out_specs=pl.BlockSpec((tm, n), lambda i: (i, 0)),
        out_shape=jax.ShapeDtypeStruct((m, n), F32),
        compiler_params=_cparams(1, est),
        name="matmul",
    )(*args)


def _normmod_kernel(x_ref, g_ref, sh_ref, sc_ref, o_ref, *, mod):
    x = x_ref[...]
    y = x * lax.rsqrt(jnp.mean(x * x, axis=-1, keepdims=True) + NORM_EPS) * g_ref[...]
    if mod:
        y = y * (1.0 + sc_ref[...]) + sh_ref[...]
    o_ref[...] = y.astype(o_ref.dtype)


def normmod(x, g, shift, scale, out_dtype, mod=True, tm=512):
    b, l, d = x.shape
    tm = min(tm, l)
    assert l % tm == 0
    vec = pl.BlockSpec((None, 1, d), lambda bi, i: (bi, 0, 0))
    return pl.pallas_call(
        functools.partial(_normmod_kernel, mod=mod),
        grid=(b, l // tm),
        in_specs=[pl.BlockSpec((None, tm, d), lambda bi, i: (bi, i, 0)),
                  pl.BlockSpec((1, d), lambda bi, i: (0, 0)), vec, vec],
        out_specs=pl.BlockSpec((None, tm, d), lambda bi, i: (bi, i, 0)),
        out_shape=jax.ShapeDtypeStruct((b, l, d), out_dtype),
        compiler_params=_cparams(2),
        name="normmod",
    )(x, g.reshape(1, d), shift, scale)


def _assemble_rows(rows, n):
    t = rows[0].shape[1]
    rid = lax.broadcasted_iota(jnp.int32, (n, t), 0)
    out = jnp.zeros((n, t), rows[0].dtype)
    for r in range(n):
        out = jnp.where(rid == r, rows[r], out)
    return out


def _extract_topk(s, rowid, k):
    vals, ids = [], []
    for _ in range(k):
        m = jnp.max(s, axis=0, keepdims=True)
        first = jnp.min(jnp.where(s == m, rowid, 1e9), axis=0, keepdims=True)
        s = jnp.where(rowid == first, -jnp.inf, s)
        vals.append(m)
        ids.append(first)
    return vals, ids


def _split_bf16(x):
    hi = x.astype(BF16)
    return hi, (x - hi.astype(F32)).astype(BF16)


def _dot3(a_hi, a_lo, b_hi, b_lo, dims):
    def one(a, b):
        return lax.dot_general(a, b, dims, preferred_element_type=F32)
    return one(a_hi, b_hi) + (one(a_hi, b_lo) + one(a_lo, b_hi))


def _peer_select_kernel(h_ref, wqh_ref, wql_ref, kh_ref, kl_ref, exp_ref, wgt_ref, q_ref):
    h_hi, h_lo = _split_bf16(h_ref[...])
    q_ref[...] = _dot3(h_hi, h_lo, wqh_ref[...], wql_ref[...], (((1,), (0,)), ((), ())))

    def lane_block(j, carry):
        r0 = pl.multiple_of(j * LANES, LANES)
        rows, weights = _select_tokens(q_ref[pl.ds(r0, LANES), :], kh_ref, kl_ref)
        exp_ref[pl.ds(r0, LANES), :] = rows
        wgt_ref[pl.ds(r0, LANES), :] = weights
        return carry

    lax.fori_loop(0, h_ref.shape[0] // LANES, lane_block, 0)


def _select_tokens(q, kh_ref, kl_ref):
    tt = q.shape[0]
    k = PEER_TOPK
    key_id = lax.broadcasted_iota(jnp.int32, (PEER_NKEYS, tt), 0).astype(F32)
    row8 = lax.broadcasted_iota(jnp.int32, (SUBLANES, tt), 0)
    row16 = lax.broadcasted_iota(jnp.int32, (2 * SUBLANES, tt), 0)
    exp_blocks, wgt_blocks = [], []
    for h in range(PEER_HEADS):
        tops = []
        for p in range(2):
            col = (h * 2 + p) * PEER_HALF
            q_hi, q_lo = _split_bf16(q[:, col:col + PEER_HALF])
            s = _dot3(kh_ref[h, p], kl_ref[h, p], q_hi, q_lo, (((1,), (1,)), ((), ())))
            vals, ids = _extract_topk(s, key_id, k)
            tops.append((_assemble_rows(vals, k), _assemble_rows(ids, k)))
        (a, ia), (b, ib) = tops
        blocks, flat = [a[0:1] + b], [row16.astype(F32)]
        for i in range(1, 8):
            nj = k // (i + 1)
            blocks.append(jnp.where(row8 < nj, a[i:i + 1] + b[0:8], -jnp.inf))
            flat.append((row8 + i * k).astype(F32))
        blocks.append(a[8:16] + b[0:1])
        flat.append(((row8 + 8) * k).astype(F32))
        cand = jnp.concatenate(blocks, axis=0)
        cand_id = jnp.concatenate(flat, axis=0)
        vals, ids = _extract_topk(cand, cand_id, k)
        best = _assemble_rows(vals, k)
        fl = _assemble_rows(ids, k)
        fi = jnp.floor(fl * (1.0 / k))
        fj = fl - fi * k
        ei = jnp.zeros_like(fl)
        ej = jnp.zeros_like(fl)
        for r in range(k):
            ei = jnp.where(fi == r, ia[r:r + 1], ei)
            ej = jnp.where(fj == r, ib[r:r + 1], ej)
        e = jnp.exp(best - jnp.max(best, axis=0, keepdims=True))
        wgt_blocks.append(e / jnp.sum(e, axis=0, keepdims=True))
        exp_blocks.append((ei * PEER_NKEYS + ej) * ROW_WORDS)
    return (jnp.concatenate(exp_blocks, axis=0).T.astype(jnp.int32),
            jnp.concatenate(wgt_blocks, axis=0).T)


def peer_select(h, wq, keys, tt=512):
    t, d = h.shape
    tt = min(tt, t)
    assert t % tt == 0 and tt % LANES == 0
    nq = wq.shape[1]
    wq_hi, wq_lo = _split_bf16(wq)
    k_hi, k_lo = _split_bf16(keys)
    wspec = pl.BlockSpec((d, nq), lambda i: (0, 0))
    kspec = pl.BlockSpec(keys.shape, lambda i: (0, 0, 0, 0))
    return pl.pallas_call(
        _peer_select_kernel,
        grid=(t // tt,),
        in_specs=[pl.BlockSpec((tt, d), lambda i: (i, 0)), wspec, wspec, kspec, kspec],
        out_specs=[pl.BlockSpec((tt, PEER_SEL), lambda i: (i, 0)),
                   pl.BlockSpec((tt, PEER_SEL), lambda i: (i, 0))],
        out_shape=[jax.ShapeDtypeStruct((t, PEER_SEL), jnp.int32),
                   jax.ShapeDtypeStruct((t, PEER_SEL), F32)],
        scratch_shapes=[pltpu.VMEM((tt, nq), F32)],
        compiler_params=_cparams(1, 2 * (2 * d * nq + 2 * keys.size) * 2 + 3 * tt * (d + nq) * 4 + (8 << 20)),
        name="peer_select",
    )(h, wq_hi, wq_lo, k_hi, k_lo)


ROW_WORDS = D_MODEL // 2 // LANES


def _unpack_pair(w):
    lo = lax.bitcast_convert_type(lax.shift_left(w, jnp.int32(16)), F32)
    hi = lax.bitcast_convert_type(jnp.bitwise_and(w, jnp.int32(-65536)), F32)
    return lo, hi


PEER_CHUNK = 128
PEER_NCHUNK = PEER_SEL // PEER_CHUNK
PEER_GROUP = 8


def _rows_to_tiles(x8):
    row8 = lax.broadcasted_iota(jnp.int32, (SUBLANES, LANES), 0)
    tiles = []
    for tk in range(SUBLANES):
        tile = jnp.zeros((SUBLANES, LANES), x8.dtype)
        for r in range(SUBLANES):
            tile = jnp.where(row8 == r, x8[tk:tk + 1, r * LANES:(r + 1) * LANES], tile)
        tiles.append(tile)
    return tiles


def _tiles_to_rows(tiles):
    row8 = lax.broadcasted_iota(jnp.int32, (SUBLANES, LANES), 0)
    cols = []
    for r in range(SUBLANES):
        col = jnp.zeros((SUBLANES, LANES), tiles[0].dtype)
        for tk in range(SUBLANES):
            col = jnp.where(row8 == tk, tiles[tk][r:r + 1, :], col)
        cols.append(col)
    return jnp.concatenate(cols, axis=1)


def _peer_act_kernel(idx_ref, x_ref, wgt_ref, tab_ref, coef_ref, p_ref, xs_ref):
    tt = x_ref.shape[0]

    def group(g, carry):
        t0 = pl.multiple_of(g * PEER_GROUP, PEER_GROUP)
        for tk, tile in enumerate(_rows_to_tiles(x_ref[pl.ds(t0, PEER_GROUP), :])):
            xs_ref[tk] = tile

        def token(tk, c):
            xl = xs_ref[tk, 0:ROW_WORDS, :]
            xh = xs_ref[tk, ROW_WORDS:2 * ROW_WORDS, :]

            def chunk(ci, c2):
                base = ((t0 + tk) * PEER_NCHUNK + ci) * PEER_CHUNK
                for j in range(PEER_CHUNK):
                    r = pl.multiple_of(idx_ref[base + j], ROW_WORDS)
                    lo, hi = _unpack_pair(tab_ref[pl.ds(r, ROW_WORDS), :])
                    p_ref[tk * PEER_NCHUNK + ci, j * ROW_WORDS:(j + 1) * ROW_WORDS, :] = lo * xl + hi * xh
                return c2

            return lax.fori_loop(0, PEER_NCHUNK, chunk, c)

        lax.fori_loop(0, PEER_GROUP, token, 0)
        for tk in range(PEER_GROUP):
            tok = p_ref.at[tk * PEER_NCHUNK:(tk + 1) * PEER_NCHUNK]
            parts = [tok[:, pl.ds(r, PEER_CHUNK, stride=ROW_WORDS), :].reshape(PEER_SEL, LANES)
                     for r in range(ROW_WORDS)]
            per_lane = (parts[0] + parts[1]) + (parts[2] + parts[3])
            act = jnp.sum(per_lane.T, axis=0, keepdims=True)
            coef_ref[t0 + tk] = wgt_ref[t0 + tk] * _gelu_tanh(act)
        return carry

    lax.fori_loop(0, tt // PEER_GROUP, group, 0)


def _peer_out_kernel(idx_ref, coef_ref, res_ref, gate_ref, tab_ref, out_ref, ys_ref):
    tt = out_ref.shape[0]
    zero = jnp.zeros((ROW_WORDS, LANES), F32)

    def group(g, carry):
        t0 = pl.multiple_of(g * PEER_GROUP, PEER_GROUP)

        def token(tk, c):
            def chunk(ci, acc):
                acc = list(acc)
                base = ((t0 + tk) * PEER_NCHUNK + ci) * PEER_CHUNK
                for j in range(PEER_CHUNK):
                    r = pl.multiple_of(idx_ref[base + j], ROW_WORDS)
                    lo, hi = _unpack_pair(tab_ref[pl.ds(r, ROW_WORDS), :])
                    cf = coef_ref[base + j]
                    acc[2 * (j % 2)] = acc[2 * (j % 2)] + cf * lo
                    acc[2 * (j % 2) + 1] = acc[2 * (j % 2) + 1] + cf * hi
                return tuple(acc)

            acc = lax.fori_loop(0, PEER_NCHUNK, chunk, (zero, zero, zero, zero))
            ys_ref[tk, 0:ROW_WORDS, :] = acc[0] + acc[2]
            ys_ref[tk, ROW_WORDS:2 * ROW_WORDS, :] = acc[1] + acc[3]
            return c

        lax.fori_loop(0, PEER_GROUP, token, 0)
        rows = _tiles_to_rows([ys_ref[tk] for tk in range(PEER_GROUP)])
        out_ref[pl.ds(t0, PEER_GROUP), :] = res_ref[pl.ds(t0, PEER_GROUP), :] + gate_ref[...] * rows
        return carry

    lax.fori_loop(0, tt // PEER_GROUP, group, 0)


def _pack_table(tab):
    e, d = tab.shape
    pairs = jnp.moveaxis(tab.astype(BF16).reshape(e, 2, d // 2), 1, 2)
    return lax.bitcast_convert_type(pairs, jnp.int32).reshape(e * ROW_WORDS, LANES)


def _table_spec(shape):
    return pl.BlockSpec(shape, lambda i: (0, 0), pipeline_mode=pl.Buffered(1))


def peer_experts(h, expert, weight, u_pack, v_pack, res, gate, tokens_per_gate, tt=128):
    t, d = h.shape
    assert t % tt == 0 and tokens_per_gate % tt == 0 and tt % PEER_GROUP == 0
    assert PEER_GROUP == SUBLANES and d == 2 * ROW_WORDS * LANES
    vmem = u_pack.size * 4 + (16 << 20)
    smem_flat = pl.BlockSpec((tt * PEER_SEL,), lambda i: (i,), memory_space=pltpu.SMEM)
    xspec = pl.BlockSpec((tt, d), lambda i: (i, 0))
    rowspec = pl.BlockSpec((tt, 1, PEER_SEL), lambda i: (i, 0, 0))
    tile_scratch = pltpu.VMEM((PEER_GROUP, SUBLANES, LANES), F32)
    rows = expert.reshape(t * PEER_SEL)
    coef = pl.pallas_call(
        _peer_act_kernel,
        grid=(t // tt,),
        in_specs=[smem_flat, xspec, rowspec, _table_spec(u_pack.shape)],
        out_specs=rowspec,
        out_shape=jax.ShapeDtypeStruct((t, 1, PEER_SEL), F32),
        scratch_shapes=[pltpu.VMEM((PEER_GROUP * PEER_NCHUNK, PEER_CHUNK * ROW_WORDS, LANES), F32),
                        tile_scratch],
        compiler_params=_cparams(1, vmem),
        name="peer_act",
    )(rows, h, weight.reshape(t, 1, PEER_SEL), u_pack)
    return pl.pallas_call(
        _peer_out_kernel,
        grid=(t // tt,),
        in_specs=[smem_flat, smem_flat, xspec,
                  pl.BlockSpec((None, 1, d), lambda i: (i * tt // tokens_per_gate, 0, 0)),
                  _table_spec(v_pack.shape)],
        out_specs=xspec,
        out_shape=jax.ShapeDtypeStruct((t, d), F32),
        scratch_shapes=[tile_scratch],
        compiler_params=_cparams(1, vmem),
        name="peer_out",
    )(rows, coef.reshape(t * PEER_SEL), res, gate, v_pack)


def peer_residual(x, gate, h, wq, keys, u_pack, v_pack):
    b, n, d = h.shape
    hf = h.reshape(b * n, d)
    expert, weight = peer_select(hf, wq, keys)
    out = peer_experts(hf, expert, weight, u_pack, v_pack, x.reshape(b * n, d), gate, n)
    return out.reshape(b, n, d)


RG_CONV = 4


def _shift_rows(cur, halo, k, reverse):
    tb = cur.shape[0]
    row8 = lax.broadcasted_iota(jnp.int32, (SUBLANES, cur.shape[1]), 0)
    if not reverse:
        rolled = pltpu.roll(cur, k, axis=0)
        first = jnp.where(row8 < k, pltpu.roll(halo, k, axis=0), rolled[0:SUBLANES])
        return jnp.concatenate([first, rolled[SUBLANES:]], axis=0)
    rolled = pltpu.roll(cur, tb - k, axis=0)
    last = jnp.where(row8 >= SUBLANES - k, pltpu.roll(halo, SUBLANES - k, axis=0),
                     rolled[tb - SUBLANES:])
    return jnp.concatenate([rolled[:tb - SUBLANES], last], axis=0)


def _rglru_kernel(u_ref, h0_ref, cw_ref, cb_ref, wa_ref, ba_ref, wx_ref, bx_ref, lam_ref, *rest,
                  reverse, has_acc):
    if has_acc:
        acc_ref, out_ref, hlast_ref, a_s, b_s, hp_s, halo_s = rest
    else:
        out_ref, hlast_ref, a_s, b_s, hp_s, halo_s = rest
    tb, c = u_ref.shape

    @pl.when(pl.program_id(1) == 0)
    def _():
        hp_s[...] = h0_ref[...]
        halo_s[...] = jnp.zeros_like(halo_s)

    cur = u_ref[...]
    halo = halo_s[...]
    xc = cb_ref[...] + cw_ref[RG_CONV - 1:RG_CONV, :] * cur
    for k in range(1, RG_CONV):
        xc = xc + cw_ref[RG_CONV - 1 - k:RG_CONV - k, :] * _shift_rows(cur, halo, k, reverse)
    halo_s[...] = cur[0:SUBLANES] if reverse else cur[tb - SUBLANES:]

    xb = xc.astype(BF16)
    gate_r = _sigmoid(jnp.dot(xb, wa_ref[...], preferred_element_type=F32) + ba_ref[...])
    gate_i = _sigmoid(jnp.dot(xb, wx_ref[...], preferred_element_type=F32) + bx_ref[...])
    lam = lam_ref[...]
    softplus_neg = jnp.maximum(-lam, 0.0) + jnp.log(1.0 + jnp.exp(-jnp.abs(lam)))
    a = jnp.exp(-RG_C * gate_r * softplus_neg)
    a_s[...] = a
    b_s[...] = jnp.sqrt(1.0 - a * a) * (gate_i * xc)

    row8 = lax.broadcasted_iota(jnp.int32, (SUBLANES, c), 0)
    nt = tb // SUBLANES

    def step(j, hp):
        jj = nt - 1 - j if reverse else j
        r0 = pl.multiple_of(jj * SUBLANES, SUBLANES)
        av = a_s[pl.ds(r0, SUBLANES), :]
        bv = b_s[pl.ds(r0, SUBLANES), :]
        for k in (1, 2, 4):
            if reverse:
                ok = row8 < SUBLANES - k
                sh = SUBLANES - k
            else:
                ok = row8 >= k
                sh = k
            a_prev = jnp.where(ok, pltpu.roll(av, sh, axis=0), 1.0)
            b_prev = jnp.where(ok, pltpu.roll(bv, sh, axis=0), 0.0)
            bv = av * b_prev + bv
            av = av * a_prev
        h = av * hp + bv
        if has_acc:
            out_ref[pl.ds(r0, SUBLANES), :] = h + acc_ref[pl.ds(r0, SUBLANES), :]
        else:
            out_ref[pl.ds(r0, SUBLANES), :] = h
        return h[0:1] if reverse else h[SUBLANES - 1:SUBLANES]

    hp = lax.fori_loop(0, nt, step, hp_s[...])
    hp_s[...] = hp
    hlast_ref[...] = hp


def rglru_scan(u, h0, p, reverse, acc=None, tb=512):
    b, n, c = u.shape
    tb = min(tb, n)
    assert n % tb == 0
    nblk = n // tb
    if reverse:
        seq = pl.BlockSpec((None, tb, c), lambda bi, i: (bi, nblk - 1 - i, 0))
    else:
        seq = pl.BlockSpec((None, tb, c), lambda bi, i: (bi, i, 0))
    state = pl.BlockSpec((None, 1, c), lambda bi, i: (bi, 0, 0))

    def par(a):
        return pl.BlockSpec(a.shape, lambda bi, i: (0,) * a.ndim)

    params = [p["conv_w"], p["conv_b"], p["wa"], p["ba"], p["wx"], p["bx"], p["lam"]]
    in_specs = [seq, state] + [par(a) for a in params]
    args = [u, h0] + params
    if acc is not None:
        in_specs.append(seq)
        args.append(acc)
    return pl.pallas_call(
        functools.partial(_rglru_kernel, reverse=reverse, has_acc=acc is not None),
        grid=(b, nblk),
        in_specs=in_specs,
        out_specs=[seq, state],
        out_shape=[jax.ShapeDtypeStruct((b, n, c), F32), jax.ShapeDtypeStruct((b, 1, c), F32)],
        scratch_shapes=[pltpu.VMEM((tb, c), F32), pltpu.VMEM((tb, c), F32),
                        pltpu.VMEM((1, c), F32), pltpu.VMEM((SUBLANES, c), F32)],
        compiler_params=_cparams(2),
        name="rglru_bwd" if reverse else "rglru_fwd",
    )(*args)


def _block_diag(w):
    g, bs, _ = w.shape
    eye = jnp.eye(g, dtype=w.dtype)
    return (eye[:, None, :, None] * w[:, :, None, :]).reshape(g * bs, g * bs)


def rglru_params(conv_w, conv_b, wa, ba, wx, bx, lam, d):
    c = conv_b.shape[-1]
    return {"conv_w": conv_w[d], "conv_b": conv_b[d].reshape(1, c),
            "wa": _block_diag(wa[d]).astype(BF16), "ba": ba[d].reshape(1, c),
            "wx": _block_diag(wx[d]).astype(BF16), "bx": bx[d].reshape(1, c),
            "lam": lam[d].reshape(1, c)}


def rglru_mix(u_ctx, u_lat, params):
    b, _, c = u_ctx.shape
    zero = jnp.zeros((b, 1, c), F32)
    h_ctx = h_lat = None
    for d in range(2):
        h_ctx, last = rglru_scan(u_ctx, zero, params[d], reverse=bool(d), acc=h_ctx)
        h_lat, _ = rglru_scan(u_lat, last, params[d], reverse=bool(d), acc=h_lat)
    return h_ctx, h_lat


def _gla_kernel(x_ref, s0_ref, wlr_ref, blr_ref, *rest, reverse, has_acc):
    if has_acc:
        acc_ref, o_ref, s_out_ref, st_s = rest
    else:
        o_ref, s_out_ref, st_s = rest
    nb, tb = x_ref.shape[0], x_ref.shape[1]
    ch = GLA_CHUNK

    @pl.when(pl.program_id(1) == 0)
    def _():
        st_s[...] = s0_ref[...]

    r_i = lax.broadcasted_iota(jnp.int32, (ch, ch), 0)
    c_i = lax.broadcasted_iota(jnp.int32, (ch, ch), 1)
    if reverse:
        cum_mat = (c_i >= r_i).astype(BF16)
        keep = c_i > r_i
    else:
        cum_mat = (c_i <= r_i).astype(BF16)
        keep = c_i <= r_i
    lane = lax.broadcasted_iota(jnp.int32, (1, LANES), 1)
    head_lanes = (lane < GLA_DK, lane >= GLA_DK)
    nt_dims = (((1,), (1,)), ((), ()))
    chunks = range(tb // ch)
    steps = [(cidx, bb) for cidx in (reversed(chunks) if reverse else chunks) for bb in range(nb)]
    for cidx, bb in steps:
        r0 = cidx * ch
        q = x_ref[bb, r0:r0 + ch, 0:GLA_QK] * (GLA_DK ** -0.5)
        k = x_ref[bb, r0:r0 + ch, GLA_QK:2 * GLA_QK]
        v = x_ref[bb, r0:r0 + ch, 2 * GLA_QK:2 * GLA_QK + GLA_V]
        lr = x_ref[bb, r0:r0 + ch, 2 * GLA_QK + GLA_V:GLA_IN_PAD]
        logits = jnp.dot(lr, wlr_ref[...], precision=HIGHEST, preferred_element_type=F32) + blr_ref[...]
        log_a = _log_sigmoid(logits) * (1.0 / GLA_TAU)
        la_hi = log_a.astype(BF16)
        la_mid, la_lo = _split_bf16(log_a - la_hi.astype(F32))
        cum = (jnp.dot(cum_mat, la_hi, preferred_element_type=F32)
               + (jnp.dot(cum_mat, la_mid, preferred_element_type=F32)
                  + jnp.dot(cum_mat, la_lo, preferred_element_type=F32)))
        tot = cum[0:1] if reverse else cum[ch - 1:ch]
        qg = q * jnp.exp(cum)
        kg = k * jnp.exp(-cum)
        kd = k * jnp.exp(tot - cum)
        decay = jnp.exp(tot)
        outs = []
        for h in range(GLA_HEADS):
            sl = slice((h // 2) * LANES, (h // 2 + 1) * LANES)
            mine = head_lanes[h % 2]
            qm = jnp.where(mine, qg[:, sl], 0.0).astype(BF16)
            scores = lax.dot_general(qm, kg[:, sl].astype(BF16), nt_dims, preferred_element_type=F32)
            scores = jnp.where(keep, scores, 0.0)
            vh = v[:, h * GLA_DV:(h + 1) * GLA_DV]
            st = st_s[bb, h]
            o = jnp.dot(scores.astype(BF16), vh.astype(BF16), preferred_element_type=F32)
            o = o + lax.dot_general(qm, st.astype(BF16), nt_dims, preferred_element_type=F32)
            kdm = jnp.where(mine, kd[:, sl], 0.0).astype(BF16)
            st_s[bb, h] = st * decay[:, sl] + jnp.dot(vh.T.astype(BF16), kdm, preferred_element_type=F32)
            outs.append(o)
        o_all = jnp.concatenate(outs, axis=1)
        if has_acc:
            o_all = o_all + acc_ref[bb, r0:r0 + ch, :]
        o_ref[bb, r0:r0 + ch, :] = o_all
    s_out_ref[...] = st_s[...]


def gla_scan(x, s0, wlr, blr, reverse, acc=None, tb=512, nb=2):
    b, n, cin = x.shape
    tb = min(tb, n)
    nb = min(nb, b)
    assert n % tb == 0 and tb % GLA_CHUNK == 0 and b % nb == 0
    nblk = n // tb

    def seq(width):
        if reverse:
            return pl.BlockSpec((nb, tb, width), lambda bi, i: (bi, nblk - 1 - i, 0))
        return pl.BlockSpec((nb, tb, width), lambda bi, i: (bi, i, 0))

    state = pl.BlockSpec((nb, GLA_HEADS, GLA_DV, LANES), lambda bi, i: (bi, 0, 0, 0))
    in_specs = [seq(cin), state,
                pl.BlockSpec(wlr.shape, lambda bi, i: (0, 0)), pl.BlockSpec(blr.shape, lambda bi, i: (0, 0))]
    args = [x, s0, wlr, blr]
    if acc is not None:
        in_specs.append(seq(GLA_V))
        args.append(acc)
    return pl.pallas_call(
        functools.partial(_gla_kernel, reverse=reverse, has_acc=acc is not None),
        grid=(b // nb, nblk),
        in_specs=in_specs,
        out_specs=[seq(GLA_V), state],
        out_shape=[jax.ShapeDtypeStruct((b, n, GLA_V), F32),
                   jax.ShapeDtypeStruct((b, GLA_HEADS, GLA_DV, LANES), F32)],
        scratch_shapes=[pltpu.VMEM((nb, GLA_HEADS, GLA_DV, LANES), F32)],
        compiler_params=_cparams(2, 40 << 20),
        name="gla_bwd" if reverse else "gla_fwd",
    )(*args)


def gla_params(w_lr, b_lr, d):
    w = jnp.zeros((LANES, GLA_QK), F32).at[d * GLA_RANK:(d + 1) * GLA_RANK].set(w_lr[d])
    return w, b_lr[d].reshape(1, GLA_QK)


def gla_mix(x_ctx, x_lat_cols, w_lr, b_lr):
    b = x_ctx.shape[0]
    zero = jnp.zeros((b, GLA_HEADS, GLA_DV, LANES), F32)
    o_ctx = o_lat = None
    for d in range(2):
        w, bias = gla_params(w_lr, b_lr, d)
        o_ctx, s = gla_scan(x_ctx, zero, w, bias, reverse=bool(d), acc=o_ctx)
        o_lat, _ = gla_scan(x_lat_cols, s, w, bias, reverse=bool(d), acc=o_lat)
    return o_ctx, o_lat


def raster_to_column(t):
    b, n = t.shape[:2]
    return t.reshape(b, n // GRID_W, GRID_W, *t.shape[2:]).swapaxes(1, 2).reshape(t.shape)


def column_to_raster(t):
    b, n = t.shape[:2]
    return t.reshape(b, GRID_W, n // GRID_W, *t.shape[2:]).swapaxes(1, 2).reshape(t.shape)


def _merge_kernel(x_ref, gt_ref, h_ref, yhy_ref, hrg_ref, ogla_ref,
                  wg_ref, bm_ref, gn_ref, why_ref, wrg_ref, wgla_ref, wout_ref, o_ref):
    d = x_ref.shape[1]

    def proj(y, w_ref):
        return jnp.dot(y.astype(BF16), w_ref[...], preferred_element_type=F32)

    gates_in = proj(h_ref[...], wg_ref)
    y_rg = hrg_ref[...] * _gelu_tanh(gates_in[:, 0:RG_W])
    o = ogla_ref[...]
    heads = []
    for h in range(GLA_HEADS):
        oh = o[:, h * GLA_DV:(h + 1) * GLA_DV]
        heads.append(oh * lax.rsqrt(jnp.mean(oh * oh, axis=-1, keepdims=True) + NORM_EPS) * gn_ref[...])
    gg = gates_in[:, RG_W:RG_W + GLA_V]
    y_gla = jnp.concatenate(heads, axis=1) * (gg * _sigmoid(gg))
    gate = _sigmoid(gates_in[:, RG_W + GLA_V:] + bm_ref[...])
    m = (gate[:, 0:d] * proj(yhy_ref[...], why_ref) + gate[:, d:2 * d] * proj(y_rg, wrg_ref)
         + gate[:, 2 * d:3 * d] * proj(y_gla, wgla_ref))
    o_ref[...] = x_ref[...] + gt_ref[...] * proj(m, wout_ref)


def merge_residual(x, gt, h, y_hy, h_rg, o_gla, lw, tm=256):
    b, n, d = x.shape
    tm = min(tm, n)
    assert n % tm == 0

    def seq(a):
        return pl.BlockSpec((None, tm, a.shape[2]), lambda bi, i: (bi, i, 0))

    def par(a):
        return pl.BlockSpec(a.shape, lambda bi, i: (0,) * a.ndim, pipeline_mode=pl.Buffered(1))

    streams = [h, y_hy, h_rg, o_gla]
    params = [lw["w_gates"], lw["b_merge"], lw["gla_norm_g"], lw["w_hy_o"], lw["w_rg_o"],
              lw["w_gla_o"], lw["w_out"]]
    return pl.pallas_call(
        _merge_kernel,
        grid=(b, n // tm),
        in_specs=[seq(x), pl.BlockSpec((None, 1, d), lambda bi, i: (bi, 0, 0))]
                 + [seq(a) for a in streams] + [par(a) for a in params],
        out_specs=seq(x),
        out_shape=jax.ShapeDtypeStruct((b, n, d), F32),
        compiler_params=_cparams(2, 48 << 20),
        name="merge_residual",
    )(x, gt, *streams, *params)


def _shortconv_kernel(x_ref, w_ref, b_ref, o_ref, *, rows):
    n, c = x_ref.shape
    zero = jnp.zeros((SUBLANES, c), F32)
    for r0 in range(0, n, rows):
        cur = x_ref[r0:r0 + rows, :]
        before = x_ref[r0 - SUBLANES:r0, :] if r0 > 0 else zero
        after = x_ref[r0 + rows:r0 + rows + SUBLANES, :] if r0 + rows < n else zero
        o_ref[r0:r0 + rows, :] = (b_ref[...] + w_ref[0:1, :] * _shift_rows(cur, before, 1, False)
                                  + w_ref[1:2, :] * cur + w_ref[2:3, :] * _shift_rows(cur, after, 1, True))


def shortconv(x, w, bias):
    b, n, c = x.shape
    blk = pl.BlockSpec((None, n, LANES), lambda bi, j: (bi, 0, j))
    return pl.pallas_call(
        functools.partial(_shortconv_kernel, rows=min(n, 1024)),
        grid=(b, c // LANES),
        in_specs=[blk, pl.BlockSpec((3, LANES), lambda bi, j: (0, j)),
                  pl.BlockSpec((1, LANES), lambda bi, j: (0, j))],
        out_specs=blk,
        out_shape=jax.ShapeDtypeStruct((b, n, c), F32),
        compiler_params=_cparams(2, 40 << 20),
        name="shortconv",
    )(x, w, bias.reshape(1, c))


def _hy_taps_kernel(w1_ref, b1_ref, w2_ref, b2_ref, w3_ref, freq_ref, band_ref, delta_ref,
                    taps_ref, asum_ref, *, n):
    tb = taps_ref.shape[1]
    i = pl.program_id(0)
    m = i * tb + lax.broadcasted_iota(jnp.int32, (tb, LANES), 0)
    idx = jnp.where(m < n, m, 2 * n - m).astype(F32)
    lane = lax.broadcasted_iota(jnp.int32, (tb, LANES), 1)
    tn = idx / (n - 1)
    ang = (2.0 * math.pi / n) * idx * band_ref[...]
    feats = jnp.where(lane == 0, tn,
                      jnp.where(lane <= HY_BANDS, jnp.cos(ang),
                                jnp.where(lane <= 2 * HY_BANDS, -jnp.sin(ang), 0.0)))
    fr = freq_ref[...]

    def dense(x, w_ref):
        return jnp.dot(x, w_ref[...], precision=HIGHEST, preferred_element_type=F32)

    h = jnp.sin(fr * (dense(feats, w1_ref) + b1_ref[...]))
    h = jnp.sin(fr * (dense(h, w2_ref) + b2_ref[...]))
    h = dense(h, w3_ref) * jnp.exp(-tn[:, 0:1] * delta_ref[...])
    row = m[:, 0:1]
    use_fwd = row < n
    use_bwd = (row > n) | (row == 0)
    sums = []
    for o in range(2):
        hf = jnp.where(use_fwd, h[:, (2 * o) * HY_W:(2 * o + 1) * HY_W], 0.0)
        hb = jnp.where(use_bwd, h[:, (2 * o + 1) * HY_W:(2 * o + 2) * HY_W], 0.0)
        taps_ref[o] = hf + hb
        sums += [jnp.sum(jnp.abs(hf), axis=0, keepdims=True), jnp.sum(jnp.abs(hb), axis=0, keepdims=True)]

    @pl.when(i == 0)
    def _():
        asum_ref[...] = jnp.zeros_like(asum_ref)

    asum_ref[...] += jnp.concatenate(sums, axis=1)


def hyena_taps(n, w1, b1, w2, b2, w3, freq, tb=256):
    nout = w3.shape[1]
    assert nout == 4 * HY_W and (2 * n) % tb == 0
    bands = np.zeros((1, LANES), np.float32)
    lin = np.linspace(1e-4, HY_BANDS - 1, HY_BANDS, dtype=np.float32)
    bands[0, 1:1 + HY_BANDS] = lin
    bands[0, 1 + HY_BANDS:1 + 2 * HY_BANDS] = lin
    deltas = np.abs(np.linspace(HY_FAST_RATE, HY_SLOW_RATE, HY_W, dtype=np.float32))
    deltas = np.tile(deltas, nout // HY_W).reshape(1, nout)
    w1p = jnp.zeros((LANES, HY_FFN), F32).at[:w1.shape[0]].set(w1)
    params = [w1p, b1.reshape(1, -1), w2, b2.reshape(1, -1), w3, freq.reshape(1, -1),
              jnp.asarray(bands), jnp.asarray(deltas)]
    return pl.pallas_call(
        functools.partial(_hy_taps_kernel, n=n),
        grid=(2 * n // tb,),
        in_specs=[pl.BlockSpec(a.shape, lambda i: (0, 0)) for a in params],
        out_specs=[pl.BlockSpec((2, tb, HY_W), lambda i: (0, i, 0)), pl.BlockSpec((1, nout), lambda i: (0, 0))],
        out_shape=[jax.ShapeDtypeStruct((2, 2 * n, HY_W), F32), jax.ShapeDtypeStruct((1, nout), F32)],
        compiler_params=_cparams(1),
        name="hyena_taps",
    )(*params)


def _cis(num, den):
    ang = (2.0 * math.pi / den) * (num % den).astype(F32)
    return jnp.cos(ang), jnp.sin(ang)


def _dft_tables(n):
    big = 2 * n
    q = int(round(math.sqrt(big)))
    assert q * q == big and q % (2 * SUBLANES) == 0
    ar = jnp.arange(q, dtype=jnp.int32)
    num = ar[None, :, None] * (q * ar[None, None, :] + ar[:, None, None])
    c, s = _cis(num, big)
    w1 = jnp.concatenate([c, -s], axis=1)
    ct, st = jnp.swapaxes(c, 1, 2)[:, :q // 2], jnp.swapaxes(s, 1, 2)[:, :q // 2]
    v = jnp.concatenate([ct, -st], axis=2) * (1.0 / big)
    c2, s2 = _cis(ar[:, None] * ar[None, :], q)
    f2 = jnp.concatenate([jnp.concatenate([c2, s2], axis=1), jnp.concatenate([-s2, c2], axis=1)], axis=0)
    g2 = jnp.concatenate([jnp.concatenate([c2, -s2], axis=1), jnp.concatenate([s2, c2], axis=1)], axis=0)
    return {"q": q, "w1": w1, "v": v, "f2": f2, "g2": g2}


def _level_kernel(w_ref, x_ref, *rest, has_gate, precise):
    if has_gate:
        src_ref, gate_ref, skip_ref = rest[:3]
    o_ref = rest[-1]
    for s in range(w_ref.shape[0]):
        x = x_ref[s]
        if precise:
            y = jnp.dot(w_ref[s], x, precision=HIGHEST, preferred_element_type=F32)
        else:
            y = jnp.dot(w_ref[s], x.astype(BF16), preferred_element_type=F32)
        if has_gate:
            y = gate_ref[s] * (y + src_ref[s] * skip_ref[...])
        o_ref[s] = y.astype(o_ref.dtype)


def dft_level(w, x, x_group=0, gate=None, out_dtype=F32, precise=False, sb=8):
    b, q, k, _ = x.shape
    m = w.shape[1]
    c = HY_W
    sb = min(sb, q)

    def seq(rows, group):
        return pl.BlockSpec((None, sb, rows, c), lambda i, bi: (bi, i, 0, group))

    in_specs = [pl.BlockSpec((sb, m, k), lambda i, bi: (i, 0, 0)), seq(k, x_group)]
    args = [w, x]
    if gate is not None:
        src, src_group, gates, gate_group, skip = gate
        in_specs += [seq(m, src_group), seq(m, gate_group), pl.BlockSpec((1, c), lambda i, bi: (0, 0))]
        args += [src, gates, skip]
    return pl.pallas_call(
        functools.partial(_level_kernel, has_gate=gate is not None, precise=precise),
        grid=(q // sb, b),
        in_specs=in_specs,
        out_specs=seq(m, 0),
        out_shape=jax.ShapeDtypeStruct((b, q, m, c), out_dtype),
        compiler_params=_cparams(2, 40 << 20),
        name="dft_level",
    )(*args)


def _dft_mid_kernel(a_ref, h_ref, f_ref, g_ref, o_ref):
    q = a_ref.shape[1] // 2
    for j in range(a_ref.shape[0]):
        x = jnp.dot(f_ref[...], a_ref[j], preferred_element_type=F32)
        xr, xi = x[:q], x[q:]
        hr, hi = h_ref[j, :q, :], h_ref[j, q:, :]
        y = jnp.concatenate([xr * hr - xi * hi, xr * hi + xi * hr], axis=0)
        o_ref[j] = jnp.dot(g_ref[...], y.astype(BF16), preferred_element_type=F32).astype(o_ref.dtype)


def dft_mid(a, spec, f2, g2, kb=8):
    b, q, q2, c = a.shape
    blk = pl.BlockSpec((None, kb, q2, c), lambda i, bi: (bi, i, 0, 0))
    mat = pl.BlockSpec((q2, q2), lambda i, bi: (0, 0))
    return pl.pallas_call(
        _dft_mid_kernel,
        grid=(q // kb, b),
        in_specs=[blk, pl.BlockSpec((kb, q2, c), lambda i, bi: (i, 0, 0)), mat, mat],
        out_specs=blk,
        out_shape=jax.ShapeDtypeStruct(a.shape, BF16),
        compiler_params=_cparams(2, 40 << 20),
        name="dft_mid",
    )(a, spec, f2, g2)


def _dft_spec_kernel(a_ref, f_ref, asum_ref, o_ref):
    inv = 1.0 / (asum_ref[0:1, :] + asum_ref[1:2, :] + 1e-6)
    for j in range(a_ref.shape[0]):
        o_ref[j] = jnp.dot(f_ref[...], a_ref[j], precision=HIGHEST, preferred_element_type=F32) * inv


def dft_spec(a, f2, asum, kb=8):
    o, q, q2, c = a.shape
    blk = pl.BlockSpec((None, kb, q2, c), lambda i, oi: (oi, i, 0, 0))
    return pl.pallas_call(
        _dft_spec_kernel,
        grid=(q // kb, o),
        in_specs=[blk, pl.BlockSpec((q2, q2), lambda i, oi: (0, 0)),
                  pl.BlockSpec((None, 2, c), lambda i, oi: (oi, 0, 0))],
        out_specs=blk,
        out_shape=jax.ShapeDtypeStruct(a.shape, F32),
        compiler_params=_cparams(2, 40 << 20),
        name="dft_spec",
    )(a, f2, asum)


def _swap_levels(a):
    b, q, q2, c = a.shape
    return a.reshape(b, q, 2, q, c).transpose(0, 3, 2, 1, 4).reshape(b, q, q2, c)


def hyena_long(u, taps, asum, skip, tabs):
    b, n, _ = u.shape
    q = tabs["q"]
    c = HY_W

    def to_levels(t, rows):
        return t.reshape(t.shape[0], rows, q, t.shape[2]).swapaxes(1, 2)

    a = dft_level(tabs["w1"], to_levels(taps, q), precise=True)
    spec = dft_spec(_swap_levels(a), tabs["f2"], asum.reshape(2, 2, c))
    w1d = tabs["w1"][:, :, :q // 2].astype(BF16)
    vd = tabs["v"].astype(BF16)
    f2, g2 = tabs["f2"].astype(BF16), tabs["g2"].astype(BF16)
    u_t = to_levels(u, q // 2)
    src, group = u_t, 0
    for order in range(2):
        a = dft_level(w1d, src, x_group=group, out_dtype=BF16)
        cm = dft_mid(_swap_levels(a), spec[order], f2, g2)
        src = dft_level(vd, _swap_levels(cm),
                        gate=(src, group, u_t, order + 1, skip[order].reshape(1, c)))
        group = 0
    return src.swapaxes(1, 2).reshape(b, n, c)


def _dft_small_kernel(u_ref, gate_ref, skip_ref, h_ref, f_ref, g_ref, o_ref):
    u = u_ref[...]
    nb = h_ref.shape[0] // 2
    x = jnp.dot(f_ref[...], u, precision=HIGHEST, preferred_element_type=F32)
    xr, xi = x[:nb], x[nb:]
    hr, hi = h_ref[:nb, :], h_ref[nb:, :]
    y = jnp.concatenate([xr * hr - xi * hi, xr * hi + xi * hr], axis=0)
    conv = jnp.dot(g_ref[...], y, precision=HIGHEST, preferred_element_type=F32)
    o_ref[...] = gate_ref[...] * (conv + u * skip_ref[...])


def _dft_small_spec_kernel(taps_ref, f_ref, asum_ref, o_ref):
    inv = 1.0 / (asum_ref[0:1, :] + asum_ref[1:2, :] + 1e-6)
    o_ref[...] = jnp.dot(f_ref[...], taps_ref[...], precision=HIGHEST,
                         preferred_element_type=F32) * inv


def hyena_short_seq(u, taps, asum, skip):
    b, n, _ = u.shape
    big = 2 * n
    c = HY_W
    ar = jnp.arange(big, dtype=jnp.int32)
    cs, sn = _cis(ar[:, None] * ar[None, :], big)
    f_full = jnp.concatenate([cs, -sn], axis=0)
    g_half = jnp.concatenate([cs[:n], -sn[:n]], axis=1) * (1.0 / big)
    tap = pl.BlockSpec((None, big, c), lambda o: (o, 0, 0))
    spec = pl.pallas_call(
        _dft_small_spec_kernel,
        grid=(2,),
        in_specs=[tap, pl.BlockSpec((2 * big, big), lambda o: (0, 0)),
                  pl.BlockSpec((None, 2, c), lambda o: (o, 0, 0))],
        out_specs=pl.BlockSpec((None, 2 * big, c), lambda o: (o, 0, 0)),
        out_shape=jax.ShapeDtypeStruct((2, 2 * big, c), F32),
        compiler_params=_cparams(1),
        name="dft_small_spec",
    )(taps, f_full, asum.reshape(2, 2, c))
    f_data = f_full[:, :n]
    src, group = u, 0
    for order in range(2):
        src = pl.pallas_call(
            _dft_small_kernel,
            grid=(b,),
            in_specs=[pl.BlockSpec((None, n, c), functools.partial(lambda bi, g: (bi, 0, g), g=group)),
                      pl.BlockSpec((None, n, c), functools.partial(lambda bi, g: (bi, 0, g), g=order + 1)),
                      pl.BlockSpec((1, c), lambda bi: (0, 0)),
                      pl.BlockSpec((2 * big, c), lambda bi: (0, 0)),
                      pl.BlockSpec((2 * big, n), lambda bi: (0, 0)),
                      pl.BlockSpec((n, 2 * big), lambda bi: (0, 0))],
            out_specs=pl.BlockSpec((None, n, c), lambda bi: (bi, 0, 0)),
            out_shape=jax.ShapeDtypeStruct((b, n, c), F32),
            compiler_params=_cparams(1),
            name="dft_small",
        )(src, u, skip[order].reshape(1, c), spec[order], f_data, g_half)
        group = 0
    return src


HY_IN = 3 * HY_W
IN_GROUPS = (HY_IN, RG_W, RG_W, GLA_IN, GLA_V, 3 * D_MODEL)


def _split_w_in(w):
    parts, start = [], 0
    for width in IN_GROUPS:
        parts.append(w[:, start:start + width])
        start += width
    parts[3] = jnp.pad(parts[3], ((0, 0), (0, GLA_IN_PAD - GLA_IN)))
    return [p.astype(BF16) for p in parts]


def _project(h, w):
    b, n, d = h.shape
    return matmul(h.reshape(b * n, d), w).reshape(b, n, w.shape[1])


def kernel(x, c, ctx, c_ctx, w_mod, b_mod, g_norm_mix, g_norm_ffn, w_in, hy_conv_w, hy_conv_b,
           hy_w1, hy_b1, hy_w2, hy_b2, hy_w3, hy_freq, hy_skip, rg_conv_w, rg_conv_b, rg_wa, rg_ba,
           rg_wx, rg_bx, rg_lambda, gla_w_lr, gla_b_lr, gla_norm_g, w_hy_o, w_rg_o, w_gla_o, b_merge,
           w_out, peer_wq, peer_keys, peer_u, peer_v, g_final):
    b, n, d = x.shape
    n_ctx = ctx.shape[1]
    depth = w_mod.shape[0]
    cond = jnp.concatenate([c, c_ctx[None, :]], axis=0)
    cond = jnp.pad(cond, ((0, -(b + 1) % SUBLANES), (0, 0)))
    tabs = _dft_tables(n)
    x_lat, x_ctx = x, ctx
    for l in range(depth):
        need_ctx = l < depth - 1
        mod = matmul(cond, w_mod[l].astype(BF16), bias=b_mod[l], silu_in=True)
        sh1, sc1, gt1, sh2, sc2, gt2 = [mod[:b, i * d:(i + 1) * d].reshape(b, 1, d) for i in range(6)]
        csh1, csc1, cgt1, csh2, csc2, cgt2 = [
            jnp.broadcast_to(mod[b:b + 1, i * d:(i + 1) * d].reshape(1, 1, d), (b, 1, d)) for i in range(6)]
        w_hy, w_rgx, w_rgg, w_gla, w_glag, w_mg = _split_w_in(w_in[l])
        lw = {"b_merge": b_merge[l].reshape(1, 3 * d), "gla_norm_g": gla_norm_g[l].reshape(1, GLA_DV),
              "w_gates": jnp.concatenate([w_rgg, w_glag, w_mg], axis=1),
              "w_hy_o": w_hy_o[l].astype(BF16), "w_rg_o": w_rg_o[l].astype(BF16),
              "w_gla_o": w_gla_o[l].astype(BF16), "w_out": w_out[l].astype(BF16)}
        rg_par = [rglru_params(rg_conv_w[l], rg_conv_b[l], rg_wa[l], rg_ba[l], rg_wx[l], rg_bx[l],
                               rg_lambda[l], dd) for dd in range(2)]
        filt = (hy_w1[l], hy_b1[l], hy_w2[l], hy_b2[l], hy_w3[l], hy_freq[l])
        u_pack, v_pack = _pack_table(peer_u[l]), _pack_table(peer_v[l])

        h_lat = normmod(x_lat, g_norm_mix[l], sh1, sc1, BF16)
        h_ctx = normmod(x_ctx, g_norm_mix[l], csh1, csc1, BF16)
        hy_l, rgx_l, gla_l = [_project(h_lat, w) for w in (w_hy, w_rgx, w_gla)]
        rgx_c, gla_c = [_project(h_ctx, w) for w in (w_rgx, w_gla)]

        taps, asum = hyena_taps(n, *filt)
        y_hy_l = hyena_long(shortconv(hy_l, hy_conv_w[l], hy_conv_b[l]), taps, asum, hy_skip[l], tabs)
        h_rg_c, h_rg_l = rglru_mix(rgx_c, rgx_l, rg_par)
        o_gla_c, o_gla_l = gla_mix(gla_c, raster_to_column(gla_l), gla_w_lr[l], gla_b_lr[l])
        x_lat = merge_residual(x_lat, gt1, h_lat, y_hy_l, h_rg_l, column_to_raster(o_gla_l), lw)
        h2 = normmod(x_lat, g_norm_ffn[l], sh2, sc2, F32)
        x_lat = peer_residual(x_lat, gt2, h2, peer_wq[l], peer_keys[l], u_pack, v_pack)
        if need_ctx:
            taps_c, asum_c = hyena_taps(n_ctx, *filt)
            y_hy_c = hyena_short_seq(shortconv(_project(h_ctx, w_hy), hy_conv_w[l], hy_conv_b[l]),
                                     taps_c, asum_c, hy_skip[l])
            x_ctx = merge_residual(x_ctx, cgt1, h_ctx, y_hy_c, h_rg_c, o_gla_c, lw)
            h2c = normmod(x_ctx, g_norm_ffn[l], csh2, csc2, F32)
            x_ctx = peer_residual(x_ctx, cgt2, h2c, peer_wq[l], peer_keys[l], u_pack, v_pack)
    zero = jnp.zeros((b, 1, d), F32)
    return normmod(x_lat, g_final, zero, zero, F32, mod=False)
```

```python
import functools
import math

import numpy as np
import jax
import jax.numpy as jnp
from jax import lax
from jax.experimental import pallas as pl
from jax.experimental.pallas import tpu as pltpu

F32 = jnp.float32
BF16 = jnp.bfloat16
HIGHEST = lax.Precision.HIGHEST

D_MODEL = 1024
DEPTH = 4
GRID_W = 64
NORM_EPS = 1e-6

HY_W = 512
HY_BANDS = 16
HY_FFN = 64
HY_FAST_RATE = math.log(1e-2) / 0.3
HY_SLOW_RATE = math.log(1e-2) / 1.5

RG_W = 512
RG_BLOCKS = 8
RG_C = 8.0

GLA_HEADS = 4
GLA_DK = 64
GLA_DV = 128
GLA_QK = GLA_HEADS * GLA_DK
GLA_V = GLA_HEADS * GLA_DV
GLA_RANK = 16
GLA_TAU = 16.0
GLA_CHUNK = 64
GLA_IN = 2 * GLA_QK + GLA_V + 2 * GLA_RANK
GLA_IN_PAD = 2 * GLA_QK + GLA_V + 128

PEER_HEADS = 8
PEER_NKEYS = 128
PEER_TOPK = 16
PEER_HALF = 128
PEER_SEL = PEER_HEADS * PEER_TOPK

V7X_VMEM_BYTES = 64 * 1024 * 1024
SUBLANES = 8
LANES = 128


def _cparams(n_grid, vmem_bytes=None):
    kw = dict(dimension_semantics=("arbitrary",) * n_grid)
    if vmem_bytes is not None:
        assert vmem_bytes < V7X_VMEM_BYTES
        kw["vmem_limit_bytes"] = int(vmem_bytes)
    return pltpu.CompilerParams(**kw)


def _gelu_tanh(x):
    return 0.5 * x * (1.0 + jnp.tanh(math.sqrt(2.0 / math.pi) * (x + 0.044715 * (x * x * x))))


def _sigmoid(x):
    return 1.0 / (1.0 + jnp.exp(-x))


def _log_sigmoid(x):
    return jnp.minimum(x, 0.0) - jnp.log(1.0 + jnp.exp(-jnp.abs(x)))


def _mm_kernel(x_ref, w_ref, *rest, silu_in, has_bias):
    o_ref = rest[-1]
    x = x_ref[...]
    if silu_in:
        x = x * _sigmoid(x)
    acc = jnp.dot(x.astype(BF16), w_ref[...], preferred_element_type=F32)
    if has_bias:
        acc = acc + rest[0][...]
    o_ref[...] = acc.astype(o_ref.dtype)


def matmul(x, w, bias=None, silu_in=False, tm=512):
    m, k = x.shape
    n = w.shape[1]
    tm = min(tm, m)
    assert m % tm == 0
    in_specs = [pl.BlockSpec((tm, k), lambda i: (i, 0)), pl.BlockSpec((k, n), lambda i: (0, 0))]
    args = [x, w]
    if bias is not None:
        in_specs.append(pl.BlockSpec((1, n), lambda i: (0, 0)))
        args.append(bias.reshape(1, n))
    est = 2 * (tm * k * x.dtype.itemsize + k * n * 2 + tm * n * 4) + (4 << 20)
    return pl.pallas_call(
        functools.partial(_mm_kernel, silu_in=silu_in, has_bias=bias is not None),
        grid=(m // tm,),
        in_specs=in_specs,
        out_specs=pl.BlockSpec((tm, n), lambda i: (i, 0)),
        out_shape=jax.ShapeDtypeStruct((m, n), F32),
        compiler_params=_cparams(1, est),
        name="matmul",
    )(*args)


def _normmod_kernel(x_ref, g_ref, sh_ref, sc_ref, o_ref, *, mod):
    x = x_ref[...]
    y = x * lax.rsqrt(jnp.mean(x * x, axis=-1, keepdims=True) + NORM_EPS) * g_ref[...]
    if mod:
        y = y * (1.0 + sc_ref[...]) + sh_ref[...]
    o_ref[...] = y.astype(o_ref.dtype)


def normmod(x, g, shift, scale, out_dtype, mod=True, tm=512):
    b, l, d = x.shape
    tm = min(tm, l)
    assert l % tm == 0
    vec = pl.BlockSpec((None, 1, d), lambda bi, i: (bi, 0, 0))
    return pl.pallas_call(
        functools.partial(_normmod_kernel, mod=mod),
        grid=(b, l // tm),
        in_specs=[pl.BlockSpec((None, tm, d), lambda bi, i: (bi, i, 0)),
                  pl.BlockSpec((1, d), lambda bi, i: (0, 0)), vec, vec],
        out_specs=pl.BlockSpec((None, tm, d), lambda bi, i: (bi, i, 0)),
        out_shape=jax.ShapeDtypeStruct((b, l, d), out_dtype),
        compiler_params=_cparams(2),
        name="normmod",
    )(x, g.reshape(1, d), shift, scale)


def _assemble_rows(rows, n):
    t = rows[0].shape[1]
    rid = lax.broadcasted_iota(jnp.int32, (n, t), 0)
    out = jnp.zeros((n, t), rows[0].dtype)
    for r in range(n):
        out = jnp.where(rid == r, rows[r], out)
    return out


def _extract_topk(s, rowid, k):
    vals, ids = [], []
    for _ in range(k):
        m = jnp.max(s, axis=0, keepdims=True)
        first = jnp.min(jnp.where(s == m, rowid, 1e9), axis=0, keepdims=True)
        s = jnp.where(rowid == first, -jnp.inf, s)
        vals.append(m)
        ids.append(first)
    return vals, ids


def _extract_topk_pairs(s, k):
    rows, t = s.shape
    tile = SUBLANES
    assert rows % (2 * tile) == 0
    row8 = lax.broadcasted_iota(jnp.int32, (tile, t), 0).astype(F32)
    up, up_id, low, low_id = [], [], [], []
    for v in range(rows // (2 * tile)):
        a = s[2 * v * tile:(2 * v + 1) * tile]
        b = s[(2 * v + 1) * tile:(2 * v + 2) * tile]
        ida = row8 + float(2 * v * tile)
        idb = row8 + float((2 * v + 1) * tile)
        first_wins = a >= b
        up.append(jnp.where(first_wins, a, b))
        low.append(jnp.where(first_wins, b, a))
        up_id.append(jnp.where(first_wins, ida, idb))
        low_id.append(jnp.where(first_wins, idb, ida))
    up, up_id = jnp.concatenate(up, axis=0), jnp.concatenate(up_id, axis=0)
    low, low_id = jnp.concatenate(low, axis=0), jnp.concatenate(low_id, axis=0)
    vals, ids = [], []
    for _ in range(k):
        m = jnp.max(up, axis=0, keepdims=True)
        first = jnp.min(jnp.where(up == m, up_id, 1e9), axis=0, keepdims=True)
        hit = up_id == first
        up = jnp.where(hit, low, up)
        up_id = jnp.where(hit, low_id, up_id)
        low = jnp.where(hit, -jnp.inf, low)
        vals.append(m)
        ids.append(first)
    return vals, ids


def _split_bf16(x):
    hi = x.astype(BF16)
    return hi, (x - hi.astype(F32)).astype(BF16)


def _dot3(a_hi, a_lo, b_hi, b_lo, dims):
    def one(a, b):
        return lax.dot_general(a, b, dims, preferred_element_type=F32)
    return one(a_hi, b_hi) + (one(a_hi, b_lo) + one(a_lo, b_hi))


def _peer_select_kernel(h_ref, wqh_ref, wql_ref, kh_ref, kl_ref, exp_ref, wgt_ref, q_ref):
    h_hi, h_lo = _split_bf16(h_ref[...])
    q_ref[...] = _dot3(h_hi, h_lo, wqh_ref[...], wql_ref[...], (((1,), (0,)), ((), ())))

    def lane_block(j, carry):
        r0 = pl.multiple_of(j * LANES, LANES)
        rows, weights = _select_tokens(q_ref[pl.ds(r0, LANES), :], kh_ref, kl_ref)
        exp_ref[pl.ds(r0, LANES), :] = rows
        wgt_ref[pl.ds(r0, LANES), :] = weights
        return carry

    lax.fori_loop(0, h_ref.shape[0] // LANES, lane_block, 0)


def _select_tokens(q, kh_ref, kl_ref):
    tt = q.shape[0]
    k = PEER_TOPK
    row8 = lax.broadcasted_iota(jnp.int32, (SUBLANES, tt), 0)
    row16 = lax.broadcasted_iota(jnp.int32, (2 * SUBLANES, tt), 0)
    exp_blocks, wgt_blocks = [], []
    for h in range(PEER_HEADS):
        tops = []
        for p in range(2):
            col = (h * 2 + p) * PEER_HALF
            q_hi, q_lo = _split_bf16(q[:, col:col + PEER_HALF])
            s = _dot3(kh_ref[h, p], kl_ref[h, p], q_hi, q_lo, (((1,), (1,)), ((), ())))
            vals, ids = _extract_topk_pairs(s, k)
            tops.append((_assemble_rows(vals, k), _assemble_rows(ids, k)))
        (a, ia), (b, ib) = tops
        blocks, flat = [a[0:1] + b], [row16.astype(F32)]
        for i in range(1, 8):
            nj = k // (i + 1)
            blocks.append(jnp.where(row8 < nj, a[i:i + 1] + b[0:8], -jnp.inf))
            flat.append((row8 + i * k).astype(F32))
        blocks.append(a[8:16] + b[0:1])
        flat.append(((row8 + 8) * k).astype(F32))
        cand = jnp.concatenate(blocks, axis=0)
        cand_id = jnp.concatenate(flat, axis=0)
        vals, ids = _extract_topk(cand, cand_id, k)
        best = _assemble_rows(vals, k)
        fl = _assemble_rows(ids, k)
        fi = jnp.floor(fl * (1.0 / k))
        fj = fl - fi * k
        ei = jnp.zeros_like(fl)
        ej = jnp.zeros_like(fl)
        for r in range(k):
            ei = jnp.where(fi == r, ia[r:r + 1], ei)
            ej = jnp.where(fj == r, ib[r:r + 1], ej)
        e = jnp.exp(best - jnp.max(best, axis=0, keepdims=True))
        wgt_blocks.append(e / jnp.sum(e, axis=0, keepdims=True))
        exp_blocks.append((ei * PEER_NKEYS + ej) * ROW_WORDS)
    return (jnp.concatenate(exp_blocks, axis=0).T.astype(jnp.int32),
            jnp.concatenate(wgt_blocks, axis=0).T)


def peer_select(h, wq, keys, tt=512):
    t, d = h.shape
    tt = min(tt, t)
    assert t % tt == 0 and tt % LANES == 0
    nq = wq.shape[1]
    wq_hi, wq_lo = _split_bf16(wq)
    k_hi, k_lo = _split_bf16(keys)
    wspec = pl.BlockSpec((d, nq), lambda i: (0, 0))
    kspec = pl.BlockSpec(keys.shape, lambda i: (0, 0, 0, 0))
    return pl.pallas_call(
        _peer_select_kernel,
        grid=(t // tt,),
        in_specs=[pl.BlockSpec((tt, d), lambda i: (i, 0)), wspec, wspec, kspec, kspec],
        out_specs=[pl.BlockSpec((tt, PEER_SEL), lambda i: (i, 0)),
                   pl.BlockSpec((tt, PEER_SEL), lambda i: (i, 0))],
        out_shape=[jax.ShapeDtypeStruct((t, PEER_SEL), jnp.int32),
                   jax.ShapeDtypeStruct((t, PEER_SEL), F32)],
        scratch_shapes=[pltpu.VMEM((tt, nq), F32)],
        compiler_params=_cparams(1, 2 * (2 * d * nq + 2 * keys.size) * 2 + 3 * tt * (d + nq) * 4 + (8 << 20)),
        name="peer_select",
    )(h, wq_hi, wq_lo, k_hi, k_lo)


ROW_WORDS = D_MODEL // 2 // LANES


def _unpack_pair(w):
    lo = lax.bitcast_convert_type(lax.shift_left(w, jnp.int32(16)), F32)
    hi = lax.bitcast_convert_type(jnp.bitwise_and(w, jnp.int32(-65536)), F32)
    return lo, hi


PEER_CHUNK = 128
PEER_NCHUNK = PEER_SEL // PEER_CHUNK
PEER_GROUP = 8


def _rows_to_tiles(x8):
    row8 = lax.broadcasted_iota(jnp.int32, (SUBLANES, LANES), 0)
    tiles = []
    for tk in range(SUBLANES):
        tile = jnp.zeros((SUBLANES, LANES), x8.dtype)
        for r in range(SUBLANES):
            tile = jnp.where(row8 == r, x8[tk:tk + 1, r * LANES:(r + 1) * LANES], tile)
        tiles.append(tile)
    return tiles


def _tiles_to_rows(tiles):
    row8 = lax.broadcasted_iota(jnp.int32, (SUBLANES, LANES), 0)
    cols = []
    for r in range(SUBLANES):
        col = jnp.zeros((SUBLANES, LANES), tiles[0].dtype)
        for tk in range(SUBLANES):
            col = jnp.where(row8 == tk, tiles[tk][r:r + 1, :], col)
        cols.append(col)
    return jnp.concatenate(cols, axis=1)


def _peer_act_kernel(idx_ref, x_ref, wgt_ref, tab_ref, coef_ref, p_ref, xs_ref):
    tt = x_ref.shape[0]

    def group(g, carry):
        t0 = pl.multiple_of(g * PEER_GROUP, PEER_GROUP)
        for tk, tile in enumerate(_rows_to_tiles(x_ref[pl.ds(t0, PEER_GROUP), :])):
            xs_ref[tk] = tile

        def token(tk, c):
            xl = xs_ref[tk, 0:ROW_WORDS, :]
            xh = xs_ref[tk, ROW_WORDS:2 * ROW_WORDS, :]

            def chunk(ci, c2):
                base = ((t0 + tk) * PEER_NCHUNK + ci) * PEER_CHUNK
                for j in range(PEER_CHUNK):
                    r = pl.multiple_of(idx_ref[base + j], ROW_WORDS)
                    lo, hi = _unpack_pair(tab_ref[pl.ds(r, ROW_WORDS), :])
                    p_ref[tk * PEER_NCHUNK + ci, j * ROW_WORDS:(j + 1) * ROW_WORDS, :] = lo * xl + hi * xh
                return c2

            return lax.fori_loop(0, PEER_NCHUNK, chunk, c)

        lax.fori_loop(0, PEER_GROUP, token, 0)
        for tk in range(PEER_GROUP):
            tok = p_ref.at[tk * PEER_NCHUNK:(tk + 1) * PEER_NCHUNK]
            parts = [tok[:, pl.ds(r, PEER_CHUNK, stride=ROW_WORDS), :].reshape(PEER_SEL, LANES)
                     for r in range(ROW_WORDS)]
            per_lane = (parts[0] + parts[1]) + (parts[2] + parts[3])
            act = jnp.sum(per_lane.T, axis=0, keepdims=True)
            coef_ref[t0 + tk] = wgt_ref[t0 + tk] * _gelu_tanh(act)
        return carry

    lax.fori_loop(0, tt // PEER_GROUP, group, 0)


def _peer_out_kernel(idx_ref, coef_ref, res_ref, gate_ref, tab_ref, out_ref, ys_ref):
    tt = out_ref.shape[0]
    zero = jnp.zeros((ROW_WORDS, LANES), F32)

    def group(g, carry):
        t0 = pl.multiple_of(g * PEER_GROUP, PEER_GROUP)

        def token(tk, c):
            def chunk(ci, acc):
                acc = list(acc)
                base = ((t0 + tk) * PEER_NCHUNK + ci) * PEER_CHUNK
                for j in range(PEER_CHUNK):
                    r = pl.multiple_of(idx_ref[base + j], ROW_WORDS)
                    lo, hi = _unpack_pair(tab_ref[pl.ds(r, ROW_WORDS), :])
                    cf = coef_ref[base + j]
                    acc[2 * (j % 2)] = acc[2 * (j % 2)] + cf * lo
                    acc[2 * (j % 2) + 1] = acc[2 * (j % 2) + 1] + cf * hi
                return tuple(acc)

            acc = lax.fori_loop(0, PEER_NCHUNK, chunk, (zero, zero, zero, zero))
            ys_ref[tk, 0:ROW_WORDS, :] = acc[0] + acc[2]
            ys_ref[tk, ROW_WORDS:2 * ROW_WORDS, :] = acc[1] + acc[3]
            return c

        lax.fori_loop(0, PEER_GROUP, token, 0)
        rows = _tiles_to_rows([ys_ref[tk] for tk in range(PEER_GROUP)])
        out_ref[pl.ds(t0, PEER_GROUP), :] = res_ref[pl.ds(t0, PEER_GROUP), :] + gate_ref[...] * rows
        return carry

    lax.fori_loop(0, tt // PEER_GROUP, group, 0)


def _pack_table(tab):
    e, d = tab.shape
    pairs = jnp.moveaxis(tab.astype(BF16).reshape(e, 2, d // 2), 1, 2)
    return lax.bitcast_convert_type(pairs, jnp.int32).reshape(e * ROW_WORDS, LANES)


def _table_spec(shape):
    return pl.BlockSpec(shape, lambda i: (0, 0), pipeline_mode=pl.Buffered(1))


def peer_experts(h, expert, weight, u_pack, v_pack, res, gate, tokens_per_gate, tt=128):
    t, d = h.shape
    assert t % tt == 0 and tokens_per_gate % tt == 0 and tt % PEER_GROUP == 0
    assert PEER_GROUP == SUBLANES and d == 2 * ROW_WORDS * LANES
    vmem = u_pack.size * 4 + (16 << 20)
    smem_flat = pl.BlockSpec((tt * PEER_SEL,), lambda i: (i,), memory_space=pltpu.SMEM)
    xspec = pl.BlockSpec((tt, d), lambda i: (i, 0))
    rowspec = pl.BlockSpec((tt, 1, PEER_SEL), lambda i: (i, 0, 0))
    tile_scratch = pltpu.VMEM((PEER_GROUP, SUBLANES, LANES), F32)
    rows = expert.reshape(t * PEER_SEL)
    coef = pl.pallas_call(
        _peer_act_kernel,
        grid=(t // tt,),
        in_specs=[smem_flat, xspec, rowspec, _table_spec(u_pack.shape)],
        out_specs=rowspec,
        out_shape=jax.ShapeDtypeStruct((t, 1, PEER_SEL), F32),
        scratch_shapes=[pltpu.VMEM((PEER_GROUP * PEER_NCHUNK, PEER_CHUNK * ROW_WORDS, LANES), F32),
                        tile_scratch],
        compiler_params=_cparams(1, vmem),
        name="peer_act",
    )(rows, h, weight.reshape(t, 1, PEER_SEL), u_pack)
    return pl.pallas_call(
        _peer_out_kernel,
        grid=(t // tt,),
        in_specs=[smem_flat, smem_flat, xspec,
                  pl.BlockSpec((None, 1, d), lambda i: (i * tt // tokens_per_gate, 0, 0)),
                  _table_spec(v_pack.shape)],
        out_specs=xspec,
        out_shape=jax.ShapeDtypeStruct((t, d), F32),
        scratch_shapes=[tile_scratch],
        compiler_params=_cparams(1, vmem),
        name="peer_out",
    )(rows, coef.reshape(t * PEER_SEL), res, gate, v_pack)


def peer_residual(x, gate, h, wq, keys, u_pack, v_pack):
    b, n, d = h.shape
    hf = h.reshape(b * n, d)
    expert, weight = peer_select(hf, wq, keys)
    out = peer_experts(hf, expert, weight, u_pack, v_pack, x.reshape(b * n, d), gate, n)
    return out.reshape(b, n, d)


RG_CONV = 4


def _shift_rows(cur, halo, k, reverse):
    tb = cur.shape[0]
    row8 = lax.broadcasted_iota(jnp.int32, (SUBLANES, cur.shape[1]), 0)
    if not reverse:
        rolled = pltpu.roll(cur, k, axis=0)
        first = jnp.where(row8 < k, pltpu.roll(halo, k, axis=0), rolled[0:SUBLANES])
        return jnp.concatenate([first, rolled[SUBLANES:]], axis=0)
    rolled = pltpu.roll(cur, tb - k, axis=0)
    last = jnp.where(row8 >= SUBLANES - k, pltpu.roll(halo, SUBLANES - k, axis=0),
                     rolled[tb - SUBLANES:])
    return jnp.concatenate([rolled[:tb - SUBLANES], last], axis=0)


def _rglru_kernel(u_ref, h0_ref, cw_ref, cb_ref, wa_ref, ba_ref, wx_ref, bx_ref, lam_ref, *rest,
                  reverse, has_acc):
    if has_acc:
        acc_ref, out_ref, hlast_ref, a_s, b_s, hp_s, halo_s = rest
    else:
        out_ref, hlast_ref, a_s, b_s, hp_s, halo_s = rest
    tb, c = u_ref.shape

    @pl.when(pl.program_id(1) == 0)
    def _():
        hp_s[...] = h0_ref[...]
        halo_s[...] = jnp.zeros_like(halo_s)

    cur = u_ref[...]
    halo = halo_s[...]
    xc = cb_ref[...] + cw_ref[RG_CONV - 1:RG_CONV, :] * cur
    for k in range(1, RG_CONV):
        xc = xc + cw_ref[RG_CONV - 1 - k:RG_CONV - k, :] * _shift_rows(cur, halo, k, reverse)
    halo_s[...] = cur[0:SUBLANES] if reverse else cur[tb - SUBLANES:]

    xb = xc.astype(BF16)
    gate_r = _sigmoid(jnp.dot(xb, wa_ref[...], preferred_element_type=F32) + ba_ref[...])
    gate_i = _sigmoid(jnp.dot(xb, wx_ref[...], preferred_element_type=F32) + bx_ref[...])
    lam = lam_ref[...]
    softplus_neg = jnp.maximum(-lam, 0.0) + jnp.log(1.0 + jnp.exp(-jnp.abs(lam)))
    a = jnp.exp(-RG_C * gate_r * softplus_neg)
    a_s[...] = a
    b_s[...] = jnp.sqrt(1.0 - a * a) * (gate_i * xc)

    row8 = lax.broadcasted_iota(jnp.int32, (SUBLANES, c), 0)
    nt = tb // SUBLANES

    def step(j, hp):
        jj = nt - 1 - j if reverse else j
        r0 = pl.multiple_of(jj * SUBLANES, SUBLANES)
        av = a_s[pl.ds(r0, SUBLANES), :]
        bv = b_s[pl.ds(r0, SUBLANES), :]
        for k in (1, 2, 4):
            if reverse:
                ok = row8 < SUBLANES - k
                sh = SUBLANES - k
            else:
                ok = row8 >= k
                sh = k
            a_prev = jnp.where(ok, pltpu.roll(av, sh, axis=0), 1.0)
            b_prev = jnp.where(ok, pltpu.roll(bv, sh, axis=0), 0.0)
            bv = av * b_prev + bv
            av = av * a_prev
        h = av * hp + bv
        if has_acc:
            out_ref[pl.ds(r0, SUBLANES), :] = h + acc_ref[pl.ds(r0, SUBLANES), :]
        else:
            out_ref[pl.ds(r0, SUBLANES), :] = h
        return h[0:1] if reverse else h[SUBLANES - 1:SUBLANES]

    hp = lax.fori_loop(0, nt, step, hp_s[...])
    hp_s[...] = hp
    hlast_ref[...] = hp


def rglru_scan(u, h0, p, reverse, acc=None, tb=512):
    b, n, c = u.shape
    tb = min(tb, n)
    assert n % tb == 0
    nblk = n // tb
    if reverse:
        seq = pl.BlockSpec((None, tb, c), lambda bi, i: (bi, nblk - 1 - i, 0))
    else:
        seq = pl.BlockSpec((None, tb, c), lambda bi, i: (bi, i, 0))
    state = pl.BlockSpec((None, 1, c), lambda bi, i: (bi, 0, 0))

    def par(a):
        return pl.BlockSpec(a.shape, lambda bi, i: (0,) * a.ndim)

    params = [p["conv_w"], p["conv_b"], p["wa"], p["ba"], p["wx"], p["bx"], p["lam"]]
    in_specs = [seq, state] + [par(a) for a in params]
    args = [u, h0] + params
    if acc is not None:
        in_specs.append(seq)
        args.append(acc)
    return pl.pallas_call(
        functools.partial(_rglru_kernel, reverse=reverse, has_acc=acc is not None),
        grid=(b, nblk),
        in_specs=in_specs,
        out_specs=[seq, state],
        out_shape=[jax.ShapeDtypeStruct((b, n, c), F32), jax.ShapeDtypeStruct((b, 1, c), F32)],
        scratch_shapes=[pltpu.VMEM((tb, c), F32), pltpu.VMEM((tb, c), F32),
                        pltpu.VMEM((1, c), F32), pltpu.VMEM((SUBLANES, c), F32)],
        compiler_params=_cparams(2),
        name="rglru_bwd" if reverse else "rglru_fwd",
    )(*args)


def _block_diag(w):
    g, bs, _ = w.shape
    eye = jnp.eye(g, dtype=w.dtype)
    return (eye[:, None, :, None] * w[:, :, None, :]).reshape(g * bs, g * bs)


def rglru_params(conv_w, conv_b, wa, ba, wx, bx, lam, d):
    c = conv_b.shape[-1]
    return {"conv_w": conv_w[d], "conv_b": conv_b[d].reshape(1, c),
            "wa": _block_diag(wa[d]).astype(BF16), "ba": ba[d].reshape(1, c),
            "wx": _block_diag(wx[d]).astype(BF16), "bx": bx[d].reshape(1, c),
            "lam": lam[d].reshape(1, c)}


def rglru_mix(u_ctx, u_lat, params):
    b, _, c = u_ctx.shape
    zero = jnp.zeros((b, 1, c), F32)
    h_ctx = h_lat = None
    for d in range(2):
        h_ctx, last = rglru_scan(u_ctx, zero, params[d], reverse=bool(d), acc=h_ctx)
        h_lat, _ = rglru_scan(u_lat, last, params[d], reverse=bool(d), acc=h_lat)
    return h_ctx, h_lat


def _gla_kernel(x_ref, s0_ref, wlr_ref, blr_ref, *rest, reverse, has_acc):
    if has_acc:
        acc_ref, o_ref, s_out_ref, st_s = rest
    else:
        o_ref, s_out_ref, st_s = rest
    nb, tb = x_ref.shape[0], x_ref.shape[1]
    ch = GLA_CHUNK

    @pl.when(pl.program_id(1) == 0)
    def _():
        st_s[...] = s0_ref[...]

    r_i = lax.broadcasted_iota(jnp.int32, (ch, ch), 0)
    c_i = lax.broadcasted_iota(jnp.int32, (ch, ch), 1)
    if reverse:
        cum_mat = (c_i >= r_i).astype(BF16)
        keep = c_i > r_i
    else:
        cum_mat = (c_i <= r_i).astype(BF16)
        keep = c_i <= r_i
    lane = lax.broadcasted_iota(jnp.int32, (1, LANES), 1)
    head_lanes = (lane < GLA_DK, lane >= GLA_DK)
    nt_dims = (((1,), (1,)), ((), ()))
    chunks = range(tb // ch)
    steps = [(cidx, bb) for cidx in (reversed(chunks) if reverse else chunks) for bb in range(nb)]
    for cidx, bb in steps:
        r0 = cidx * ch
        q = x_ref[bb, r0:r0 + ch, 0:GLA_QK] * (GLA_DK ** -0.5)
        k = x_ref[bb, r0:r0 + ch, GLA_QK:2 * GLA_QK]
        v = x_ref[bb, r0:r0 + ch, 2 * GLA_QK:2 * GLA_QK + GLA_V]
        lr = x_ref[bb, r0:r0 + ch, 2 * GLA_QK + GLA_V:GLA_IN_PAD]
        logits = jnp.dot(lr, wlr_ref[...], precision=HIGHEST, preferred_element_type=F32) + blr_ref[...]
        log_a = _log_sigmoid(logits) * (1.0 / GLA_TAU)
        la_hi = log_a.astype(BF16)
        la_mid, la_lo = _split_bf16(log_a - la_hi.astype(F32))
        cum = (jnp.dot(cum_mat, la_hi, preferred_element_type=F32)
               + (jnp.dot(cum_mat, la_mid, preferred_element_type=F32)
                  + jnp.dot(cum_mat, la_lo, preferred_element_type=F32)))
        tot = cum[0:1] if reverse else cum[ch - 1:ch]
        qg = q * jnp.exp(cum)
        kg = k * jnp.exp(-cum)
        kd = k * jnp.exp(tot - cum)
        decay = jnp.exp(tot)
        outs = []
        for h in range(GLA_HEADS):
            sl = slice((h // 2) * LANES, (h // 2 + 1) * LANES)
            mine = head_lanes[h % 2]
            qm = jnp.where(mine, qg[:, sl], 0.0).astype(BF16)
            scores = lax.dot_general(qm, kg[:, sl].astype(BF16), nt_dims, preferred_element_type=F32)
            scores = jnp.where(keep, scores, 0.0)
            vh = v[:, h * GLA_DV:(h + 1) * GLA_DV]
            st = st_s[bb, h]
            o = jnp.dot(scores.astype(BF16), vh.astype(BF16), preferred_element_type=F32)
            o = o + lax.dot_general(qm, st.astype(BF16), nt_dims, preferred_element_type=F32)
            kdm = jnp.where(mine, kd[:, sl], 0.0).astype(BF16)
            st_s[bb, h] = st * decay[:, sl] + jnp.dot(vh.T.astype(BF16), kdm, preferred_element_type=F32)
            outs.append(o)
        o_all = jnp.concatenate(outs, axis=1)
        if has_acc:
            o_all = o_all + acc_ref[bb, r0:r0 + ch, :]
        o_ref[bb, r0:r0 + ch, :] = o_all
    s_out_ref[...] = st_s[...]


def gla_scan(x, s0, wlr, blr, reverse, acc=None, tb=512, nb=2):
    b, n, cin = x.shape
    tb = min(tb, n)
    nb = min(nb, b)
    assert n % tb == 0 and tb % GLA_CHUNK == 0 and b % nb == 0
    nblk = n // tb

    def seq(width):
        if reverse:
            return pl.BlockSpec((nb, tb, width), lambda bi, i: (bi, nblk - 1 - i, 0))
        return pl.BlockSpec((nb, tb, width), lambda bi, i: (bi, i, 0))

    state = pl.BlockSpec((nb, GLA_HEADS, GLA_DV, LANES), lambda bi, i: (bi, 0, 0, 0))
    in_specs = [seq(cin), state,
                pl.BlockSpec(wlr.shape, lambda bi, i: (0, 0)), pl.BlockSpec(blr.shape, lambda bi, i: (0, 0))]
    args = [x, s0, wlr, blr]
    if acc is not None:
        in_specs.append(seq(GLA_V))
        args.append(acc)
    return pl.pallas_call(
        functools.partial(_gla_kernel, reverse=reverse, has_acc=acc is not None),
        grid=(b // nb, nblk),
        in_specs=in_specs,
        out_specs=[seq(GLA_V), state],
        out_shape=[jax.ShapeDtypeStruct((b, n, GLA_V), F32),
                   jax.ShapeDtypeStruct((b, GLA_HEADS, GLA_DV, LANES), F32)],
        scratch_shapes=[pltpu.VMEM((nb, GLA_HEADS, GLA_DV, LANES), F32)],
        compiler_params=_cparams(2, 40 << 20),
        name="gla_bwd" if reverse else "gla_fwd",
    )(*args)


def gla_params(w_lr, b_lr, d):
    w = jnp.zeros((LANES, GLA_QK), F32).at[d * GLA_RANK:(d + 1) * GLA_RANK].set(w_lr[d])
    return w, b_lr[d].reshape(1, GLA_QK)


def gla_mix(x_ctx, x_lat_cols, w_lr, b_lr):
    b = x_ctx.shape[0]
    zero = jnp.zeros((b, GLA_HEADS, GLA_DV, LANES), F32)
    o_ctx = o_lat = None
    for d in range(2):
        w, bias = gla_params(w_lr, b_lr, d)
        o_ctx, s = gla_scan(x_ctx, zero, w, bias, reverse=bool(d), acc=o_ctx)
        o_lat, _ = gla_scan(x_lat_cols, s, w, bias, reverse=bool(d), acc=o_lat)
    return o_ctx, o_lat


def raster_to_column(t):
    b, n = t.shape[:2]
    return t.reshape(b, n // GRID_W, GRID_W, *t.shape[2:]).swapaxes(1, 2).reshape(t.shape)


def column_to_raster(t):
    b, n = t.shape[:2]
    return t.reshape(b, GRID_W, n // GRID_W, *t.shape[2:]).swapaxes(1, 2).reshape(t.shape)


def _merge_kernel(x_ref, gt_ref, h_ref, yhy_ref, hrg_ref, ogla_ref,
                  wg_ref, bm_ref, gn_ref, why_ref, wrg_ref, wgla_ref, wout_ref, o_ref):
    d = x_ref.shape[1]

    def proj(y, w_ref):
        return jnp.dot(y.astype(BF16), w_ref[...], preferred_element_type=F32)

    gates_in = proj(h_ref[...], wg_ref)
    y_rg = hrg_ref[...] * _gelu_tanh(gates_in[:, 0:RG_W])
    o = ogla_ref[...]
    heads = []
    for h in range(GLA_HEADS):
        oh = o[:, h * GLA_DV:(h + 1) * GLA_DV]
        heads.append(oh * lax.rsqrt(jnp.mean(oh * oh, axis=-1, keepdims=True) + NORM_EPS) * gn_ref[...])
    gg = gates_in[:, RG_W:RG_W + GLA_V]
    y_gla = jnp.concatenate(heads, axis=1) * (gg * _sigmoid(gg))
    gate = _sigmoid(gates_in[:, RG_W + GLA_V:] + bm_ref[...])
    m = (gate[:, 0:d] * proj(yhy_ref[...], why_ref) + gate[:, d:2 * d] * proj(y_rg, wrg_ref)
         + gate[:, 2 * d:3 * d] * proj(y_gla, wgla_ref))
    o_ref[...] = x_ref[...] + gt_ref[...] * proj(m, wout_ref)


def merge_residual(x, gt, h, y_hy, h_rg, o_gla, lw, tm=256):
    b, n, d = x.shape
    tm = min(tm, n)
    assert n % tm == 0

    def seq(a):
        return pl.BlockSpec((None, tm, a.shape[2]), lambda bi, i: (bi, i, 0))

    def par(a):
        return pl.BlockSpec(a.shape, lambda bi, i: (0,) * a.ndim, pipeline_mode=pl.Buffered(1))

    streams = [h, y_hy, h_rg, o_gla]
    params = [lw["w_gates"], lw["b_merge"], lw["gla_norm_g"], lw["w_hy_o"], lw["w_rg_o"],
              lw["w_gla_o"], lw["w_out"]]
    return pl.pallas_call(
        _merge_kernel,
        grid=(b, n // tm),
        in_specs=[seq(x), pl.BlockSpec((None, 1, d), lambda bi, i: (bi, 0, 0))]
                 + [seq(a) for a in streams] + [par(a) for a in params],
        out_specs=seq(x),
        out_shape=jax.ShapeDtypeStruct((b, n, d), F32),
        compiler_params=_cparams(2, 48 << 20),
        name="merge_residual",
    )(x, gt, *streams, *params)


def _shortconv_kernel(x_ref, w_ref, b_ref, o_ref, *, rows):
    n, c = x_ref.shape
    zero = jnp.zeros((SUBLANES, c), F32)
    for r0 in range(0, n, rows):
        cur = x_ref[r0:r0 + rows, :]
        before = x_ref[r0 - SUBLANES:r0, :] if r0 > 0 else zero
        after = x_ref[r0 + rows:r0 + rows + SUBLANES, :] if r0 + rows < n else zero
        o_ref[r0:r0 + rows, :] = (b_ref[...] + w_ref[0:1, :] * _shift_rows(cur, before, 1, False)
                                  + w_ref[1:2, :] * cur + w_ref[2:3, :] * _shift_rows(cur, after, 1, True))


def shortconv(x, w, bias):
    b, n, c = x.shape
    blk = pl.BlockSpec((None, n, LANES), lambda bi, j: (bi, 0, j))
    return pl.pallas_call(
        functools.partial(_shortconv_kernel, rows=min(n, 1024)),
        grid=(b, c // LANES),
        in_specs=[blk, pl.BlockSpec((3, LANES), lambda bi, j: (0, j)),
                  pl.BlockSpec((1, LANES), lambda bi, j: (0, j))],
        out_specs=blk,
        out_shape=jax.ShapeDtypeStruct((b, n, c), F32),
        compiler_params=_cparams(2, 40 << 20),
        name="shortconv",
    )(x, w, bias.reshape(1, c))


def _hy_taps_kernel(w1_ref, b1_ref, w2_ref, b2_ref, w3_ref, freq_ref, band_ref, delta_ref,
                    taps_ref, asum_ref, *, n):
    tb = taps_ref.shape[1]
    i = pl.program_id(0)
    m = i * tb + lax.broadcasted_iota(jnp.int32, (tb, LANES), 0)
    idx = jnp.where(m < n, m, 2 * n - m).astype(F32)
    lane = lax.broadcasted_iota(jnp.int32, (tb, LANES), 1)
    tn = idx / (n - 1)
    ang = (2.0 * math.pi / n) * idx * band_ref[...]
    feats = jnp.where(lane == 0, tn,
                      jnp.where(lane <= HY_BANDS, jnp.cos(ang),
                                jnp.where(lane <= 2 * HY_BANDS, -jnp.sin(ang), 0.0)))
    fr = freq_ref[...]

    def dense(x, w_ref):
        return jnp.dot(x, w_ref[...], precision=HIGHEST, preferred_element_type=F32)

    h = jnp.sin(fr * (dense(feats, w1_ref) + b1_ref[...]))
    h = jnp.sin(fr * (dense(h, w2_ref) + b2_ref[...]))
    h = dense(h, w3_ref) * jnp.exp(-tn[:, 0:1] * delta_ref[...])
    row = m[:, 0:1]
    use_fwd = row < n
    use_bwd = (row > n) | (row == 0)
    sums = []
    for o in range(2):
        hf = jnp.where(use_fwd, h[:, (2 * o) * HY_W:(2 * o + 1) * HY_W], 0.0)
        hb = jnp.where(use_bwd, h[:, (2 * o + 1) * HY_W:(2 * o + 2) * HY_W], 0.0)
        taps_ref[o] = hf + hb
        sums += [jnp.sum(jnp.abs(hf), axis=0, keepdims=True), jnp.sum(jnp.abs(hb), axis=0, keepdims=True)]

    @pl.when(i == 0)
    def _():
        asum_ref[...] = jnp.zeros_like(asum_ref)

    asum_ref[...] += jnp.concatenate(sums, axis=1)


def hyena_taps(n, w1, b1, w2, b2, w3, freq, tb=256):
    nout = w3.shape[1]
    assert nout == 4 * HY_W and (2 * n) % tb == 0
    bands = np.zeros((1, LANES), np.float32)
    lin = np.linspace(1e-4, HY_BANDS - 1, HY_BANDS, dtype=np.float32)
    bands[0, 1:1 + HY_BANDS] = lin
    bands[0, 1 + HY_BANDS:1 + 2 * HY_BANDS] = lin
    deltas = np.abs(np.linspace(HY_FAST_RATE, HY_SLOW_RATE, HY_W, dtype=np.float32))
    deltas = np.tile(deltas, nout // HY_W).reshape(1, nout)
    w1p = jnp.zeros((LANES, HY_FFN), F32).at[:w1.shape[0]].set(w1)
    params = [w1p, b1.reshape(1, -1), w2, b2.reshape(1, -1), w3, freq.reshape(1, -1),
              jnp.asarray(bands), jnp.asarray(deltas)]
    return pl.pallas_call(
        functools.partial(_hy_taps_kernel, n=n),
        grid=(2 * n // tb,),
        in_specs=[pl.BlockSpec(a.shape, lambda i: (0, 0)) for a in params],
        out_specs=[pl.BlockSpec((2, tb, HY_W), lambda i: (0, i, 0)), pl.BlockSpec((1, nout), lambda i: (0, 0))],
        out_shape=[jax.ShapeDtypeStruct((2, 2 * n, HY_W), F32), jax.ShapeDtypeStruct((1, nout), F32)],
        compiler_params=_cparams(1),
        name="hyena_taps",
    )(*params)


def _cis(num, den):
    ang = (2.0 * math.pi / den) * (num % den).astype(F32)
    return jnp.cos(ang), jnp.sin(ang)


def _dft_tables(n):
    big = 2 * n
    q = int(round(math.sqrt(big)))
    assert q * q == big and q % (2 * SUBLANES) == 0
    ar = jnp.arange(q, dtype=jnp.int32)
    num = ar[None, :, None] * (q * ar[None, None, :] + ar[:, None, None])
    c, s = _cis(num, big)
    w1 = jnp.concatenate([c, -s], axis=1)
    ct, st = jnp.swapaxes(c, 1, 2)[:, :q // 2], jnp.swapaxes(s, 1, 2)[:, :q // 2]
    v = jnp.concatenate([ct, -st], axis=2) * (1.0 / big)
    c2, s2 = _cis(ar[:, None] * ar[None, :], q)
    f2 = jnp.concatenate([jnp.concatenate([c2, s2], axis=1), jnp.concatenate([-s2, c2], axis=1)], axis=0)
    g2 = jnp.concatenate([jnp.concatenate([c2, -s2], axis=1), jnp.concatenate([s2, c2], axis=1)], axis=0)
    return {"q": q, "w1": w1, "v": v, "f2": f2, "g2": g2}


def _level_kernel(w_ref, x_ref, *rest, has_gate, precise):
    if has_gate:
        src_ref, gate_ref, skip_ref = rest[:3]
    o_ref = rest[-1]
    for s in range(w_ref.shape[0]):
        x = x_ref[s]
        if precise:
            y = jnp.dot(w_ref[s], x, precision=HIGHEST, preferred_element_type=F32)
        else:
            y = jnp.dot(w_ref[s], x.astype(BF16), preferred_element_type=F32)
        if has_gate:
            y = gate_ref[s] * (y + src_ref[s] * skip_ref[...])
        o_ref[s] = y.astype(o_ref.dtype)


def dft_level(w, x, x_group=0, gate=None, out_dtype=F32, precise=False, sb=8):
    b, q, k, _ = x.shape
    m = w.shape[1]
    c = HY_W
    sb = min(sb, q)

    def seq(rows, group):
        return pl.BlockSpec((None, sb, rows, c), lambda i, bi: (bi, i, 0, group))

    in_specs = [pl.BlockSpec((sb, m, k), lambda i, bi: (i, 0, 0)), seq(k, x_group)]
    args = [w, x]
    if gate is not None:
        src, src_group, gates, gate_group, skip = gate
        in_specs += [seq(m, src_group), seq(m, gate_group), pl.BlockSpec((1, c), lambda i, bi: (0, 0))]
        args += [src, gates, skip]
    return pl.pallas_call(
        functools.partial(_level_kernel, has_gate=gate is not None, precise=precise),
        grid=(q // sb, b),
        in_specs=in_specs,
        out_specs=seq(m, 0),
        out_shape=jax.ShapeDtypeStruct((b, q, m, c), out_dtype),
        compiler_params=_cparams(2, 40 << 20),
        name="dft_level",
    )(*args)


def _dft_mid_kernel(a_ref, h_ref, f_ref, g_ref, o_ref):
    q = a_ref.shape[1] // 2
    for j in range(a_ref.shape[0]):
        x = jnp.dot(f_ref[...], a_ref[j], preferred_element_type=F32)
        xr, xi = x[:q], x[q:]
        hr, hi = h_ref[j, :q, :], h_ref[j, q:, :]
        y = jnp.concatenate([xr * hr - xi * hi, xr * hi + xi * hr], axis=0)
        o_ref[j] = jnp.dot(g_ref[...], y.astype(BF16), preferred_element_type=F32).astype(o_ref.dtype)


def dft_mid(a, spec, f2, g2, kb=8):
    b, q, q2, c = a.shape
    blk = pl.BlockSpec((None, kb, q2, c), lambda i, bi: (bi, i, 0, 0))
    mat = pl.BlockSpec((q2, q2), lambda i, bi: (0, 0))
    return pl.pallas_call(
        _dft_mid_kernel,
        grid=(q // kb, b),
        in_specs=[blk, pl.BlockSpec((kb, q2, c), lambda i, bi: (i, 0, 0)), mat, mat],
        out_specs=blk,
        out_shape=jax.ShapeDtypeStruct(a.shape, BF16),
        compiler_params=_cparams(2, 40 << 20),
        name="dft_mid",
    )(a, spec, f2, g2)


def _dft_spec_kernel(a_ref, f_ref, asum_ref, o_ref):
    inv = 1.0 / (asum_ref[0:1, :] + asum_ref[1:2, :] + 1e-6)
    for j in range(a_ref.shape[0]):
        o_ref[j] = jnp.dot(f_ref[...], a_ref[j], precision=HIGHEST, preferred_element_type=F32) * inv


def dft_spec(a, f2, asum, kb=8):
    o, q, q2, c = a.shape
    blk = pl.BlockSpec((None, kb, q2, c), lambda i, oi: (oi, i, 0, 0))
    return pl.pallas_call(
        _dft_spec_kernel,
        grid=(q // kb, o),
        in_specs=[blk, pl.BlockSpec((q2, q2), lambda i, oi: (0, 0)),
                  pl.BlockSpec((None, 2, c), lambda i, oi: (oi, 0, 0))],
        out_specs=blk,
        out_shape=jax.ShapeDtypeStruct(a.shape, F32),
        compiler_params=_cparams(2, 40 << 20),
        name="dft_spec",
    )(a, f2, asum)


def _swap_levels(a):
    b, q, q2, c = a.shape
    return a.reshape(b, q, 2, q, c).transpose(0, 3, 2, 1, 4).reshape(b, q, q2, c)


def hyena_long(u, taps, asum, skip, tabs):
    b, n, _ = u.shape
    q = tabs["q"]
    c = HY_W

    def to_levels(t, rows):
        return t.reshape(t.shape[0], rows, q, t.shape[2]).swapaxes(1, 2)

    a = dft_level(tabs["w1"], to_levels(taps, q), precise=True)
    spec = dft_spec(_swap_levels(a), tabs["f2"], asum.reshape(2, 2, c))
    w1d = tabs["w1"][:, :, :q // 2].astype(BF16)
    vd = tabs["v"].astype(BF16)
    f2, g2 = tabs["f2"].astype(BF16), tabs["g2"].astype(BF16)
    u_t = to_levels(u, q // 2)
    src, group = u_t, 0
    for order in range(2):
        a = dft_level(w1d, src, x_group=group, out_dtype=BF16)
        cm = dft_mid(_swap_levels(a), spec[order], f2, g2)
        src = dft_level(vd, _swap_levels(cm),
                        gate=(src, group, u_t, order + 1, skip[order].reshape(1, c)))
        group = 0
    return src.swapaxes(1, 2).reshape(b, n, c)


def _dft_small_kernel(u_ref, gate_ref, skip_ref, h_ref, f_ref, g_ref, o_ref):
    u = u_ref[...]
    nb = h_ref.shape[0] // 2
    x = jnp.dot(f_ref[...], u, precision=HIGHEST, preferred_element_type=F32)
    xr, xi = x[:nb], x[nb:]
    hr, hi = h_ref[:nb, :], h_ref[nb:, :]
    y = jnp.concatenate([xr * hr - xi * hi, xr * hi + xi * hr], axis=0)
    conv = jnp.dot(g_ref[...], y, precision=HIGHEST, preferred_element_type=F32)
    o_ref[...] = gate_ref[...] * (conv + u * skip_ref[...])


def _dft_small_spec_kernel(taps_ref, f_ref, asum_ref, o_ref):
    inv = 1.0 / (asum_ref[0:1, :] + asum_ref[1:2, :] + 1e-6)
    o_ref[...] = jnp.dot(f_ref[...], taps_ref[...], precision=HIGHEST,
                         preferred_element_type=F32) * inv


def hyena_short_seq(u, taps, asum, skip):
    b, n, _ = u.shape
    big = 2 * n
    c = HY_W
    ar = jnp.arange(big, dtype=jnp.int32)
    cs, sn = _cis(ar[:, None] * ar[None, :], big)
    f_full = jnp.concatenate([cs, -sn], axis=0)
    g_half = jnp.concatenate([cs[:n], -sn[:n]], axis=1) * (1.0 / big)
    tap = pl.BlockSpec((None, big, c), lambda o: (o, 0, 0))
    spec = pl.pallas_call(
        _dft_small_spec_kernel,
        grid=(2,),
        in_specs=[tap, pl.BlockSpec((2 * big, big), lambda o: (0, 0)),
                  pl.BlockSpec((None, 2, c), lambda o: (o, 0, 0))],
        out_specs=pl.BlockSpec((None, 2 * big, c), lambda o: (o, 0, 0)),
        out_shape=jax.ShapeDtypeStruct((2, 2 * big, c), F32),
        compiler_params=_cparams(1),
        name="dft_small_spec",
    )(taps, f_full, asum.reshape(2, 2, c))
    f_data = f_full[:, :n]
    src, group = u, 0
    for order in range(2):
        src = pl.pallas_call(
            _dft_small_kernel,
            grid=(b,),
            in_specs=[pl.BlockSpec((None, n, c), functools.partial(lambda bi, g: (bi, 0, g), g=group)),
                      pl.BlockSpec((None, n, c), functools.partial(lambda bi, g: (bi, 0, g), g=order + 1)),
                      pl.BlockSpec((1, c), lambda bi: (0, 0)),
                      pl.BlockSpec((2 * big, c), lambda bi: (0, 0)),
                      pl.BlockSpec((2 * big, n), lambda bi: (0, 0)),
                      pl.BlockSpec((n, 2 * big), lambda bi: (0, 0))],
            out_specs=pl.BlockSpec((None, n, c), lambda bi: (bi, 0, 0)),
            out_shape=jax.ShapeDtypeStruct((b, n, c), F32),
            compiler_params=_cparams(1),
            name="dft_small",
        )(src, u, skip[order].reshape(1, c), spec[order], f_data, g_half)
        group = 0
    return src


HY_IN = 3 * HY_W
IN_GROUPS = (HY_IN, RG_W, RG_W, GLA_IN, GLA_V, 3 * D_MODEL)


def _split_w_in(w):
    parts, start = [], 0
    for width in IN_GROUPS:
        parts.append(w[:, start:start + width])
        start += width
    parts[3] = jnp.pad(parts[3], ((0, 0), (0, GLA_IN_PAD - GLA_IN)))
    return [p.astype(BF16) for p in parts]


def _project(h, w):
    b, n, d = h.shape
    return matmul(h.reshape(b * n, d), w).reshape(b, n, w.shape[1])


def kernel(x, c, ctx, c_ctx, w_mod, b_mod, g_norm_mix, g_norm_ffn, w_in, hy_conv_w, hy_conv_b,
           hy_w1, hy_b1, hy_w2, hy_b2, hy_w3, hy_freq, hy_skip, rg_conv_w, rg_conv_b, rg_wa, rg_ba,
           rg_wx, rg_bx, rg_lambda, gla_w_lr, gla_b_lr, gla_norm_g, w_hy_o, w_rg_o, w_gla_o, b_merge,
           w_out, peer_wq, peer_keys, peer_u, peer_v, g_final):
    b, n, d = x.shape
    n_ctx = ctx.shape[1]
    depth = w_mod.shape[0]
    cond = jnp.concatenate([c, c_ctx[None, :]], axis=0)
    cond = jnp.pad(cond, ((0, -(b + 1) % SUBLANES), (0, 0)))
    tabs = _dft_tables(n)
    x_lat, x_ctx = x, ctx
    for l in range(depth):
        need_ctx = l < depth - 1
        mod = matmul(cond, w_mod[l].astype(BF16), bias=b_mod[l], silu_in=True)
        sh1, sc1, gt1, sh2, sc2, gt2 = [mod[:b, i * d:(i + 1) * d].reshape(b, 1, d) for i in range(6)]
        csh1, csc1, cgt1, csh2, csc2, cgt2 = [
            jnp.broadcast_to(mod[b:b + 1, i * d:(i + 1) * d].reshape(1, 1, d), (b, 1, d)) for i in range(6)]
        w_hy, w_rgx, w_rgg, w_gla, w_glag, w_mg = _split_w_in(w_in[l])
        lw = {"b_merge": b_merge[l].reshape(1, 3 * d), "gla_norm_g": gla_norm_g[l].reshape(1, GLA_DV),
              "w_gates": jnp.concatenate([w_rgg, w_glag, w_mg], axis=1),
              "w_hy_o": w_hy_o[l].astype(BF16), "w_rg_o": w_rg_o[l].astype(BF16),
              "w_gla_o": w_gla_o[l].astype(BF16), "w_out": w_out[l].astype(BF16)}
        rg_par = [rglru_params(rg_conv_w[l], rg_conv_b[l], rg_wa[l], rg_ba[l], rg_wx[l], rg_bx[l],
                               rg_lambda[l], dd) for dd in range(2)]
        filt = (hy_w1[l], hy_b1[l], hy_w2[l], hy_b2[l], hy_w3[l], hy_freq[l])
        u_pack, v_pack = _pack_table(peer_u[l]), _pack_table(peer_v[l])

        h_lat = normmod(x_lat, g_norm_mix[l], sh1, sc1, BF16)
        h_ctx = normmod(x_ctx, g_norm_mix[l], csh1, csc1, BF16)
        hy_l, rgx_l, gla_l = [_project(h_lat, w) for w in (w_hy, w_rgx, w_gla)]
        rgx_c, gla_c = [_project(h_ctx, w) for w in (w_rgx, w_gla)]

        taps, asum = hyena_taps(n, *filt)
        y_hy_l = hyena_long(shortconv(hy_l, hy_conv_w[l], hy_conv_b[l]), taps, asum, hy_skip[l], tabs)
        h_rg_c, h_rg_l = rglru_mix(rgx_c, rgx_l, rg_par)
        o_gla_c, o_gla_l = gla_mix(gla_c, raster_to_column(gla_l), gla_w_lr[l], gla_b_lr[l])
        x_lat = merge_residual(x_lat, gt1, h_lat, y_hy_l, h_rg_l, column_to_raster(o_gla_l), lw)
        h2 = normmod(x_lat, g_norm_ffn[l], sh2, sc2, F32)
        x_lat = peer_residual(x_lat, gt2, h2, peer_wq[l], peer_keys[l], u_pack, v_pack)
        if need_ctx:
            taps_c, asum_c = hyena_taps(n_ctx, *filt)
            y_hy_c = hyena_short_seq(shortconv(_project(h_ctx, w_hy), hy_conv_w[l], hy_conv_b[l]),
                                     taps_c, asum_c, hy_skip[l])
            x_ctx = merge_residual(x_ctx, cgt1, h_ctx, y_hy_c, h_rg_c, o_gla_c, lw)
            h2c = normmod(x_ctx, g_norm_ffn[l], csh2, csc2, F32)
            x_ctx = peer_residual(x_ctx, cgt2, h2c, peer_wq[l], peer_keys[l], u_pack, v_pack)
    zero = jnp.zeros((b, 1, d), F32)
    return normmod(x_lat, g_final, zero, zero, F32, mod=False)
```

```python
import functools
import math

import numpy as np
import jax
import jax.numpy as jnp
from jax import lax
from jax.experimental import pallas as pl
from jax.experimental.pallas import tpu as pltpu

F32 = jnp.float32
BF16 = jnp.bfloat16
HIGHEST = lax.Precision.HIGHEST

D_MODEL = 1024
DEPTH = 4
GRID_W = 64
NORM_EPS = 1e-6

HY_W = 512
HY_BANDS = 16
HY_FFN = 64
HY_FAST_RATE = math.log(1e-2) / 0.3
HY_SLOW_RATE = math.log(1e-2) / 1.5

RG_W = 512
RG_BLOCKS = 8
RG_C = 8.0

GLA_HEADS = 4
GLA_DK = 64
GLA_DV = 128
GLA_QK = GLA_HEADS * GLA_DK
GLA_V = GLA_HEADS * GLA_DV
GLA_RANK = 16
GLA_TAU = 16.0
GLA_CHUNK = 64
GLA_IN = 2 * GLA_QK + GLA_V + 2 * GLA_RANK
GLA_IN_PAD = 2 * GLA_QK + GLA_V + 128

PEER_HEADS = 8
PEER_NKEYS = 128
PEER_TOPK = 16
PEER_HALF = 128
PEER_SEL = PEER_HEADS * PEER_TOPK

V7X_VMEM_BYTES = 64 * 1024 * 1024
SUBLANES = 8
LANES = 128


def _cparams(n_grid, vmem_bytes=None):
    kw = dict(dimension_semantics=("arbitrary",) * n_grid)
    if vmem_bytes is not None:
        assert vmem_bytes < V7X_VMEM_BYTES
        kw["vmem_limit_bytes"] = int(vmem_bytes)
    return pltpu.CompilerParams(**kw)


def _gelu_tanh(x):
    return 0.5 * x * (1.0 + jnp.tanh(math.sqrt(2.0 / math.pi) * (x + 0.044715 * (x * x * x))))


def _sigmoid(x):
    return 1.0 / (1.0 + jnp.exp(-x))


def _log_sigmoid(x):
    return jnp.minimum(x, 0.0) - jnp.log(1.0 + jnp.exp(-jnp.abs(x)))


def _mm_kernel(x_ref, w_ref, *rest, silu_in, has_bias):
    o_ref = rest[-1]
    x = x_ref[...]
    if silu_in:
        x = x * _sigmoid(x)
    acc = jnp.dot(x.astype(BF16), w_ref[...], preferred_element_type=F32)
    if has_bias:
        acc = acc + rest[0][...]
    o_ref[...] = acc.astype(o_ref.dtype)


def matmul(x, w, bias=None, silu_in=False, tm=512):
    m, k = x.shape
    n = w.shape[1]
    tm = min(tm, m)
    assert m % tm == 0
    in_specs = [pl.BlockSpec((tm, k), lambda i: (i, 0)), pl.BlockSpec((k, n), lambda i: (0, 0))]
    args = [x, w]
    if bias is not None:
        in_specs.append(pl.BlockSpec((1, n), lambda i: (0, 0)))
        args.append(bias.reshape(1, n))
    est = 2 * (tm * k * x.dtype.itemsize + k * n * 2 + tm * n * 4) + (4 << 20)
    return pl.pallas_call(
        functools.partial(_mm_kernel, silu_in=silu_in, has_bias=bias is not None),
        grid=(m // tm,),
        in_specs=in_specs,
        out_specs=pl.BlockSpec((tm, n), lambda i: (i, 0)),
        out_shape=jax.ShapeDtypeStruct((m, n), F32),
        compiler_params=_cparams(1, est),
        name="matmul",
    )(*args)


def _normmod_kernel(x_ref, g_ref, sh_ref, sc_ref, o_ref, *, mod):
    x = x_ref[...]
    y = x * lax.rsqrt(jnp.mean(x * x, axis=-1, keepdims=True) + NORM_EPS) * g_ref[...]
    if mod:
        y = y * (1.0 + sc_ref[...]) + sh_ref[...]
    o_ref[...] = y.astype(o_ref.dtype)


def normmod(x, g, shift, scale, out_dtype, mod=True, tm=512):
    b, l, d = x.shape
    tm = min(tm, l)
    assert l % tm == 0
    vec = pl.BlockSpec((None, 1, d), lambda bi, i: (bi, 0, 0))
    return pl.pallas_call(
        functools.partial(_normmod_kernel, mod=mod),
        grid=(b, l // tm),
        in_specs=[pl.BlockSpec((None, tm, d), lambda bi, i: (bi, i, 0)),
                  pl.BlockSpec((1, d), lambda bi, i: (0, 0)), vec, vec],
        out_specs=pl.BlockSpec((None, tm, d), lambda bi, i: (bi, i, 0)),
        out_shape=jax.ShapeDtypeStruct((b, l, d), out_dtype),
        compiler_params=_cparams(2),
        name="normmod",
    )(x, g.reshape(1, d), shift, scale)


def _assemble_rows(rows, n):
    t = rows[0].shape[1]
    rid = lax.broadcasted_iota(jnp.int32, (n, t), 0)
    out = jnp.zeros((n, t), rows[0].dtype)
    for r in range(n):
        out = jnp.where(rid == r, rows[r], out)
    return out


def _extract_topk_pairs(s, k, rowid=None):
    rows, t = s.shape
    tile = SUBLANES
    assert rows % (2 * tile) == 0
    row8 = lax.broadcasted_iota(jnp.int32, (tile, t), 0).astype(F32)
    up, up_id, low, low_id = [], [], [], []
    for v in range(rows // (2 * tile)):
        a = s[2 * v * tile:(2 * v + 1) * tile]
        b = s[(2 * v + 1) * tile:(2 * v + 2) * tile]
        if rowid is None:
            ida = row8 + float(2 * v * tile)
            idb = row8 + float((2 * v + 1) * tile)
        else:
            ida = rowid[2 * v * tile:(2 * v + 1) * tile]
            idb = rowid[(2 * v + 1) * tile:(2 * v + 2) * tile]
        first_wins = a >= b
        up.append(jnp.where(first_wins, a, b))
        low.append(jnp.where(first_wins, b, a))
        up_id.append(jnp.where(first_wins, ida, idb))
        low_id.append(jnp.where(first_wins, idb, ida))
    up, up_id = jnp.concatenate(up, axis=0), jnp.concatenate(up_id, axis=0)
    low, low_id = jnp.concatenate(low, axis=0), jnp.concatenate(low_id, axis=0)
    vals, ids = [], []
    for _ in range(k):
        m = jnp.max(up, axis=0, keepdims=True)
        first = jnp.min(jnp.where(up == m, up_id, 1e9), axis=0, keepdims=True)
        hit = up_id == first
        up = jnp.where(hit, low, up)
        up_id = jnp.where(hit, low_id, up_id)
        low = jnp.where(hit, -jnp.inf, low)
        vals.append(m)
        ids.append(first)
    return vals, ids


def _split_bf16(x):
    hi = x.astype(BF16)
    return hi, (x - hi.astype(F32)).astype(BF16)


def _dot3(a_hi, a_lo, b_hi, b_lo, dims):
    def one(a, b):
        return lax.dot_general(a, b, dims, preferred_element_type=F32)
    return one(a_hi, b_hi) + (one(a_hi, b_lo) + one(a_lo, b_hi))


def _peer_select_kernel(h_ref, wqh_ref, wql_ref, kh_ref, kl_ref, exp_ref, wgt_ref, q_ref):
    h_hi, h_lo = _split_bf16(h_ref[...])
    q_ref[...] = _dot3(h_hi, h_lo, wqh_ref[...], wql_ref[...], (((1,), (0,)), ((), ())))

    def lane_block(j, carry):
        r0 = pl.multiple_of(j * LANES, LANES)
        rows, weights = _select_tokens(q_ref[pl.ds(r0, LANES), :], kh_ref, kl_ref)
        exp_ref[pl.ds(r0, LANES), :] = rows
        wgt_ref[pl.ds(r0, LANES), :] = weights
        return carry

    lax.fori_loop(0, h_ref.shape[0] // LANES, lane_block, 0)


def _select_tokens(q, kh_ref, kl_ref):
    tt = q.shape[0]
    k = PEER_TOPK
    row8 = lax.broadcasted_iota(jnp.int32, (SUBLANES, tt), 0)
    row16 = lax.broadcasted_iota(jnp.int32, (2 * SUBLANES, tt), 0)
    exp_blocks, wgt_blocks = [], []
    for h in range(PEER_HEADS):
        tops = []
        for p in range(2):
            col = (h * 2 + p) * PEER_HALF
            q_hi, q_lo = _split_bf16(q[:, col:col + PEER_HALF])
            s = _dot3(kh_ref[h, p], kl_ref[h, p], q_hi, q_lo, (((1,), (1,)), ((), ())))
            vals, ids = _extract_topk_pairs(s, k)
            tops.append((_assemble_rows(vals, k), _assemble_rows(ids, k)))
        (a, ia), (b, ib) = tops
        blocks, flat = [a[0:1] + b], [row16.astype(F32)]
        for i in range(1, 8):
            nj = k // (i + 1)
            blocks.append(jnp.where(row8 < nj, a[i:i + 1] + b[0:8], -jnp.inf))
            flat.append((row8 + i * k).astype(F32))
        blocks.append(a[8:16] + b[0:1])
        flat.append(((row8 + 8) * k).astype(F32))
        cand = jnp.concatenate(blocks, axis=0)
        cand_id = jnp.concatenate(flat, axis=0)
        vals, ids = _extract_topk_pairs(cand, k, cand_id)
        best = _assemble_rows(vals, k)
        fl = _assemble_rows(ids, k)
        fi = jnp.floor(fl * (1.0 / k))
        fj = fl - fi * k
        ei = jnp.zeros_like(fl)
        ej = jnp.zeros_like(fl)
        for r in range(k):
            ei = jnp.where(fi == r, ia[r:r + 1], ei)
            ej = jnp.where(fj == r, ib[r:r + 1], ej)
        e = jnp.exp(best - jnp.max(best, axis=0, keepdims=True))
        wgt_blocks.append(e / jnp.sum(e, axis=0, keepdims=True))
        exp_blocks.append((ei * PEER_NKEYS + ej) * ROW_WORDS)
    return (jnp.concatenate(exp_blocks, axis=0).T.astype(jnp.int32),
            jnp.concatenate(wgt_blocks, axis=0).T)


def peer_select(h, wq, keys, tt=512):
    t, d = h.shape
    tt = min(tt, t)
    assert t % tt == 0 and tt % LANES == 0
    nq = wq.shape[1]
    wq_hi, wq_lo = _split_bf16(wq)
    k_hi, k_lo = _split_bf16(keys)
    wspec = pl.BlockSpec((d, nq), lambda i: (0, 0))
    kspec = pl.BlockSpec(keys.shape, lambda i: (0, 0, 0, 0))
    return pl.pallas_call(
        _peer_select_kernel,
        grid=(t // tt,),
        in_specs=[pl.BlockSpec((tt, d), lambda i: (i, 0)), wspec, wspec, kspec, kspec],
        out_specs=[pl.BlockSpec((tt, PEER_SEL), lambda i: (i, 0)),
                   pl.BlockSpec((tt, PEER_SEL), lambda i: (i, 0))],
        out_shape=[jax.ShapeDtypeStruct((t, PEER_SEL), jnp.int32),
                   jax.ShapeDtypeStruct((t, PEER_SEL), F32)],
        scratch_shapes=[pltpu.VMEM((tt, nq), F32)],
        compiler_params=_cparams(1, 2 * (2 * d * nq + 2 * keys.size) * 2 + 3 * tt * (d + nq) * 4 + (8 << 20)),
        name="peer_select",
    )(h, wq_hi, wq_lo, k_hi, k_lo)


ROW_WORDS = D_MODEL // 2 // LANES


def _unpack_pair(w):
    lo = lax.bitcast_convert_type(lax.shift_left(w, jnp.int32(16)), F32)
    hi = lax.bitcast_convert_type(jnp.bitwise_and(w, jnp.int32(-65536)), F32)
    return lo, hi


PEER_CHUNK = 128
PEER_NCHUNK = PEER_SEL // PEER_CHUNK
PEER_GROUP = 8


def _rows_to_tiles(x8):
    row8 = lax.broadcasted_iota(jnp.int32, (SUBLANES, LANES), 0)
    tiles = []
    for tk in range(SUBLANES):
        tile = jnp.zeros((SUBLANES, LANES), x8.dtype)
        for r in range(SUBLANES):
            tile = jnp.where(row8 == r, x8[tk:tk + 1, r * LANES:(r + 1) * LANES], tile)
        tiles.append(tile)
    return tiles


def _tiles_to_rows(tiles):
    row8 = lax.broadcasted_iota(jnp.int32, (SUBLANES, LANES), 0)
    cols = []
    for r in range(SUBLANES):
        col = jnp.zeros((SUBLANES, LANES), tiles[0].dtype)
        for tk in range(SUBLANES):
            col = jnp.where(row8 == tk, tiles[tk][r:r + 1, :], col)
        cols.append(col)
    return jnp.concatenate(cols, axis=1)


def _peer_act_kernel(idx_ref, x_ref, wgt_ref, tab_ref, coef_ref, p_ref, xs_ref):
    tt = x_ref.shape[0]

    def group(g, carry):
        t0 = pl.multiple_of(g * PEER_GROUP, PEER_GROUP)
        for tk, tile in enumerate(_rows_to_tiles(x_ref[pl.ds(t0, PEER_GROUP), :])):
            xs_ref[tk] = tile

        def token(tk, c):
            xl = xs_ref[tk, 0:ROW_WORDS, :]
            xh = xs_ref[tk, ROW_WORDS:2 * ROW_WORDS, :]

            def chunk(ci, c2):
                base = ((t0 + tk) * PEER_NCHUNK + ci) * PEER_CHUNK
                for j in range(PEER_CHUNK):
                    r = pl.multiple_of(idx_ref[base + j], ROW_WORDS)
                    lo, hi = _unpack_pair(tab_ref[pl.ds(r, ROW_WORDS), :])
                    p_ref[tk * PEER_NCHUNK + ci, j * ROW_WORDS:(j + 1) * ROW_WORDS, :] = lo * xl + hi * xh
                return c2

            return lax.fori_loop(0, PEER_NCHUNK, chunk, c)

        lax.fori_loop(0, PEER_GROUP, token, 0)
        for tk in range(PEER_GROUP):
            tok = p_ref.at[tk * PEER_NCHUNK:(tk + 1) * PEER_NCHUNK]
            parts = [tok[:, pl.ds(r, PEER_CHUNK, stride=ROW_WORDS), :].reshape(PEER_SEL, LANES)
                     for r in range(ROW_WORDS)]
            per_lane = (parts[0] + parts[1]) + (parts[2] + parts[3])
            act = jnp.sum(per_lane.T, axis=0, keepdims=True)
            coef_ref[t0 + tk] = wgt_ref[t0 + tk] * _gelu_tanh(act)
        return carry

    lax.fori_loop(0, tt // PEER_GROUP, group, 0)


def _peer_out_kernel(idx_ref, coef_ref, res_ref, gate_ref, tab_ref, out_ref, ys_ref):
    tt = out_ref.shape[0]
    zero = jnp.zeros((ROW_WORDS, LANES), F32)

    def group(g, carry):
        t0 = pl.multiple_of(g * PEER_GROUP, PEER_GROUP)

        def token(tk, c):
            def chunk(ci, acc):
                acc = list(acc)
                base = ((t0 + tk) * PEER_NCHUNK + ci) * PEER_CHUNK
                for j in range(PEER_CHUNK):
                    r = pl.multiple_of(idx_ref[base + j], ROW_WORDS)
                    lo, hi = _unpack_pair(tab_ref[pl.ds(r, ROW_WORDS), :])
                    cf = coef_ref[base + j]
                    acc[2 * (j % 2)] = acc[2 * (j % 2)] + cf * lo
                    acc[2 * (j % 2) + 1] = acc[2 * (j % 2) + 1] + cf * hi
                return tuple(acc)

            acc = lax.fori_loop(0, PEER_NCHUNK, chunk, (zero, zero, zero, zero))
            ys_ref[tk, 0:ROW_WORDS, :] = acc[0] + acc[2]
            ys_ref[tk, ROW_WORDS:2 * ROW_WORDS, :] = acc[1] + acc[3]
            return c

        lax.fori_loop(0, PEER_GROUP, token, 0)
        rows = _tiles_to_rows([ys_ref[tk] for tk in range(PEER_GROUP)])
        out_ref[pl.ds(t0, PEER_GROUP), :] = res_ref[pl.ds(t0, PEER_GROUP), :] + gate_ref[...] * rows
        return carry

    lax.fori_loop(0, tt // PEER_GROUP, group, 0)


def _pack_table(tab):
    e, d = tab.shape
    pairs = jnp.moveaxis(tab.astype(BF16).reshape(e, 2, d // 2), 1, 2)
    return lax.bitcast_convert_type(pairs, jnp.int32).reshape(e * ROW_WORDS, LANES)


def _table_spec(shape):
    return pl.BlockSpec(shape, lambda i: (0, 0), pipeline_mode=pl.Buffered(1))


def peer_experts(h, expert, weight, u_pack, v_pack, res, gate, tokens_per_gate, tt=256):
    t, d = h.shape
    assert t % tt == 0 and tokens_per_gate % tt == 0 and tt % PEER_GROUP == 0
    assert PEER_GROUP == SUBLANES and d == 2 * ROW_WORDS * LANES
    vmem = u_pack.size * 4 + (16 << 20)
    smem_flat = pl.BlockSpec((tt * PEER_SEL,), lambda i: (i,), memory_space=pltpu.SMEM)
    xspec = pl.BlockSpec((tt, d), lambda i: (i, 0))
    rowspec = pl.BlockSpec((tt, 1, PEER_SEL), lambda i: (i, 0, 0))
    tile_scratch = pltpu.VMEM((PEER_GROUP, SUBLANES, LANES), F32)
    rows = expert.reshape(t * PEER_SEL)
    coef = pl.pallas_call(
        _peer_act_kernel,
        grid=(t // tt,),
        in_specs=[smem_flat, xspec, rowspec, _table_spec(u_pack.shape)],
        out_specs=rowspec,
        out_shape=jax.ShapeDtypeStruct((t, 1, PEER_SEL), F32),
        scratch_shapes=[pltpu.VMEM((PEER_GROUP * PEER_NCHUNK, PEER_CHUNK * ROW_WORDS, LANES), F32),
                        tile_scratch],
        compiler_params=_cparams(1, vmem),
        name="peer_act",
    )(rows, h, weight.reshape(t, 1, PEER_SEL), u_pack)
    return pl.pallas_call(
        _peer_out_kernel,
        grid=(t // tt,),
        in_specs=[smem_flat, smem_flat, xspec,
                  pl.BlockSpec((None, 1, d), lambda i: (i * tt // tokens_per_gate, 0, 0)),
                  _table_spec(v_pack.shape)],
        out_specs=xspec,
        out_shape=jax.ShapeDtypeStruct((t, d), F32),
        scratch_shapes=[tile_scratch],
        compiler_params=_cparams(1, vmem),
        name="peer_out",
    )(rows, coef.reshape(t * PEER_SEL), res, gate, v_pack)


def peer_residual(x, gate, h, wq, keys, u_pack, v_pack):
    b, n, d = h.shape
    hf = h.reshape(b * n, d)
    expert, weight = peer_select(hf, wq, keys)
    out = peer_experts(hf, expert, weight, u_pack, v_pack, x.reshape(b * n, d), gate, n)
    return out.reshape(b, n, d)


RG_CONV = 4


def _shift_rows(cur, halo, k, reverse):
    tb = cur.shape[0]
    row8 = lax.broadcasted_iota(jnp.int32, (SUBLANES, cur.shape[1]), 0)
    if not reverse:
        rolled = pltpu.roll(cur, k, axis=0)
        first = jnp.where(row8 < k, pltpu.roll(halo, k, axis=0), rolled[0:SUBLANES])
        return jnp.concatenate([first, rolled[SUBLANES:]], axis=0)
    rolled = pltpu.roll(cur, tb - k, axis=0)
    last = jnp.where(row8 >= SUBLANES - k, pltpu.roll(halo, SUBLANES - k, axis=0),
                     rolled[tb - SUBLANES:])
    return jnp.concatenate([rolled[:tb - SUBLANES], last], axis=0)


def _rglru_kernel(u_ref, h0_ref, cw_ref, cb_ref, wa_ref, ba_ref, wx_ref, bx_ref, lam_ref, *rest,
                  reverse, has_acc):
    if has_acc:
        acc_ref, out_ref, hlast_ref, a_s, b_s, hp_s, halo_s = rest
    else:
        out_ref, hlast_ref, a_s, b_s, hp_s, halo_s = rest
    tb, c = u_ref.shape

    @pl.when(pl.program_id(1) == 0)
    def _():
        hp_s[...] = h0_ref[...]
        halo_s[...] = jnp.zeros_like(halo_s)

    cur = u_ref[...]
    halo = halo_s[...]
    xc = cb_ref[...] + cw_ref[RG_CONV - 1:RG_CONV, :] * cur
    for k in range(1, RG_CONV):
        xc = xc + cw_ref[RG_CONV - 1 - k:RG_CONV - k, :] * _shift_rows(cur, halo, k, reverse)
    halo_s[...] = cur[0:SUBLANES] if reverse else cur[tb - SUBLANES:]

    xb = xc.astype(BF16)
    gate_r = _sigmoid(jnp.dot(xb, wa_ref[...], preferred_element_type=F32) + ba_ref[...])
    gate_i = _sigmoid(jnp.dot(xb, wx_ref[...], preferred_element_type=F32) + bx_ref[...])
    lam = lam_ref[...]
    softplus_neg = jnp.maximum(-lam, 0.0) + jnp.log(1.0 + jnp.exp(-jnp.abs(lam)))
    a = jnp.exp(-RG_C * gate_r * softplus_neg)
    a_s[...] = a
    b_s[...] = jnp.sqrt(1.0 - a * a) * (gate_i * xc)

    row8 = lax.broadcasted_iota(jnp.int32, (SUBLANES, c), 0)
    nt = tb // SUBLANES

    def step(j, hp):
        jj = nt - 1 - j if reverse else j
        r0 = pl.multiple_of(jj * SUBLANES, SUBLANES)
        av = a_s[pl.ds(r0, SUBLANES), :]
        bv = b_s[pl.ds(r0, SUBLANES), :]
        for k in (1, 2, 4):
            if reverse:
                ok = row8 < SUBLANES - k
                sh = SUBLANES - k
            else:
                ok = row8 >= k
                sh = k
            a_prev = jnp.where(ok, pltpu.roll(av, sh, axis=0), 1.0)
            b_prev = jnp.where(ok, pltpu.roll(bv, sh, axis=0), 0.0)
            bv = av * b_prev + bv
            av = av * a_prev
        h = av * hp + bv
        if has_acc:
            out_ref[pl.ds(r0, SUBLANES), :] = h + acc_ref[pl.ds(r0, SUBLANES), :]
        else:
            out_ref[pl.ds(r0, SUBLANES), :] = h
        return h[0:1] if reverse else h[SUBLANES - 1:SUBLANES]

    hp = lax.fori_loop(0, nt, step, hp_s[...])
    hp_s[...] = hp
    hlast_ref[...] = hp


def rglru_scan(u, h0, p, reverse, acc=None, tb=512):
    b, n, c = u.shape
    tb = min(tb, n)
    assert n % tb == 0
    nblk = n // tb
    if reverse:
        seq = pl.BlockSpec((None, tb, c), lambda bi, i: (bi, nblk - 1 - i, 0))
    else:
        seq = pl.BlockSpec((None, tb, c), lambda bi, i: (bi, i, 0))
    state = pl.BlockSpec((None, 1, c), lambda bi, i: (bi, 0, 0))

    def par(a):
        return pl.BlockSpec(a.shape, lambda bi, i: (0,) * a.ndim)

    params = [p["conv_w"], p["conv_b"], p["wa"], p["ba"], p["wx"], p["bx"], p["lam"]]
    in_specs = [seq, state] + [par(a) for a in params]
    args = [u, h0] + params
    if acc is not None:
        in_specs.append(seq)
        args.append(acc)
    return pl.pallas_call(
        functools.partial(_rglru_kernel, reverse=reverse, has_acc=acc is not None),
        grid=(b, nblk),
        in_specs=in_specs,
        out_specs=[seq, state],
        out_shape=[jax.ShapeDtypeStruct((b, n, c), F32), jax.ShapeDtypeStruct((b, 1, c), F32)],
        scratch_shapes=[pltpu.VMEM((tb, c), F32), pltpu.VMEM((tb, c), F32),
                        pltpu.VMEM((1, c), F32), pltpu.VMEM((SUBLANES, c), F32)],
        compiler_params=_cparams(2),
        name="rglru_bwd" if reverse else "rglru_fwd",
    )(*args)


def _block_diag(w):
    g, bs, _ = w.shape
    eye = jnp.eye(g, dtype=w.dtype)
    return (eye[:, None, :, None] * w[:, :, None, :]).reshape(g * bs, g * bs)


def rglru_params(conv_w, conv_b, wa, ba, wx, bx, lam, d):
    c = conv_b.shape[-1]
    return {"conv_w": conv_w[d], "conv_b": conv_b[d].reshape(1, c),
            "wa": _block_diag(wa[d]).astype(BF16), "ba": ba[d].reshape(1, c),
            "wx": _block_diag(wx[d]).astype(BF16), "bx": bx[d].reshape(1, c),
            "lam": lam[d].reshape(1, c)}


def rglru_mix(u_ctx, u_lat, params):
    b, _, c = u_ctx.shape
    zero = jnp.zeros((b, 1, c), F32)
    h_ctx = h_lat = None
    for d in range(2):
        h_ctx, last = rglru_scan(u_ctx, zero, params[d], reverse=bool(d), acc=h_ctx)
        h_lat, _ = rglru_scan(u_lat, last, params[d], reverse=bool(d), acc=h_lat)
    return h_ctx, h_lat


def _gla_kernel(x_ref, s0_ref, wlr_ref, blr_ref, *rest, reverse, has_acc):
    if has_acc:
        acc_ref, o_ref, s_out_ref, st_s = rest
    else:
        o_ref, s_out_ref, st_s = rest
    nb, tb = x_ref.shape[0], x_ref.shape[1]
    ch = GLA_CHUNK

    @pl.when(pl.program_id(1) == 0)
    def _():
        st_s[...] = s0_ref[...]

    r_i = lax.broadcasted_iota(jnp.int32, (ch, ch), 0)
    c_i = lax.broadcasted_iota(jnp.int32, (ch, ch), 1)
    if reverse:
        cum_mat = (c_i >= r_i).astype(BF16)
        keep = c_i > r_i
    else:
        cum_mat = (c_i <= r_i).astype(BF16)
        keep = c_i <= r_i
    lane = lax.broadcasted_iota(jnp.int32, (1, LANES), 1)
    head_lanes = (lane < GLA_DK, lane >= GLA_DK)
    nt_dims = (((1,), (1,)), ((), ()))
    chunks = range(tb // ch)
    steps = [(cidx, bb) for cidx in (reversed(chunks) if reverse else chunks) for bb in range(nb)]
    for cidx, bb in steps:
        r0 = cidx * ch
        q = x_ref[bb, r0:r0 + ch, 0:GLA_QK] * (GLA_DK ** -0.5)
        k = x_ref[bb, r0:r0 + ch, GLA_QK:2 * GLA_QK]
        v = x_ref[bb, r0:r0 + ch, 2 * GLA_QK:2 * GLA_QK + GLA_V]
        lr = x_ref[bb, r0:r0 + ch, 2 * GLA_QK + GLA_V:GLA_IN_PAD]
        logits = jnp.dot(lr, wlr_ref[...], precision=HIGHEST, preferred_element_type=F32) + blr_ref[...]
        log_a = _log_sigmoid(logits) * (1.0 / GLA_TAU)
        la_hi = log_a.astype(BF16)
        la_mid, la_lo = _split_bf16(log_a - la_hi.astype(F32))
        cum = (jnp.dot(cum_mat, la_hi, preferred_element_type=F32)
               + (jnp.dot(cum_mat, la_mid, preferred_element_type=F32)
                  + jnp.dot(cum_mat, la_lo, preferred_element_type=F32)))
        tot = cum[0:1] if reverse else cum[ch - 1:ch]
        qg = q * jnp.exp(cum)
        kg = k * jnp.exp(-cum)
        kd = k * jnp.exp(tot - cum)
        decay = jnp.exp(tot)
        outs = []
        for h in range(GLA_HEADS):
            sl = slice((h // 2) * LANES, (h // 2 + 1) * LANES)
            mine = head_lanes[h % 2]
            qm = jnp.where(mine, qg[:, sl], 0.0).astype(BF16)
            scores = lax.dot_general(qm, kg[:, sl].astype(BF16), nt_dims, preferred_element_type=F32)
            scores = jnp.where(keep, scores, 0.0)
            vh = v[:, h * GLA_DV:(h + 1) * GLA_DV]
            st = st_s[bb, h]
            o = jnp.dot(scores.astype(BF16), vh.astype(BF16), preferred_element_type=F32)
            o = o + lax.dot_general(qm, st.astype(BF16), nt_dims, preferred_element_type=F32)
            kdm = jnp.where(mine, kd[:, sl], 0.0).astype(BF16)
            st_s[bb, h] = st * decay[:, sl] + jnp.dot(vh.T.astype(BF16), kdm, preferred_element_type=F32)
            outs.append(o)
        o_all = jnp.concatenate(outs, axis=1)
        if has_acc:
            o_all = o_all + acc_ref[bb, r0:r0 + ch, :]
        o_ref[bb, r0:r0 + ch, :] = o_all
    s_out_ref[...] = st_s[...]


def gla_scan(x, s0, wlr, blr, reverse, acc=None, tb=512, nb=2):
    b, n, cin = x.shape
    tb = min(tb, n)
    nb = min(nb, b)
    assert n % tb == 0 and tb % GLA_CHUNK == 0 and b % nb == 0
    nblk = n // tb

    def seq(width):
        if reverse:
            return pl.BlockSpec((nb, tb, width), lambda bi, i: (bi, nblk - 1 - i, 0))
        return pl.BlockSpec((nb, tb, width), lambda bi, i: (bi, i, 0))

    state = pl.BlockSpec((nb, GLA_HEADS, GLA_DV, LANES), lambda bi, i: (bi, 0, 0, 0))
    in_specs = [seq(cin), state,
                pl.BlockSpec(wlr.shape, lambda bi, i: (0, 0)), pl.BlockSpec(blr.shape, lambda bi, i: (0, 0))]
    args = [x, s0, wlr, blr]
    if acc is not None:
        in_specs.append(seq(GLA_V))
        args.append(acc)
    return pl.pallas_call(
        functools.partial(_gla_kernel, reverse=reverse, has_acc=acc is not None),
        grid=(b // nb, nblk),
        in_specs=in_specs,
        out_specs=[seq(GLA_V), state],
        out_shape=[jax.ShapeDtypeStruct((b, n, GLA_V), F32),
                   jax.ShapeDtypeStruct((b, GLA_HEADS, GLA_DV, LANES), F32)],
        scratch_shapes=[pltpu.VMEM((nb, GLA_HEADS, GLA_DV, LANES), F32)],
        compiler_params=_cparams(2, 40 << 20),
        name="gla_bwd" if reverse else "gla_fwd",
    )(*args)


def gla_params(w_lr, b_lr, d):
    w = jnp.zeros((LANES, GLA_QK), F32).at[d * GLA_RANK:(d + 1) * GLA_RANK].set(w_lr[d])
    return w, b_lr[d].reshape(1, GLA_QK)


def gla_mix(x_ctx, x_lat_cols, w_lr, b_lr):
    b = x_ctx.shape[0]
    zero = jnp.zeros((b, GLA_HEADS, GLA_DV, LANES), F32)
    o_ctx = o_lat = None
    for d in range(2):
        w, bias = gla_params(w_lr, b_lr, d)
        o_ctx, s = gla_scan(x_ctx, zero, w, bias, reverse=bool(d), acc=o_ctx)
        o_lat, _ = gla_scan(x_lat_cols, s, w, bias, reverse=bool(d), acc=o_lat)
    return o_ctx, o_lat


def raster_to_column(t):
    b, n = t.shape[:2]
    return t.reshape(b, n // GRID_W, GRID_W, *t.shape[2:]).swapaxes(1, 2).reshape(t.shape)


def column_to_raster(t):
    b, n = t.shape[:2]
    return t.reshape(b, GRID_W, n // GRID_W, *t.shape[2:]).swapaxes(1, 2).reshape(t.shape)


def _merge_kernel(x_ref, gt_ref, h_ref, yhy_ref, hrg_ref, ogla_ref,
                  wg_ref, bm_ref, gn_ref, why_ref, wrg_ref, wgla_ref, wout_ref, o_ref):
    d = x_ref.shape[1]

    def proj(y, w_ref):
        return jnp.dot(y.astype(BF16), w_ref[...], preferred_element_type=F32)

    gates_in = proj(h_ref[...], wg_ref)
    y_rg = hrg_ref[...] * _gelu_tanh(gates_in[:, 0:RG_W])
    o = ogla_ref[...]
    heads = []
    for h in range(GLA_HEADS):
        oh = o[:, h * GLA_DV:(h + 1) * GLA_DV]
        heads.append(oh * lax.rsqrt(jnp.mean(oh * oh, axis=-1, keepdims=True) + NORM_EPS) * gn_ref[...])
    gg = gates_in[:, RG_W:RG_W + GLA_V]
    y_gla = jnp.concatenate(heads, axis=1) * (gg * _sigmoid(gg))
    gate = _sigmoid(gates_in[:, RG_W + GLA_V:] + bm_ref[...])
    m = (gate[:, 0:d] * proj(yhy_ref[...], why_ref) + gate[:, d:2 * d] * proj(y_rg, wrg_ref)
         + gate[:, 2 * d:3 * d] * proj(y_gla, wgla_ref))
    o_ref[...] = x_ref[...] + gt_ref[...] * proj(m, wout_ref)


def merge_residual(x, gt, h, y_hy, h_rg, o_gla, lw, tm=256):
    b, n, d = x.shape
    tm = min(tm, n)
    assert n % tm == 0

    def seq(a):
        return pl.BlockSpec((None, tm, a.shape[2]), lambda bi, i: (bi, i, 0))

    def par(a):
        return pl.BlockSpec(a.shape, lambda bi, i: (0,) * a.ndim, pipeline_mode=pl.Buffered(1))

    streams = [h, y_hy, h_rg, o_gla]
    params = [lw["w_gates"], lw["b_merge"], lw["gla_norm_g"], lw["w_hy_o"], lw["w_rg_o"],
              lw["w_gla_o"], lw["w_out"]]
    return pl.pallas_call(
        _merge_kernel,
        grid=(b, n // tm),
        in_specs=[seq(x), pl.BlockSpec((None, 1, d), lambda bi, i: (bi, 0, 0))]
                 + [seq(a) for a in streams] + [par(a) for a in params],
        out_specs=seq(x),
        out_shape=jax.ShapeDtypeStruct((b, n, d), F32),
        compiler_params=_cparams(2, 48 << 20),
        name="merge_residual",
    )(x, gt, *streams, *params)


def _shortconv_kernel(x_ref, w_ref, b_ref, o_ref, *, rows):
    n, c = x_ref.shape
    zero = jnp.zeros((SUBLANES, c), F32)
    for r0 in range(0, n, rows):
        cur = x_ref[r0:r0 + rows, :]
        before = x_ref[r0 - SUBLANES:r0, :] if r0 > 0 else zero
        after = x_ref[r0 + rows:r0 + rows + SUBLANES, :] if r0 + rows < n else zero
        o_ref[r0:r0 + rows, :] = (b_ref[...] + w_ref[0:1, :] * _shift_rows(cur, before, 1, False)
                                  + w_ref[1:2, :] * cur + w_ref[2:3, :] * _shift_rows(cur, after, 1, True))


def shortconv(x, w, bias):
    b, n, c = x.shape
    blk = pl.BlockSpec((None, n, LANES), lambda bi, j: (bi, 0, j))
    return pl.pallas_call(
        functools.partial(_shortconv_kernel, rows=min(n, 1024)),
        grid=(b, c // LANES),
        in_specs=[blk, pl.BlockSpec((3, LANES), lambda bi, j: (0, j)),
                  pl.BlockSpec((1, LANES), lambda bi, j: (0, j))],
        out_specs=blk,
        out_shape=jax.ShapeDtypeStruct((b, n, c), F32),
        compiler_params=_cparams(2, 40 << 20),
        name="shortconv",
    )(x, w, bias.reshape(1, c))


def _hy_taps_kernel(w1_ref, b1_ref, w2_ref, b2_ref, w3_ref, freq_ref, band_ref, delta_ref,
                    taps_ref, asum_ref, *, n):
    tb = taps_ref.shape[1]
    i = pl.program_id(0)
    m = i * tb + lax.broadcasted_iota(jnp.int32, (tb, LANES), 0)
    idx = jnp.where(m < n, m, 2 * n - m).astype(F32)
    lane = lax.broadcasted_iota(jnp.int32, (tb, LANES), 1)
    tn = idx / (n - 1)
    ang = (2.0 * math.pi / n) * idx * band_ref[...]
    feats = jnp.where(lane == 0, tn,
                      jnp.where(lane <= HY_BANDS, jnp.cos(ang),
                                jnp.where(lane <= 2 * HY_BANDS, -jnp.sin(ang), 0.0)))
    fr = freq_ref[...]

    def dense(x, w_ref):
        return jnp.dot(x, w_ref[...], precision=HIGHEST, preferred_element_type=F32)

    h = jnp.sin(fr * (dense(feats, w1_ref) + b1_ref[...]))
    h = jnp.sin(fr * (dense(h, w2_ref) + b2_ref[...]))
    h = dense(h, w3_ref) * jnp.exp(-tn[:, 0:1] * delta_ref[...])
    row = m[:, 0:1]
    use_fwd = row < n
    use_bwd = (row > n) | (row == 0)
    sums = []
    for o in range(2):
        hf = jnp.where(use_fwd, h[:, (2 * o) * HY_W:(2 * o + 1) * HY_W], 0.0)
        hb = jnp.where(use_bwd, h[:, (2 * o + 1) * HY_W:(2 * o + 2) * HY_W], 0.0)
        taps_ref[o] = hf + hb
        sums += [jnp.sum(jnp.abs(hf), axis=0, keepdims=True), jnp.sum(jnp.abs(hb), axis=0, keepdims=True)]

    @pl.when(i == 0)
    def _():
        asum_ref[...] = jnp.zeros_like(asum_ref)

    asum_ref[...] += jnp.concatenate(sums, axis=1)


def hyena_taps(n, w1, b1, w2, b2, w3, freq, tb=256):
    nout = w3.shape[1]
    assert nout == 4 * HY_W and (2 * n) % tb == 0
    bands = np.zeros((1, LANES), np.float32)
    lin = np.linspace(1e-4, HY_BANDS - 1, HY_BANDS, dtype=np.float32)
    bands[0, 1:1 + HY_BANDS] = lin
    bands[0, 1 + HY_BANDS:1 + 2 * HY_BANDS] = lin
    deltas = np.abs(np.linspace(HY_FAST_RATE, HY_SLOW_RATE, HY_W, dtype=np.float32))
    deltas = np.tile(deltas, nout // HY_W).reshape(1, nout)
    w1p = jnp.zeros((LANES, HY_FFN), F32).at[:w1.shape[0]].set(w1)
    params = [w1p, b1.reshape(1, -1), w2, b2.reshape(1, -1), w3, freq.reshape(1, -1),
              jnp.asarray(bands), jnp.asarray(deltas)]
    return pl.pallas_call(
        functools.partial(_hy_taps_kernel, n=n),
        grid=(2 * n // tb,),
        in_specs=[pl.BlockSpec(a.shape, lambda i: (0, 0)) for a in params],
        out_specs=[pl.BlockSpec((2, tb, HY_W), lambda i: (0, i, 0)), pl.BlockSpec((1, nout), lambda i: (0, 0))],
        out_shape=[jax.ShapeDtypeStruct((2, 2 * n, HY_W), F32), jax.ShapeDtypeStruct((1, nout), F32)],
        compiler_params=_cparams(1),
        name="hyena_taps",
    )(*params)


def _cis(num, den):
    ang = (2.0 * math.pi / den) * (num % den).astype(F32)
    return jnp.cos(ang), jnp.sin(ang)


def _dft_tables(n):
    big = 2 * n
    q = int(round(math.sqrt(big)))
    assert q * q == big and q % (2 * SUBLANES) == 0
    ar = jnp.arange(q, dtype=jnp.int32)
    num = ar[None, :, None] * (q * ar[None, None, :] + ar[:, None, None])
    c, s = _cis(num, big)
    w1 = jnp.concatenate([c, -s], axis=1)
    ct, st = jnp.swapaxes(c, 1, 2)[:, :q // 2], jnp.swapaxes(s, 1, 2)[:, :q // 2]
    v = jnp.concatenate([ct, -st], axis=2) * (1.0 / big)
    c2, s2 = _cis(ar[:, None] * ar[None, :], q)
    f2 = jnp.concatenate([jnp.concatenate([c2, s2], axis=1), jnp.concatenate([-s2, c2], axis=1)], axis=0)
    g2 = jnp.concatenate([jnp.concatenate([c2, -s2], axis=1), jnp.concatenate([s2, c2], axis=1)], axis=0)
    return {"q": q, "w1": w1, "v": v, "f2": f2, "g2": g2}


def _level_kernel(w_ref, x_ref, *rest, has_gate, precise):
    if has_gate:
        src_ref, gate_ref, skip_ref = rest[:3]
    o_ref = rest[-1]
    for s in range(w_ref.shape[0]):
        x = x_ref[s]
        if precise:
            y = jnp.dot(w_ref[s], x, precision=HIGHEST, preferred_element_type=F32)
        else:
            y = jnp.dot(w_ref[s], x.astype(BF16), preferred_element_type=F32)
        if has_gate:
            y = gate_ref[s] * (y + src_ref[s] * skip_ref[...])
        o_ref[s] = y.astype(o_ref.dtype)


def dft_level(w, x, x_group=0, gate=None, out_dtype=F32, precise=False, sb=8):
    b, q, k, _ = x.shape
    m = w.shape[1]
    c = HY_W
    sb = min(sb, q)

    def seq(rows, group):
        return pl.BlockSpec((None, sb, rows, c), lambda i, bi: (bi, i, 0, group))

    in_specs = [pl.BlockSpec((sb, m, k), lambda i, bi: (i, 0, 0)), seq(k, x_group)]
    args = [w, x]
    if gate is not None:
        src, src_group, gates, gate_group, skip = gate
        in_specs += [seq(m, src_group), seq(m, gate_group), pl.BlockSpec((1, c), lambda i, bi: (0, 0))]
        args += [src, gates, skip]
    return pl.pallas_call(
        functools.partial(_level_kernel, has_gate=gate is not None, precise=precise),
        grid=(q // sb, b),
        in_specs=in_specs,
        out_specs=seq(m, 0),
        out_shape=jax.ShapeDtypeStruct((b, q, m, c), out_dtype),
        compiler_params=_cparams(2, 40 << 20),
        name="dft_level",
    )(*args)


def _dft_mid_kernel(a_ref, h_ref, f_ref, g_ref, o_ref):
    q = a_ref.shape[1] // 2
    for j in range(a_ref.shape[0]):
        x = jnp.dot(f_ref[...], a_ref[j], preferred_element_type=F32)
        xr, xi = x[:q], x[q:]
        hr, hi = h_ref[j, :q, :], h_ref[j, q:, :]
        y = jnp.concatenate([xr * hr - xi * hi, xr * hi + xi * hr], axis=0)
        o_ref[j] = jnp.dot(g_ref[...], y.astype(BF16), preferred_element_type=F32).astype(o_ref.dtype)


def dft_mid(a, spec, f2, g2, kb=8):
    b, q, q2, c = a.shape
    blk = pl.BlockSpec((None, kb, q2, c), lambda i, bi: (bi, i, 0, 0))
    mat = pl.BlockSpec((q2, q2), lambda i, bi: (0, 0))
    return pl.pallas_call(
        _dft_mid_kernel,
        grid=(q // kb, b),
        in_specs=[blk, pl.BlockSpec((kb, q2, c), lambda i, bi: (i, 0, 0)), mat, mat],
        out_specs=blk,
        out_shape=jax.ShapeDtypeStruct(a.shape, BF16),
        compiler_params=_cparams(2, 40 << 20),
        name="dft_mid",
    )(a, spec, f2, g2)


def _dft_spec_kernel(a_ref, f_ref, asum_ref, o_ref):
    inv = 1.0 / (asum_ref[0:1, :] + asum_ref[1:2, :] + 1e-6)
    for j in range(a_ref.shape[0]):
        o_ref[j] = jnp.dot(f_ref[...], a_ref[j], precision=HIGHEST, preferred_element_type=F32) * inv


def dft_spec(a, f2, asum, kb=8):
    o, q, q2, c = a.shape
    blk = pl.BlockSpec((None, kb, q2, c), lambda i, oi: (oi, i, 0, 0))
    return pl.pallas_call(
        _dft_spec_kernel,
        grid=(q // kb, o),
        in_specs=[blk, pl.BlockSpec((q2, q2), lambda i, oi: (0, 0)),
                  pl.BlockSpec((None, 2, c), lambda i, oi: (oi, 0, 0))],
        out_specs=blk,
        out_shape=jax.ShapeDtypeStruct(a.shape, F32),
        compiler_params=_cparams(2, 40 << 20),
        name="dft_spec",
    )(a, f2, asum)


def _swap_levels(a):
    b, q, q2, c = a.shape
    return a.reshape(b, q, 2, q, c).transpose(0, 3, 2, 1, 4).reshape(b, q, q2, c)


def hyena_long(u, taps, asum, skip, tabs):
    b, n, _ = u.shape
    q = tabs["q"]
    c = HY_W

    def to_levels(t, rows):
        return t.reshape(t.shape[0], rows, q, t.shape[2]).swapaxes(1, 2)

    a = dft_level(tabs["w1"], to_levels(taps, q), precise=True)
    spec = dft_spec(_swap_levels(a), tabs["f2"], asum.reshape(2, 2, c))
    w1d = tabs["w1"][:, :, :q // 2].astype(BF16)
    vd = tabs["v"].astype(BF16)
    f2, g2 = tabs["f2"].astype(BF16), tabs["g2"].astype(BF16)
    u_t = to_levels(u, q // 2)
    src, group = u_t, 0
    for order in range(2):
        a = dft_level(w1d, src, x_group=group, out_dtype=BF16)
        cm = dft_mid(_swap_levels(a), spec[order], f2, g2)
        src = dft_level(vd, _swap_levels(cm),
                        gate=(src, group, u_t, order + 1, skip[order].reshape(1, c)))
        group = 0
    return src.swapaxes(1, 2).reshape(b, n, c)


def _dft_small_kernel(u_ref, gate_ref, skip_ref, h_ref, f_ref, g_ref, o_ref):
    u = u_ref[...]
    nb = h_ref.shape[0] // 2
    x = jnp.dot(f_ref[...], u, precision=HIGHEST, preferred_element_type=F32)
    xr, xi = x[:nb], x[nb:]
    hr, hi = h_ref[:nb, :], h_ref[nb:, :]
    y = jnp.concatenate([xr * hr - xi * hi, xr * hi + xi * hr], axis=0)
    conv = jnp.dot(g_ref[...], y, precision=HIGHEST, preferred_element_type=F32)
    o_ref[...] = gate_ref[...] * (conv + u * skip_ref[...])


def _dft_small_spec_kernel(taps_ref, f_ref, asum_ref, o_ref):
    inv = 1.0 / (asum_ref[0:1, :] + asum_ref[1:2, :] + 1e-6)
    o_ref[...] = jnp.dot(f_ref[...], taps_ref[...], precision=HIGHEST,
                         preferred_element_type=F32) * inv


def hyena_short_seq(u, taps, asum, skip):
    b, n, _ = u.shape
    big = 2 * n
    c = HY_W
    ar = jnp.arange(big, dtype=jnp.int32)
    cs, sn = _cis(ar[:, None] * ar[None, :], big)
    f_full = jnp.concatenate([cs, -sn], axis=0)
    g_half = jnp.concatenate([cs[:n], -sn[:n]], axis=1) * (1.0 / big)
    tap = pl.BlockSpec((None, big, c), lambda o: (o, 0, 0))
    spec = pl.pallas_call(
        _dft_small_spec_kernel,
        grid=(2,),
        in_specs=[tap, pl.BlockSpec((2 * big, big), lambda o: (0, 0)),
                  pl.BlockSpec((None, 2, c), lambda o: (o, 0, 0))],
        out_specs=pl.BlockSpec((None, 2 * big, c), lambda o: (o, 0, 0)),
        out_shape=jax.ShapeDtypeStruct((2, 2 * big, c), F32),
        compiler_params=_cparams(1),
        name="dft_small_spec",
    )(taps, f_full, asum.reshape(2, 2, c))
    f_data = f_full[:, :n]
    src, group = u, 0
    for order in range(2):
        src = pl.pallas_call(
            _dft_small_kernel,
            grid=(b,),
            in_specs=[pl.BlockSpec((None, n, c), functools.partial(lambda bi, g: (bi, 0, g), g=group)),
                      pl.BlockSpec((None, n, c), functools.partial(lambda bi, g: (bi, 0, g), g=order + 1)),
                      pl.BlockSpec((1, c), lambda bi: (0, 0)),
                      pl.BlockSpec((2 * big, c), lambda bi: (0, 0)),
                      pl.BlockSpec((2 * big, n), lambda bi: (0, 0)),
                      pl.BlockSpec((n, 2 * big), lambda bi: (0, 0))],
            out_specs=pl.BlockSpec((None, n, c), lambda bi: (bi, 0, 0)),
            out_shape=jax.ShapeDtypeStruct((b, n, c), F32),
            compiler_params=_cparams(1),
            name="dft_small",
        )(src, u, skip[order].reshape(1, c), spec[order], f_data, g_half)
        group = 0
    return src


HY_IN = 3 * HY_W
IN_GROUPS = (HY_IN, RG_W, RG_W, GLA_IN, GLA_V, 3 * D_MODEL)


def _split_w_in(w):
    parts, start = [], 0
    for width in IN_GROUPS:
        parts.append(w[:, start:start + width])
        start += width
    parts[3] = jnp.pad(parts[3], ((0, 0), (0, GLA_IN_PAD - GLA_IN)))
    return [p.astype(BF16) for p in parts]


def _project(h, w):
    b, n, d = h.shape
    return matmul(h.reshape(b * n, d), w).reshape(b, n, w.shape[1])


def kernel(x, c, ctx, c_ctx, w_mod, b_mod, g_norm_mix, g_norm_ffn, w_in, hy_conv_w, hy_conv_b,
           hy_w1, hy_b1, hy_w2, hy_b2, hy_w3, hy_freq, hy_skip, rg_conv_w, rg_conv_b, rg_wa, rg_ba,
           rg_wx, rg_bx, rg_lambda, gla_w_lr, gla_b_lr, gla_norm_g, w_hy_o, w_rg_o, w_gla_o, b_merge,
           w_out, peer_wq, peer_keys, peer_u, peer_v, g_final):
    b, n, d = x.shape
    n_ctx = ctx.shape[1]
    depth = w_mod.shape[0]
    cond = jnp.concatenate([c, c_ctx[None, :]], axis=0)
    cond = jnp.pad(cond, ((0, -(b + 1) % SUBLANES), (0, 0)))
    tabs = _dft_tables(n)
    x_lat, x_ctx = x, ctx
    for l in range(depth):
        need_ctx = l < depth - 1
        mod = matmul(cond, w_mod[l].astype(BF16), bias=b_mod[l], silu_in=True)
        sh1, sc1, gt1, sh2, sc2, gt2 = [mod[:b, i * d:(i + 1) * d].reshape(b, 1, d) for i in range(6)]
        csh1, csc1, cgt1, csh2, csc2, cgt2 = [
            jnp.broadcast_to(mod[b:b + 1, i * d:(i + 1) * d].reshape(1, 1, d), (b, 1, d)) for i in range(6)]
        w_hy, w_rgx, w_rgg, w_gla, w_glag, w_mg = _split_w_in(w_in[l])
        lw = {"b_merge": b_merge[l].reshape(1, 3 * d), "gla_norm_g": gla_norm_g[l].reshape(1, GLA_DV),
              "w_gates": jnp.concatenate([w_rgg, w_glag, w_mg], axis=1),
              "w_hy_o": w_hy_o[l].astype(BF16), "w_rg_o": w_rg_o[l].astype(BF16),
              "w_gla_o": w_gla_o[l].astype(BF16), "w_out": w_out[l].astype(BF16)}
        rg_par = [rglru_params(rg_conv_w[l], rg_conv_b[l], rg_wa[l], rg_ba[l], rg_wx[l], rg_bx[l],
                               rg_lambda[l], dd) for dd in range(2)]
        filt = (hy_w1[l], hy_b1[l], hy_w2[l], hy_b2[l], hy_w3[l], hy_freq[l])
        u_pack, v_pack = _pack_table(peer_u[l]), _pack_table(peer_v[l])

        h_lat = normmod(x_lat, g_norm_mix[l], sh1, sc1, BF16)
        h_ctx = normmod(x_ctx, g_norm_mix[l], csh1, csc1, BF16)
        hy_l, rgx_l, gla_l = [_project(h_lat, w) for w in (w_hy, w_rgx, w_gla)]
        rgx_c, gla_c = [_project(h_ctx, w) for w in (w_rgx, w_gla)]

        taps, asum = hyena_taps(n, *filt)
        y_hy_l = hyena_long(shortconv(hy_l, hy_conv_w[l], hy_conv_b[l]), taps, asum, hy_skip[l], tabs)
        h_rg_c, h_rg_l = rglru_mix(rgx_c, rgx_l, rg_par)
        o_gla_c, o_gla_l = gla_mix(gla_c, raster_to_column(gla_l), gla_w_lr[l], gla_b_lr[l])
        x_lat = merge_residual(x_lat, gt1, h_lat, y_hy_l, h_rg_l, column_to_raster(o_gla_l), lw)
        h2 = normmod(x_lat, g_norm_ffn[l], sh2, sc2, F32)
        x_lat = peer_residual(x_lat, gt2, h2, peer_wq[l], peer_keys[l], u_pack, v_pack)
        if need_ctx:
            taps_c, asum_c = hyena_taps(n_ctx, *filt)
            y_hy_c = hyena_short_seq(shortconv(_project(h_ctx, w_hy), hy_conv_w[l], hy_conv_b[l]),
                                     taps_c, asum_c, hy_skip[l])
            x_ctx = merge_residual(x_ctx, cgt1, h_ctx, y_hy_c, h_rg_c, o_gla_c, lw)
            h2c = normmod(x_ctx, g_norm_ffn[l], csh2, csc2, F32)
            x_ctx = peer_residual(x_ctx, cgt2, h2c, peer_wq[l], peer_keys[l], u_pack, v_pack)
    zero = jnp.zeros((b, 1, d), F32)
    return normmod(x_lat, g_final, zero, zero, F32, mod=False)
```

```python
import functools
import math

import numpy as np
import jax
import jax.numpy as jnp
from jax import lax
from jax.experimental import pallas as pl
from jax.experimental.pallas import tpu as pltpu

F32 = jnp.float32
BF16 = jnp.bfloat16
HIGHEST = lax.Precision.HIGHEST

D_MODEL = 1024
DEPTH = 4
GRID_W = 64
NORM_EPS = 1e-6

HY_W = 512
HY_BANDS = 16
HY_FFN = 64
HY_FAST_RATE = math.log(1e-2) / 0.3
HY_SLOW_RATE = math.log(1e-2) / 1.5

RG_W = 512
RG_BLOCKS = 8
RG_C = 8.0

GLA_HEADS = 4
GLA_DK = 64
GLA_DV = 128
GLA_QK = GLA_HEADS * GLA_DK
GLA_V = GLA_HEADS * GLA_DV
GLA_RANK = 16
GLA_TAU = 16.0
GLA_CHUNK = 64
GLA_IN = 2 * GLA_QK + GLA_V + 2 * GLA_RANK
GLA_IN_PAD = 2 * GLA_QK + GLA_V + 128

PEER_HEADS = 8
PEER_NKEYS = 128
PEER_TOPK = 16
PEER_HALF = 128
PEER_SEL = PEER_HEADS * PEER_TOPK

V7X_VMEM_BYTES = 64 * 1024 * 1024
SUBLANES = 8
LANES = 128


def _cparams(n_grid, vmem_bytes=None):
    kw = dict(dimension_semantics=("arbitrary",) * n_grid)
    if vmem_bytes is not None:
        assert vmem_bytes < V7X_VMEM_BYTES
        kw["vmem_limit_bytes"] = int(vmem_bytes)
    return pltpu.CompilerParams(**kw)


def _gelu_tanh(x):
    return 0.5 * x * (1.0 + jnp.tanh(math.sqrt(2.0 / math.pi) * (x + 0.044715 * (x * x * x))))


def _sigmoid(x):
    return 1.0 / (1.0 + jnp.exp(-x))


def _log_sigmoid(x):
    return jnp.minimum(x, 0.0) - jnp.log(1.0 + jnp.exp(-jnp.abs(x)))


def _mm_kernel(x_ref, w_ref, *rest, silu_in, has_bias):
    o_ref = rest[-1]
    x = x_ref[...]
    if silu_in:
        x = x * _sigmoid(x)
    acc = jnp.dot(x.astype(BF16), w_ref[...], preferred_element_type=F32)
    if has_bias:
        acc = acc + rest[0][...]
    o_ref[...] = acc.astype(o_ref.dtype)


def matmul(x, w, bias=None, silu_in=False, tm=512):
    m, k = x.shape
    n = w.shape[1]
    tm = min(tm, m)
    assert m % tm == 0
    in_specs = [pl.BlockSpec((tm, k), lambda i: (i, 0)), pl.BlockSpec((k, n), lambda i: (0, 0))]
    args = [x, w]
    if bias is not None:
        in_specs.append(pl.BlockSpec((1, n), lambda i: (0, 0)))
        args.append(bias.reshape(1, n))
    est = 2 * (tm * k * x.dtype.itemsize + k * n * 2 + tm * n * 4) + (4 << 20)
    return pl.pallas_call(
        functools.partial(_mm_kernel, silu_in=silu_in, has_bias=bias is not None),
        grid=(m // tm,),
        in_specs=in_specs,
        out_specs=pl.BlockSpec((tm, n), lambda i: (i, 0)),
        out_shape=jax.ShapeDtypeStruct((m, n), F32),
        compiler_params=_cparams(1, est),
        name="matmul",
    )(*args)


def _normmod_kernel(x_ref, g_ref, sh_ref, sc_ref, o_ref, *, mod):
    x = x_ref[...]
    y = x * lax.rsqrt(jnp.mean(x * x, axis=-1, keepdims=True) + NORM_EPS) * g_ref[...]
    if mod:
        y = y * (1.0 + sc_ref[...]) + sh_ref[...]
    o_ref[...] = y.astype(o_ref.dtype)


def normmod(x, g, shift, scale, out_dtype, mod=True, tm=512):
    b, l, d = x.shape
    tm = min(tm, l)
    assert l % tm == 0
    vec = pl.BlockSpec((None, 1, d), lambda bi, i: (bi, 0, 0))
    return pl.pallas_call(
        functools.partial(_normmod_kernel, mod=mod),
        grid=(b, l // tm),
        in_specs=[pl.BlockSpec((None, tm, d), lambda bi, i: (bi, i, 0)),
                  pl.BlockSpec((1, d), lambda bi, i: (0, 0)), vec, vec],
        out_specs=pl.BlockSpec((None, tm, d), lambda bi, i: (bi, i, 0)),
        out_shape=jax.ShapeDtypeStruct((b, l, d), out_dtype),
        compiler_params=_cparams(2),
        name="normmod",
    )(x, g.reshape(1, d), shift, scale)


def _assemble_rows(rows, n):
    t = rows[0].shape[1]
    rid = lax.broadcasted_iota(jnp.int32, (n, t), 0)
    out = jnp.zeros((n, t), rows[0].dtype)
    for r in range(n):
        out = jnp.where(rid == r, rows[r], out)
    return out


def _extract_topk_pairs(s, k, rowid=None):
    rows, t = s.shape
    tile = SUBLANES
    assert rows % (2 * tile) == 0
    row8 = lax.broadcasted_iota(jnp.int32, (tile, t), 0).astype(F32)
    up, up_id, low, low_id = [], [], [], []
    for v in range(rows // (2 * tile)):
        a = s[2 * v * tile:(2 * v + 1) * tile]
        b = s[(2 * v + 1) * tile:(2 * v + 2) * tile]
        if rowid is None:
            ida = row8 + float(2 * v * tile)
            idb = row8 + float((2 * v + 1) * tile)
        else:
            ida = rowid[2 * v * tile:(2 * v + 1) * tile]
            idb = rowid[(2 * v + 1) * tile:(2 * v + 2) * tile]
        first_wins = a >= b
        up.append(jnp.where(first_wins, a, b))
        low.append(jnp.where(first_wins, b, a))
        up_id.append(jnp.where(first_wins, ida, idb))
        low_id.append(jnp.where(first_wins, idb, ida))
    up, up_id = jnp.concatenate(up, axis=0), jnp.concatenate(up_id, axis=0)
    low, low_id = jnp.concatenate(low, axis=0), jnp.concatenate(low_id, axis=0)
    vals, ids = [], []
    for _ in range(k):
        m = jnp.max(up, axis=0, keepdims=True)
        first = jnp.min(jnp.where(up == m, up_id, 1e9), axis=0, keepdims=True)
        hit = up_id == first
        up = jnp.where(hit, low, up)
        up_id = jnp.where(hit, low_id, up_id)
        low = jnp.where(hit, -jnp.inf, low)
        vals.append(m)
        ids.append(first)
    return vals, ids


def _split_bf16(x):
    hi = x.astype(BF16)
    return hi, (x - hi.astype(F32)).astype(BF16)


def _dot3(a_hi, a_lo, b_hi, b_lo, dims):
    def one(a, b):
        return lax.dot_general(a, b, dims, preferred_element_type=F32)
    return one(a_hi, b_hi) + (one(a_hi, b_lo) + one(a_lo, b_hi))


def _peer_select_kernel(h_ref, wqh_ref, wql_ref, kh_ref, kl_ref, exp_ref, wgt_ref, q_ref):
    h_hi, h_lo = _split_bf16(h_ref[...])
    q_ref[...] = _dot3(h_hi, h_lo, wqh_ref[...], wql_ref[...], (((1,), (0,)), ((), ())))

    def lane_block(j, carry):
        for half in range(2):
            r0 = pl.multiple_of((2 * j + half) * LANES, LANES)
            rows, weights = _select_tokens(q_ref[pl.ds(r0, LANES), :], kh_ref, kl_ref)
            exp_ref[pl.ds(r0, LANES), :] = rows
            wgt_ref[pl.ds(r0, LANES), :] = weights
        return carry

    lax.fori_loop(0, h_ref.shape[0] // (2 * LANES), lane_block, 0)


def _select_tokens(q, kh_ref, kl_ref):
    tt = q.shape[0]
    k = PEER_TOPK
    row8 = lax.broadcasted_iota(jnp.int32, (SUBLANES, tt), 0)
    row16 = lax.broadcasted_iota(jnp.int32, (2 * SUBLANES, tt), 0)
    exp_blocks, wgt_blocks = [], []
    for h in range(PEER_HEADS):
        tops = []
        for p in range(2):
            col = (h * 2 + p) * PEER_HALF
            q_hi, q_lo = _split_bf16(q[:, col:col + PEER_HALF])
            s = _dot3(kh_ref[h, p], kl_ref[h, p], q_hi, q_lo, (((1,), (1,)), ((), ())))
            vals, ids = _extract_topk_pairs(s, k)
            tops.append((_assemble_rows(vals, k), _assemble_rows(ids, k)))
        (a, ia), (b, ib) = tops
        blocks, flat = [a[0:1] + b], [row16.astype(F32)]
        for i in range(1, 8):
            nj = k // (i + 1)
            blocks.append(jnp.where(row8 < nj, a[i:i + 1] + b[0:8], -jnp.inf))
            flat.append((row8 + i * k).astype(F32))
        blocks.append(a[8:16] + b[0:1])
        flat.append(((row8 + 8) * k).astype(F32))
        cand = jnp.concatenate(blocks, axis=0)
        cand_id = jnp.concatenate(flat, axis=0)
        vals, ids = _extract_topk_pairs(cand, k, cand_id)
        best = _assemble_rows(vals, k)
        fl = _assemble_rows(ids, k)
        fi = jnp.floor(fl * (1.0 / k))
        fj = fl - fi * k
        ei = jnp.zeros_like(fl)
        ej = jnp.zeros_like(fl)
        for r in range(k):
            ei = jnp.where(fi == r, ia[r:r + 1], ei)
            ej = jnp.where(fj == r, ib[r:r + 1], ej)
        e = jnp.exp(best - jnp.max(best, axis=0, keepdims=True))
        wgt_blocks.append(e / jnp.sum(e, axis=0, keepdims=True))
        exp_blocks.append((ei * PEER_NKEYS + ej) * ROW_WORDS)
    return (jnp.concatenate(exp_blocks, axis=0).T.astype(jnp.int32),
            jnp.concatenate(wgt_blocks, axis=0).T)


def peer_select(h, wq, keys, tt=512):
    t, d = h.shape
    tt = min(tt, t)
    assert t % tt == 0 and tt % LANES == 0
    nq = wq.shape[1]
    wq_hi, wq_lo = _split_bf16(wq)
    k_hi, k_lo = _split_bf16(keys)
    wspec = pl.BlockSpec((d, nq), lambda i: (0, 0))
    kspec = pl.BlockSpec(keys.shape, lambda i: (0, 0, 0, 0))
    return pl.pallas_call(
        _peer_select_kernel,
        grid=(t // tt,),
        in_specs=[pl.BlockSpec((tt, d), lambda i: (i, 0)), wspec, wspec, kspec, kspec],
        out_specs=[pl.BlockSpec((tt, PEER_SEL), lambda i: (i, 0)),
                   pl.BlockSpec((tt, PEER_SEL), lambda i: (i, 0))],
        out_shape=[jax.ShapeDtypeStruct((t, PEER_SEL), jnp.int32),
                   jax.ShapeDtypeStruct((t, PEER_SEL), F32)],
        scratch_shapes=[pltpu.VMEM((tt, nq), F32)],
        compiler_params=_cparams(1, 2 * (2 * d * nq + 2 * keys.size) * 2 + 3 * tt * (d + nq) * 4 + (8 << 20)),
        name="peer_select",
    )(h, wq_hi, wq_lo, k_hi, k_lo)


ROW_WORDS = D_MODEL // 2 // LANES


def _unpack_pair(w):
    lo = lax.bitcast_convert_type(lax.shift_left(w, jnp.int32(16)), F32)
    hi = lax.bitcast_convert_type(jnp.bitwise_and(w, jnp.int32(-65536)), F32)
    return lo, hi


PEER_CHUNK = 128
PEER_NCHUNK = PEER_SEL // PEER_CHUNK
PEER_GROUP = 8


def _rows_to_tiles(x8):
    row8 = lax.broadcasted_iota(jnp.int32, (SUBLANES, LANES), 0)
    tiles = []
    for tk in range(SUBLANES):
        tile = jnp.zeros((SUBLANES, LANES), x8.dtype)
        for r in range(SUBLANES):
            tile = jnp.where(row8 == r, x8[tk:tk + 1, r * LANES:(r + 1) * LANES], tile)
        tiles.append(tile)
    return tiles


def _tiles_to_rows(tiles):
    row8 = lax.broadcasted_iota(jnp.int32, (SUBLANES, LANES), 0)
    cols = []
    for r in range(SUBLANES):
        col = jnp.zeros((SUBLANES, LANES), tiles[0].dtype)
        for tk in range(SUBLANES):
            col = jnp.where(row8 == tk, tiles[tk][r:r + 1, :], col)
        cols.append(col)
    return jnp.concatenate(cols, axis=1)


def _peer_act_kernel(idx_ref, x_ref, wgt_ref, tab_ref, coef_ref, p_ref, xs_ref):
    tt = x_ref.shape[0]

    def group(g, carry):
        t0 = pl.multiple_of(g * PEER_GROUP, PEER_GROUP)
        for tk, tile in enumerate(_rows_to_tiles(x_ref[pl.ds(t0, PEER_GROUP), :])):
            xs_ref[tk] = tile

        def token(tk, c):
            xl = xs_ref[tk, 0:ROW_WORDS, :]
            xh = xs_ref[tk, ROW_WORDS:2 * ROW_WORDS, :]

            def chunk(ci, c2):
                base = ((t0 + tk) * PEER_NCHUNK + ci) * PEER_CHUNK
                for j in range(PEER_CHUNK):
                    r = pl.multiple_of(idx_ref[base + j], ROW_WORDS)
                    lo, hi = _unpack_pair(tab_ref[pl.ds(r, ROW_WORDS), :])
                    p_ref[tk * PEER_NCHUNK + ci, j * ROW_WORDS:(j + 1) * ROW_WORDS, :] = lo * xl + hi * xh
                return c2

            return lax.fori_loop(0, PEER_NCHUNK, chunk, c)

        lax.fori_loop(0, PEER_GROUP, token, 0)
        for tk in range(PEER_GROUP):
            tok = p_ref.at[tk * PEER_NCHUNK:(tk + 1) * PEER_NCHUNK]
            parts = [tok[:, pl.ds(r, PEER_CHUNK, stride=ROW_WORDS), :].reshape(PEER_SEL, LANES)
                     for r in range(ROW_WORDS)]
            per_lane = (parts[0] + parts[1]) + (parts[2] + parts[3])
            act = jnp.sum(per_lane.T, axis=0, keepdims=True)
            coef_ref[t0 + tk] = wgt_ref[t0 + tk] * _gelu_tanh(act)
        return carry

    lax.fori_loop(0, tt // PEER_GROUP, group, 0)


def _peer_out_kernel(idx_ref, coef_ref, res_ref, gate_ref, tab_ref, out_ref, ys_ref):
    tt = out_ref.shape[0]
    zero = jnp.zeros((ROW_WORDS, LANES), F32)

    def group(g, carry):
        t0 = pl.multiple_of(g * PEER_GROUP, PEER_GROUP)

        def token(tk, c):
            def chunk(ci, acc):
                acc = list(acc)
                base = ((t0 + tk) * PEER_NCHUNK + ci) * PEER_CHUNK
                for j in range(PEER_CHUNK):
                    r = pl.multiple_of(idx_ref[base + j], ROW_WORDS)
                    lo, hi = _unpack_pair(tab_ref[pl.ds(r, ROW_WORDS), :])
                    cf = coef_ref[base + j]
                    acc[2 * (j % 2)] = acc[2 * (j % 2)] + cf * lo
                    acc[2 * (j % 2) + 1] = acc[2 * (j % 2) + 1] + cf * hi
                return tuple(acc)

            acc = lax.fori_loop(0, PEER_NCHUNK, chunk, (zero, zero, zero, zero))
            ys_ref[tk, 0:ROW_WORDS, :] = acc[0] + acc[2]
            ys_ref[tk, ROW_WORDS:2 * ROW_WORDS, :] = acc[1] + acc[3]
            return c

        lax.fori_loop(0, PEER_GROUP, token, 0)
        rows = _tiles_to_rows([ys_ref[tk] for tk in range(PEER_GROUP)])
        out_ref[pl.ds(t0, PEER_GROUP), :] = res_ref[pl.ds(t0, PEER_GROUP), :] + gate_ref[...] * rows
        return carry

    lax.fori_loop(0, tt // PEER_GROUP, group, 0)


def _pack_table(tab):
    e, d = tab.shape
    pairs = jnp.moveaxis(tab.astype(BF16).reshape(e, 2, d // 2), 1, 2)
    return lax.bitcast_convert_type(pairs, jnp.int32).reshape(e * ROW_WORDS, LANES)


def _table_spec(shape):
    return pl.BlockSpec(shape, lambda i: (0, 0), pipeline_mode=pl.Buffered(1))


def peer_experts(h, expert, weight, u_pack, v_pack, res, gate, tokens_per_gate, tt=256):
    t, d = h.shape
    assert t % tt == 0 and tokens_per_gate % tt == 0 and tt % PEER_GROUP == 0
    assert PEER_GROUP == SUBLANES and d == 2 * ROW_WORDS * LANES
    vmem = u_pack.size * 4 + (16 << 20)
    smem_flat = pl.BlockSpec((tt * PEER_SEL,), lambda i: (i,), memory_space=pltpu.SMEM)
    xspec = pl.BlockSpec((tt, d), lambda i: (i, 0))
    rowspec = pl.BlockSpec((tt, 1, PEER_SEL), lambda i: (i, 0, 0))
    tile_scratch = pltpu.VMEM((PEER_GROUP, SUBLANES, LANES), F32)
    rows = expert.reshape(t * PEER_SEL)
    coef = pl.pallas_call(
        _peer_act_kernel,
        grid=(t // tt,),
        in_specs=[smem_flat, xspec, rowspec, _table_spec(u_pack.shape)],
        out_specs=rowspec,
        out_shape=jax.ShapeDtypeStruct((t, 1, PEER_SEL), F32),
        scratch_shapes=[pltpu.VMEM((PEER_GROUP * PEER_NCHUNK, PEER_CHUNK * ROW_WORDS, LANES), F32),
                        tile_scratch],
        compiler_params=_cparams(1, vmem),
        name="peer_act",
    )(rows, h, weight.reshape(t, 1, PEER_SEL), u_pack)
    return pl.pallas_call(
        _peer_out_kernel,
        grid=(t // tt,),
        in_specs=[smem_flat, smem_flat, xspec,
                  pl.BlockSpec((None, 1, d), lambda i: (i * tt // tokens_per_gate, 0, 0)),
                  _table_spec(v_pack.shape)],
        out_specs=xspec,
        out_shape=jax.ShapeDtypeStruct((t, d), F32),
        scratch_shapes=[tile_scratch],
        compiler_params=_cparams(1, vmem),
        name="peer_out",
    )(rows, coef.reshape(t * PEER_SEL), res, gate, v_pack)


def peer_residual(x, gate, h, wq, keys, u_pack, v_pack):
    b, n, d = h.shape
    hf = h.reshape(b * n, d)
    expert, weight = peer_select(hf, wq, keys)
    out = peer_experts(hf, expert, weight, u_pack, v_pack, x.reshape(b * n, d), gate, n)
    return out.reshape(b, n, d)


RG_CONV = 4


def _shift_rows(cur, halo, k, reverse):
    tb = cur.shape[0]
    row8 = lax.broadcasted_iota(jnp.int32, (SUBLANES, cur.shape[1]), 0)
    if not reverse:
        rolled = pltpu.roll(cur, k, axis=0)
        first = jnp.where(row8 < k, pltpu.roll(halo, k, axis=0), rolled[0:SUBLANES])
        return jnp.concatenate([first, rolled[SUBLANES:]], axis=0)
    rolled = pltpu.roll(cur, tb - k, axis=0)
    last = jnp.where(row8 >= SUBLANES - k, pltpu.roll(halo, SUBLANES - k, axis=0),
                     rolled[tb - SUBLANES:])
    return jnp.concatenate([rolled[:tb - SUBLANES], last], axis=0)


def _rglru_kernel(u_ref, h0_ref, cw_ref, cb_ref, wa_ref, ba_ref, wx_ref, bx_ref, lam_ref, *rest,
                  reverse, has_acc):
    if has_acc:
        acc_ref, out_ref, hlast_ref, a_s, b_s, hp_s, halo_s = rest
    else:
        out_ref, hlast_ref, a_s, b_s, hp_s, halo_s = rest
    tb, c = u_ref.shape

    @pl.when(pl.program_id(1) == 0)
    def _():
        hp_s[...] = h0_ref[...]
        halo_s[...] = jnp.zeros_like(halo_s)

    cur = u_ref[...]
    halo = halo_s[...]
    xc = cb_ref[...] + cw_ref[RG_CONV - 1:RG_CONV, :] * cur
    for k in range(1, RG_CONV):
        xc = xc + cw_ref[RG_CONV - 1 - k:RG_CONV - k, :] * _shift_rows(cur, halo, k, reverse)
    halo_s[...] = cur[0:SUBLANES] if reverse else cur[tb - SUBLANES:]

    xb = xc.astype(BF16)
    gate_r = _sigmoid(jnp.dot(xb, wa_ref[...], preferred_element_type=F32) + ba_ref[...])
    gate_i = _sigmoid(jnp.dot(xb, wx_ref[...], preferred_element_type=F32) + bx_ref[...])
    lam = lam_ref[...]
    softplus_neg = jnp.maximum(-lam, 0.0) + jnp.log(1.0 + jnp.exp(-jnp.abs(lam)))
    a = jnp.exp(-RG_C * gate_r * softplus_neg)
    a_s[...] = a
    b_s[...] = jnp.sqrt(1.0 - a * a) * (gate_i * xc)

    row8 = lax.broadcasted_iota(jnp.int32, (SUBLANES, c), 0)
    nt = tb // SUBLANES

    def step(j, hp):
        jj = nt - 1 - j if reverse else j
        r0 = pl.multiple_of(jj * SUBLANES, SUBLANES)
        av = a_s[pl.ds(r0, SUBLANES), :]
        bv = b_s[pl.ds(r0, SUBLANES), :]
        for k in (1, 2, 4):
            if reverse:
                ok = row8 < SUBLANES - k
                sh = SUBLANES - k
            else:
                ok = row8 >= k
                sh = k
            a_prev = jnp.where(ok, pltpu.roll(av, sh, axis=0), 1.0)
            b_prev = jnp.where(ok, pltpu.roll(bv, sh, axis=0), 0.0)
            bv = av * b_prev + bv
            av = av * a_prev
        h = av * hp + bv
        if has_acc:
            out_ref[pl.ds(r0, SUBLANES), :] = h + acc_ref[pl.ds(r0, SUBLANES), :]
        else:
            out_ref[pl.ds(r0, SUBLANES), :] = h
        return h[0:1] if reverse else h[SUBLANES - 1:SUBLANES]

    hp = lax.fori_loop(0, nt, step, hp_s[...])
    hp_s[...] = hp
    hlast_ref[...] = hp


def rglru_scan(u, h0, p, reverse, acc=None, tb=512):
    b, n, c = u.shape
    tb = min(tb, n)
    assert n % tb == 0
    nblk = n // tb
    if reverse:
        seq = pl.BlockSpec((None, tb, c), lambda bi, i: (bi, nblk - 1 - i, 0))
    else:
        seq = pl.BlockSpec((None, tb, c), lambda bi, i: (bi, i, 0))
    state = pl.BlockSpec((None, 1, c), lambda bi, i: (bi, 0, 0))

    def par(a):
        return pl.BlockSpec(a.shape, lambda bi, i: (0,) * a.ndim)

    params = [p["conv_w"], p["conv_b"], p["wa"], p["ba"], p["wx"], p["bx"], p["lam"]]
    in_specs = [seq, state] + [par(a) for a in params]
    args = [u, h0] + params
    if acc is not None:
        in_specs.append(seq)
        args.append(acc)
    return pl.pallas_call(
        functools.partial(_rglru_kernel, reverse=reverse, has_acc=acc is not None),
        grid=(b, nblk),
        in_specs=in_specs,
        out_specs=[seq, state],
        out_shape=[jax.ShapeDtypeStruct((b, n, c), F32), jax.ShapeDtypeStruct((b, 1, c), F32)],
        scratch_shapes=[pltpu.VMEM((tb, c), F32), pltpu.VMEM((tb, c), F32),
                        pltpu.VMEM((1, c), F32), pltpu.VMEM((SUBLANES, c), F32)],
        compiler_params=_cparams(2),
        name="rglru_bwd" if reverse else "rglru_fwd",
    )(*args)


def _block_diag(w):
    g, bs, _ = w.shape
    eye = jnp.eye(g, dtype=w.dtype)
    return (eye[:, None, :, None] * w[:, :, None, :]).reshape(g * bs, g * bs)


def rglru_params(conv_w, conv_b, wa, ba, wx, bx, lam, d):
    c = conv_b.shape[-1]
    return {"conv_w": conv_w[d], "conv_b": conv_b[d].reshape(1, c),
            "wa": _block_diag(wa[d]).astype(BF16), "ba": ba[d].reshape(1, c),
            "wx": _block_diag(wx[d]).astype(BF16), "bx": bx[d].reshape(1, c),
            "lam": lam[d].reshape(1, c)}


def rglru_mix(u_ctx, u_lat, params):
    b, _, c = u_ctx.shape
    zero = jnp.zeros((b, 1, c), F32)
    h_ctx = h_lat = None
    for d in range(2):
        h_ctx, last = rglru_scan(u_ctx, zero, params[d], reverse=bool(d), acc=h_ctx)
        h_lat, _ = rglru_scan(u_lat, last, params[d], reverse=bool(d), acc=h_lat)
    return h_ctx, h_lat


def _gla_kernel(x_ref, s0_ref, wlr_ref, blr_ref, *rest, reverse, has_acc):
    if has_acc:
        acc_ref, o_ref, s_out_ref, st_s = rest
    else:
        o_ref, s_out_ref, st_s = rest
    nb, tb = x_ref.shape[0], x_ref.shape[1]
    ch = GLA_CHUNK

    @pl.when(pl.program_id(1) == 0)
    def _():
        st_s[...] = s0_ref[...]

    r_i = lax.broadcasted_iota(jnp.int32, (ch, ch), 0)
    c_i = lax.broadcasted_iota(jnp.int32, (ch, ch), 1)
    if reverse:
        cum_mat = (c_i >= r_i).astype(BF16)
        keep = c_i > r_i
    else:
        cum_mat = (c_i <= r_i).astype(BF16)
        keep = c_i <= r_i
    lane = lax.broadcasted_iota(jnp.int32, (1, LANES), 1)
    head_lanes = (lane < GLA_DK, lane >= GLA_DK)
    nt_dims = (((1,), (1,)), ((), ()))
    chunks = range(tb // ch)
    steps = [(cidx, bb) for cidx in (reversed(chunks) if reverse else chunks) for bb in range(nb)]
    for cidx, bb in steps:
        r0 = cidx * ch
        q = x_ref[bb, r0:r0 + ch, 0:GLA_QK] * (GLA_DK ** -0.5)
        k = x_ref[bb, r0:r0 + ch, GLA_QK:2 * GLA_QK]
        v = x_ref[bb, r0:r0 + ch, 2 * GLA_QK:2 * GLA_QK + GLA_V]
        lr = x_ref[bb, r0:r0 + ch, 2 * GLA_QK + GLA_V:GLA_IN_PAD]
        logits = jnp.dot(lr, wlr_ref[...], precision=HIGHEST, preferred_element_type=F32) + blr_ref[...]
        log_a = _log_sigmoid(logits) * (1.0 / GLA_TAU)
        la_hi = log_a.astype(BF16)
        la_mid, la_lo = _split_bf16(log_a - la_hi.astype(F32))
        cum = (jnp.dot(cum_mat, la_hi, preferred_element_type=F32)
               + (jnp.dot(cum_mat, la_mid, preferred_element_type=F32)
                  + jnp.dot(cum_mat, la_lo, preferred_element_type=F32)))
        tot = cum[0:1] if reverse else cum[ch - 1:ch]
        qg = q * jnp.exp(cum)
        kg = k * jnp.exp(-cum)
        kd = k * jnp.exp(tot - cum)
        decay = jnp.exp(tot)
        outs = []
        for h in range(GLA_HEADS):
            sl = slice((h // 2) * LANES, (h // 2 + 1) * LANES)
            mine = head_lanes[h % 2]
            qm = jnp.where(mine, qg[:, sl], 0.0).astype(BF16)
            scores = lax.dot_general(qm, kg[:, sl].astype(BF16), nt_dims, preferred_element_type=F32)
            scores = jnp.where(keep, scores, 0.0)
            vh = v[:, h * GLA_DV:(h + 1) * GLA_DV]
            st = st_s[bb, h]
            o = jnp.dot(scores.astype(BF16), vh.astype(BF16), preferred_element_type=F32)
            o = o + lax.dot_general(qm, st.astype(BF16), nt_dims, preferred_element_type=F32)
            kdm = jnp.where(mine, kd[:, sl], 0.0).astype(BF16)
            st_s[bb, h] = st * decay[:, sl] + jnp.dot(vh.T.astype(BF16), kdm, preferred_element_type=F32)
            outs.append(o)
        o_all = jnp.concatenate(outs, axis=1)
        if has_acc:
            o_all = o_all + acc_ref[bb, r0:r0 + ch, :]
        o_ref[bb, r0:r0 + ch, :] = o_all
    s_out_ref[...] = st_s[...]


def gla_scan(x, s0, wlr, blr, reverse, acc=None, tb=512, nb=2):
    b, n, cin = x.shape
    tb = min(tb, n)
    nb = min(nb, b)
    assert n % tb == 0 and tb % GLA_CHUNK == 0 and b % nb == 0
    nblk = n // tb

    def seq(width):
        if reverse:
            return pl.BlockSpec((nb, tb, width), lambda bi, i: (bi, nblk - 1 - i, 0))
        return pl.BlockSpec((nb, tb, width), lambda bi, i: (bi, i, 0))

    state = pl.BlockSpec((nb, GLA_HEADS, GLA_DV, LANES), lambda bi, i: (bi, 0, 0, 0))
    in_specs = [seq(cin), state,
                pl.BlockSpec(wlr.shape, lambda bi, i: (0, 0)), pl.BlockSpec(blr.shape, lambda bi, i: (0, 0))]
    args = [x, s0, wlr, blr]
    if acc is not None:
        in_specs.append(seq(GLA_V))
        args.append(acc)
    return pl.pallas_call(
        functools.partial(_gla_kernel, reverse=reverse, has_acc=acc is not None),
        grid=(b // nb, nblk),
        in_specs=in_specs,
        out_specs=[seq(GLA_V), state],
        out_shape=[jax.ShapeDtypeStruct((b, n, GLA_V), F32),
                   jax.ShapeDtypeStruct((b, GLA_HEADS, GLA_DV, LANES), F32)],
        scratch_shapes=[pltpu.VMEM((nb, GLA_HEADS, GLA_DV, LANES), F32)],
        compiler_params=_cparams(2, 40 << 20),
        name="gla_bwd" if reverse else "gla_fwd",
    )(*args)


def gla_params(w_lr, b_lr, d):
    w = jnp.zeros((LANES, GLA_QK), F32).at[d * GLA_RANK:(d + 1) * GLA_RANK].set(w_lr[d])
    return w, b_lr[d].reshape(1, GLA_QK)


def gla_mix(x_ctx, x_lat_cols, w_lr, b_lr):
    b = x_ctx.shape[0]
    zero = jnp.zeros((b, GLA_HEADS, GLA_DV, LANES), F32)
    o_ctx = o_lat = None
    for d in range(2):
        w, bias = gla_params(w_lr, b_lr, d)
        o_ctx, s = gla_scan(x_ctx, zero, w, bias, reverse=bool(d), acc=o_ctx)
        o_lat, _ = gla_scan(x_lat_cols, s, w, bias, reverse=bool(d), acc=o_lat)
    return o_ctx, o_lat


def raster_to_column(t):
    b, n = t.shape[:2]
    return t.reshape(b, n // GRID_W, GRID_W, *t.shape[2:]).swapaxes(1, 2).reshape(t.shape)


def column_to_raster(t):
    b, n = t.shape[:2]
    return t.reshape(b, GRID_W, n // GRID_W, *t.shape[2:]).swapaxes(1, 2).reshape(t.shape)


def _merge_kernel(x_ref, gt_ref, h_ref, yhy_ref, hrg_ref, ogla_ref,
                  wg_ref, bm_ref, gn_ref, why_ref, wrg_ref, wgla_ref, wout_ref, o_ref):
    d = x_ref.shape[1]

    def proj(y, w_ref):
        return jnp.dot(y.astype(BF16), w_ref[...], preferred_element_type=F32)

    gates_in = proj(h_ref[...], wg_ref)
    y_rg = hrg_ref[...] * _gelu_tanh(gates_in[:, 0:RG_W])
    o = ogla_ref[...]
    heads = []
    for h in range(GLA_HEADS):
        oh = o[:, h * GLA_DV:(h + 1) * GLA_DV]
        heads.append(oh * lax.rsqrt(jnp.mean(oh * oh, axis=-1, keepdims=True) + NORM_EPS) * gn_ref[...])
    gg = gates_in[:, RG_W:RG_W + GLA_V]
    y_gla = jnp.concatenate(heads, axis=1) * (gg * _sigmoid(gg))
    gate = _sigmoid(gates_in[:, RG_W + GLA_V:] + bm_ref[...])
    m = (gate[:, 0:d] * proj(yhy_ref[...], why_ref) + gate[:, d:2 * d] * proj(y_rg, wrg_ref)
         + gate[:, 2 * d:3 * d] * proj(y_gla, wgla_ref))
    o_ref[...] = x_ref[...] + gt_ref[...] * proj(m, wout_ref)


def merge_residual(x, gt, h, y_hy, h_rg, o_gla, lw, tm=256):
    b, n, d = x.shape
    tm = min(tm, n)
    assert n % tm == 0

    def seq(a):
        return pl.BlockSpec((None, tm, a.shape[2]), lambda bi, i: (bi, i, 0))

    def par(a):
        return pl.BlockSpec(a.shape, lambda bi, i: (0,) * a.ndim, pipeline_mode=pl.Buffered(1))

    streams = [h, y_hy, h_rg, o_gla]
    params = [lw["w_gates"], lw["b_merge"], lw["gla_norm_g"], lw["w_hy_o"], lw["w_rg_o"],
              lw["w_gla_o"], lw["w_out"]]
    return pl.pallas_call(
        _merge_kernel,
        grid=(b, n // tm),
        in_specs=[seq(x), pl.BlockSpec((None, 1, d), lambda bi, i: (bi, 0, 0))]
                 + [seq(a) for a in streams] + [par(a) for a in params],
        out_specs=seq(x),
        out_shape=jax.ShapeDtypeStruct((b, n, d), F32),
        compiler_params=_cparams(2, 48 << 20),
        name="merge_residual",
    )(x, gt, *streams, *params)


def _shortconv_kernel(x_ref, w_ref, b_ref, o_ref, *, rows):
    n, c = x_ref.shape
    zero = jnp.zeros((SUBLANES, c), F32)
    for r0 in range(0, n, rows):
        cur = x_ref[r0:r0 + rows, :]
        before = x_ref[r0 - SUBLANES:r0, :] if r0 > 0 else zero
        after = x_ref[r0 + rows:r0 + rows + SUBLANES, :] if r0 + rows < n else zero
        o_ref[r0:r0 + rows, :] = (b_ref[...] + w_ref[0:1, :] * _shift_rows(cur, before, 1, False)
                                  + w_ref[1:2, :] * cur + w_ref[2:3, :] * _shift_rows(cur, after, 1, True))


def shortconv(x, w, bias):
    b, n, c = x.shape
    blk = pl.BlockSpec((None, n, LANES), lambda bi, j: (bi, 0, j))
    return pl.pallas_call(
        functools.partial(_shortconv_kernel, rows=min(n, 1024)),
        grid=(b, c // LANES),
        in_specs=[blk, pl.BlockSpec((3, LANES), lambda bi, j: (0, j)),
                  pl.BlockSpec((1, LANES), lambda bi, j: (0, j))],
        out_specs=blk,
        out_shape=jax.ShapeDtypeStruct((b, n, c), F32),
        compiler_params=_cparams(2, 40 << 20),
        name="shortconv",
    )(x, w, bias.reshape(1, c))


def _hy_taps_kernel(w1_ref, b1_ref, w2_ref, b2_ref, w3_ref, freq_ref, band_ref, delta_ref,
                    taps_ref, asum_ref, *, n):
    tb = taps_ref.shape[1]
    i = pl.program_id(0)
    m = i * tb + lax.broadcasted_iota(jnp.int32, (tb, LANES), 0)
    idx = jnp.where(m < n, m, 2 * n - m).astype(F32)
    lane = lax.broadcasted_iota(jnp.int32, (tb, LANES), 1)
    tn = idx / (n - 1)
    ang = (2.0 * math.pi / n) * idx * band_ref[...]
    feats = jnp.where(lane == 0, tn,
                      jnp.where(lane <= HY_BANDS, jnp.cos(ang),
                                jnp.where(lane <= 2 * HY_BANDS, -jnp.sin(ang), 0.0)))
    fr = freq_ref[...]

    def dense(x, w_ref):
        return jnp.dot(x, w_ref[...], precision=HIGHEST, preferred_element_type=F32)

    h = jnp.sin(fr * (dense(feats, w1_ref) + b1_ref[...]))
    h = jnp.sin(fr * (dense(h, w2_ref) + b2_ref[...]))
    h = dense(h, w3_ref) * jnp.exp(-tn[:, 0:1] * delta_ref[...])
    row = m[:, 0:1]
    use_fwd = row < n
    use_bwd = (row > n) | (row == 0)
    sums = []
    for o in range(2):
        hf = jnp.where(use_fwd, h[:, (2 * o) * HY_W:(2 * o + 1) * HY_W], 0.0)
        hb = jnp.where(use_bwd, h[:, (2 * o + 1) * HY_W:(2 * o + 2) * HY_W], 0.0)
        taps_ref[o] = hf + hb
        sums += [jnp.sum(jnp.abs(hf), axis=0, keepdims=True), jnp.sum(jnp.abs(hb), axis=0, keepdims=True)]

    @pl.when(i == 0)
    def _():
        asum_ref[...] = jnp.zeros_like(asum_ref)

    asum_ref[...] += jnp.concatenate(sums, axis=1)


def hyena_taps(n, w1, b1, w2, b2, w3, freq, tb=256):
    nout = w3.shape[1]
    assert nout == 4 * HY_W and (2 * n) % tb == 0
    bands = np.zeros((1, LANES), np.float32)
    lin = np.linspace(1e-4, HY_BANDS - 1, HY_BANDS, dtype=np.float32)
    bands[0, 1:1 + HY_BANDS] = lin
    bands[0, 1 + HY_BANDS:1 + 2 * HY_BANDS] = lin
    deltas = np.abs(np.linspace(HY_FAST_RATE, HY_SLOW_RATE, HY_W, dtype=np.float32))
    deltas = np.tile(deltas, nout // HY_W).reshape(1, nout)
    w1p = jnp.zeros((LANES, HY_FFN), F32).at[:w1.shape[0]].set(w1)
    params = [w1p, b1.reshape(1, -1), w2, b2.reshape(1, -1), w3, freq.reshape(1, -1),
              jnp.asarray(bands), jnp.asarray(deltas)]
    return pl.pallas_call(
        functools.partial(_hy_taps_kernel, n=n),
        grid=(2 * n // tb,),
        in_specs=[pl.BlockSpec(a.shape, lambda i: (0, 0)) for a in params],
        out_specs=[pl.BlockSpec((2, tb, HY_W), lambda i: (0, i, 0)), pl.BlockSpec((1, nout), lambda i: (0, 0))],
        out_shape=[jax.ShapeDtypeStruct((2, 2 * n, HY_W), F32), jax.ShapeDtypeStruct((1, nout), F32)],
        compiler_params=_cparams(1),
        name="hyena_taps",
    )(*params)


def _cis(num, den):
    ang = (2.0 * math.pi / den) * (num % den).astype(F32)
    return jnp.cos(ang), jnp.sin(ang)


def _dft_tables(n):
    big = 2 * n
    q = int(round(math.sqrt(big)))
    assert q * q == big and q % (2 * SUBLANES) == 0
    ar = jnp.arange(q, dtype=jnp.int32)
    num = ar[None, :, None] * (q * ar[None, None, :] + ar[:, None, None])
    c, s = _cis(num, big)
    w1 = jnp.concatenate([c, -s], axis=1)
    ct, st = jnp.swapaxes(c, 1, 2)[:, :q // 2], jnp.swapaxes(s, 1, 2)[:, :q // 2]
    v = jnp.concatenate([ct, -st], axis=2) * (1.0 / big)
    c2, s2 = _cis(ar[:, None] * ar[None, :], q)
    f2 = jnp.concatenate([jnp.concatenate([c2, s2], axis=1), jnp.concatenate([-s2, c2], axis=1)], axis=0)
    g2 = jnp.concatenate([jnp.concatenate([c2, -s2], axis=1), jnp.concatenate([s2, c2], axis=1)], axis=0)
    return {"q": q, "w1": w1, "v": v, "f2": f2, "g2": g2}


def _level_kernel(w_ref, x_ref, *rest, has_gate, precise):
    if has_gate:
        src_ref, gate_ref, skip_ref = rest[:3]
    o_ref = rest[-1]
    for s in range(w_ref.shape[0]):
        x = x_ref[s]
        if precise:
            y = jnp.dot(w_ref[s], x, precision=HIGHEST, preferred_element_type=F32)
        else:
            y = jnp.dot(w_ref[s], x.astype(BF16), preferred_element_type=F32)
        if has_gate:
            y = gate_ref[s] * (y + src_ref[s] * skip_ref[...])
        o_ref[s] = y.astype(o_ref.dtype)


def dft_level(w, x, x_group=0, gate=None, out_dtype=F32, precise=False, sb=8):
    b, q, k, _ = x.shape
    m = w.shape[1]
    c = HY_W
    sb = min(sb, q)

    def seq(rows, group):
        return pl.BlockSpec((None, sb, rows, c), lambda i, bi: (bi, i, 0, group))

    in_specs = [pl.BlockSpec((sb, m, k), lambda i, bi: (i, 0, 0)), seq(k, x_group)]
    args = [w, x]
    if gate is not None:
        src, src_group, gates, gate_group, skip = gate
        in_specs += [seq(m, src_group), seq(m, gate_group), pl.BlockSpec((1, c), lambda i, bi: (0, 0))]
        args += [src, gates, skip]
    return pl.pallas_call(
        functools.partial(_level_kernel, has_gate=gate is not None, precise=precise),
        grid=(q // sb, b),
        in_specs=in_specs,
        out_specs=seq(m, 0),
        out_shape=jax.ShapeDtypeStruct((b, q, m, c), out_dtype),
        compiler_params=_cparams(2, 40 << 20),
        name="dft_level",
    )(*args)


def _dft_mid_kernel(a_ref, h_ref, f_ref, g_ref, o_ref):
    q = a_ref.shape[1] // 2
    for j in range(a_ref.shape[0]):
        x = jnp.dot(f_ref[...], a_ref[j], preferred_element_type=F32)
        xr, xi = x[:q], x[q:]
        hr, hi = h_ref[j, :q, :], h_ref[j, q:, :]
        y = jnp.concatenate([xr * hr - xi * hi, xr * hi + xi * hr], axis=0)
        o_ref[j] = jnp.dot(g_ref[...], y.astype(BF16), preferred_element_type=F32).astype(o_ref.dtype)


def dft_mid(a, spec, f2, g2, kb=8):
    b, q, q2, c = a.shape
    blk = pl.BlockSpec((None, kb, q2, c), lambda i, bi: (bi, i, 0, 0))
    mat = pl.BlockSpec((q2, q2), lambda i, bi: (0, 0))
    return pl.pallas_call(
        _dft_mid_kernel,
        grid=(q // kb, b),
        in_specs=[blk, pl.BlockSpec((kb, q2, c), lambda i, bi: (i, 0, 0)), mat, mat],
        out_specs=blk,
        out_shape=jax.ShapeDtypeStruct(a.shape, BF16),
        compiler_params=_cparams(2, 40 << 20),
        name="dft_mid",
    )(a, spec, f2, g2)


def _dft_spec_kernel(a_ref, f_ref, asum_ref, o_ref):
    inv = 1.0 / (asum_ref[0:1, :] + asum_ref[1:2, :] + 1e-6)
    for j in range(a_ref.shape[0]):
        o_ref[j] = jnp.dot(f_ref[...], a_ref[j], precision=HIGHEST, preferred_element_type=F32) * inv


def dft_spec(a, f2, asum, kb=8):
    o, q, q2, c = a.shape
    blk = pl.BlockSpec((None, kb, q2, c), lambda i, oi: (oi, i, 0, 0))
    return pl.pallas_call(
        _dft_spec_kernel,
        grid=(q // kb, o),
        in_specs=[blk, pl.BlockSpec((q2, q2), lambda i, oi: (0, 0)),
                  pl.BlockSpec((None, 2, c), lambda i, oi: (oi, 0, 0))],
        out_specs=blk,
        out_shape=jax.ShapeDtypeStruct(a.shape, F32),
        compiler_params=_cparams(2, 40 << 20),
        name="dft_spec",
    )(a, f2, asum)


def _swap_levels(a):
    b, q, q2, c = a.shape
    return a.reshape(b, q, 2, q, c).transpose(0, 3, 2, 1, 4).reshape(b, q, q2, c)


def hyena_long(u, taps, asum, skip, tabs):
    b, n, _ = u.shape
    q = tabs["q"]
    c = HY_W

    def to_levels(t, rows):
        return t.reshape(t.shape[0], rows, q, t.shape[2]).swapaxes(1, 2)

    a = dft_level(tabs["w1"], to_levels(taps, q), precise=True)
    spec = dft_spec(_swap_levels(a), tabs["f2"], asum.reshape(2, 2, c))
    w1d = tabs["w1"][:, :, :q // 2].astype(BF16)
    vd = tabs["v"].astype(BF16)
    f2, g2 = tabs["f2"].astype(BF16), tabs["g2"].astype(BF16)
    u_t = to_levels(u, q // 2)
    src, group = u_t, 0
    for order in range(2):
        a = dft_level(w1d, src, x_group=group, out_dtype=BF16)
        cm = dft_mid(_swap_levels(a), spec[order], f2, g2)
        src = dft_level(vd, _swap_levels(cm),
                        gate=(src, group, u_t, order + 1, skip[order].reshape(1, c)))
        group = 0
    return src.swapaxes(1, 2).reshape(b, n, c)


def _dft_small_kernel(u_ref, gate_ref, skip_ref, h_ref, f_ref, g_ref, o_ref):
    u = u_ref[...]
    nb = h_ref.shape[0] // 2
    x = jnp.dot(f_ref[...], u, precision=HIGHEST, preferred_element_type=F32)
    xr, xi = x[:nb], x[nb:]
    hr, hi = h_ref[:nb, :], h_ref[nb:, :]
    y = jnp.concatenate([xr * hr - xi * hi, xr * hi + xi * hr], axis=0)
    conv = jnp.dot(g_ref[...], y, precision=HIGHEST, preferred_element_type=F32)
    o_ref[...] = gate_ref[...] * (conv + u * skip_ref[...])


def _dft_small_spec_kernel(taps_ref, f_ref, asum_ref, o_ref):
    inv = 1.0 / (asum_ref[0:1, :] + asum_ref[1:2, :] + 1e-6)
    o_ref[...] = jnp.dot(f_ref[...], taps_ref[...], precision=HIGHEST,
                         preferred_element_type=F32) * inv


def hyena_short_seq(u, taps, asum, skip):
    b, n, _ = u.shape
    big = 2 * n
    c = HY_W
    ar = jnp.arange(big, dtype=jnp.int32)
    cs, sn = _cis(ar[:, None] * ar[None, :], big)
    f_full = jnp.concatenate([cs, -sn], axis=0)
    g_half = jnp.concatenate([cs[:n], -sn[:n]], axis=1) * (1.0 / big)
    tap = pl.BlockSpec((None, big, c), lambda o: (o, 0, 0))
    spec = pl.pallas_call(
        _dft_small_spec_kernel,
        grid=(2,),
        in_specs=[tap, pl.BlockSpec((2 * big, big), lambda o: (0, 0)),
                  pl.BlockSpec((None, 2, c), lambda o: (o, 0, 0))],
        out_specs=pl.BlockSpec((None, 2 * big, c), lambda o: (o, 0, 0)),
        out_shape=jax.ShapeDtypeStruct((2, 2 * big, c), F32),
        compiler_params=_cparams(1),
        name="dft_small_spec",
    )(taps, f_full, asum.reshape(2, 2, c))
    f_data = f_full[:, :n]
    src, group = u, 0
    for order in range(2):
        src = pl.pallas_call(
            _dft_small_kernel,
            grid=(b,),
            in_specs=[pl.BlockSpec((None, n, c), functools.partial(lambda bi, g: (bi, 0, g), g=group)),
                      pl.BlockSpec((None, n, c), functools.partial(lambda bi, g: (bi, 0, g), g=order + 1)),
                      pl.BlockSpec((1, c), lambda bi: (0, 0)),
                      pl.BlockSpec((2 * big, c), lambda bi: (0, 0)),
                      pl.BlockSpec((2 * big, n), lambda bi: (0, 0)),
                      pl.BlockSpec((n, 2 * big), lambda bi: (0, 0))],
            out_specs=pl.BlockSpec((None, n, c), lambda bi: (bi, 0, 0)),
            out_shape=jax.ShapeDtypeStruct((b, n, c), F32),
            compiler_params=_cparams(1),
            name="dft_small",
        )(src, u, skip[order].reshape(1, c), spec[order], f_data, g_half)
        group = 0
    return src


HY_IN = 3 * HY_W
IN_GROUPS = (HY_IN, RG_W, RG_W, GLA_IN, GLA_V, 3 * D_MODEL)


def _split_w_in(w):
    parts, start = [], 0
    for width in IN_GROUPS:
        parts.append(w[:, start:start + width])
        start += width
    parts[3] = jnp.pad(parts[3], ((0, 0), (0, GLA_IN_PAD - GLA_IN)))
    return [p.astype(BF16) for p in parts]


def _project(h, w):
    b, n, d = h.shape
    return matmul(h.reshape(b * n, d), w).reshape(b, n, w.shape[1])


def kernel(x, c, ctx, c_ctx, w_mod, b_mod, g_norm_mix, g_norm_ffn, w_in, hy_conv_w, hy_conv_b,
           hy_w1, hy_b1, hy_w2, hy_b2, hy_w3, hy_freq, hy_skip, rg_conv_w, rg_conv_b, rg_wa, rg_ba,
           rg_wx, rg_bx, rg_lambda, gla_w_lr, gla_b_lr, gla_norm_g, w_hy_o, w_rg_o, w_gla_o, b_merge,
           w_out, peer_wq, peer_keys, peer_u, peer_v, g_final):
    b, n, d = x.shape
    n_ctx = ctx.shape[1]
    depth = w_mod.shape[0]
    cond = jnp.concatenate([c, c_ctx[None, :]], axis=0)
    cond = jnp.pad(cond, ((0, -(b + 1) % SUBLANES), (0, 0)))
    tabs = _dft_tables(n)
    x_lat, x_ctx = x, ctx
    for l in range(depth):
        need_ctx = l < depth - 1
        mod = matmul(cond, w_mod[l].astype(BF16), bias=b_mod[l], silu_in=True)
        sh1, sc1, gt1, sh2, sc2, gt2 = [mod[:b, i * d:(i + 1) * d].reshape(b, 1, d) for i in range(6)]
        csh1, csc1, cgt1, csh2, csc2, cgt2 = [
            jnp.broadcast_to(mod[b:b + 1, i * d:(i + 1) * d].reshape(1, 1, d), (b, 1, d)) for i in range(6)]
        w_hy, w_rgx, w_rgg, w_gla, w_glag, w_mg = _split_w_in(w_in[l])
        lw = {"b_merge": b_merge[l].reshape(1, 3 * d), "gla_norm_g": gla_norm_g[l].reshape(1, GLA_DV),
              "w_gates": jnp.concatenate([w_rgg, w_glag, w_mg], axis=1),
              "w_hy_o": w_hy_o[l].astype(BF16), "w_rg_o": w_rg_o[l].astype(BF16),
              "w_gla_o": w_gla_o[l].astype(BF16), "w_out": w_out[l].astype(BF16)}
        rg_par = [rglru_params(rg_conv_w[l], rg_conv_b[l], rg_wa[l], rg_ba[l], rg_wx[l], rg_bx[l],
                               rg_lambda[l], dd) for dd in range(2)]
        filt = (hy_w1[l], hy_b1[l], hy_w2[l], hy_b2[l], hy_w3[l], hy_freq[l])
        u_pack, v_pack = _pack_table(peer_u[l]), _pack_table(peer_v[l])

        h_lat = normmod(x_lat, g_norm_mix[l], sh1, sc1, BF16)
        h_ctx = normmod(x_ctx, g_norm_mix[l], csh1, csc1, BF16)
        hy_l, rgx_l, gla_l = [_project(h_lat, w) for w in (w_hy, w_rgx, w_gla)]
        rgx_c, gla_c = [_project(h_ctx, w) for w in (w_rgx, w_gla)]

        taps, asum = hyena_taps(n, *filt)
        y_hy_l = hyena_long(shortconv(hy_l, hy_conv_w[l], hy_conv_b[l]), taps, asum, hy_skip[l], tabs)
        h_rg_c, h_rg_l = rglru_mix(rgx_c, rgx_l, rg_par)
        o_gla_c, o_gla_l = gla_mix(gla_c, raster_to_column(gla_l), gla_w_lr[l], gla_b_lr[l])
        x_lat = merge_residual(x_lat, gt1, h_lat, y_hy_l, h_rg_l, column_to_raster(o_gla_l), lw)
        h2 = normmod(x_lat, g_norm_ffn[l], sh2, sc2, F32)
        x_lat = peer_residual(x_lat, gt2, h2, peer_wq[l], peer_keys[l], u_pack, v_pack)
        if need_ctx:
            taps_c, asum_c = hyena_taps(n_ctx, *filt)
            y_hy_c = hyena_short_seq(shortconv(_project(h_ctx, w_hy), hy_conv_w[l], hy_conv_b[l]),
                                     taps_c, asum_c, hy_skip[l])
            x_ctx = merge_residual(x_ctx, cgt1, h_ctx, y_hy_c, h_rg_c, o_gla_c, lw)
            h2c = normmod(x_ctx, g_norm_ffn[l], csh2, csc2, F32)
            x_ctx = peer_residual(x_ctx, cgt2, h2c, peer_wq[l], peer_keys[l], u_pack, v_pack)
    zero = jnp.zeros((b, 1, d), F32)
    return normmod(x_lat, g_final, zero, zero, F32, mod=False)
```
